```python
import math
import jax, jax.numpy as jnp
from jax import lax
import numpy as np

D_MODEL = 1024
BATCH = 2
SEQ = 8192
DEPTH = 1

CHUNK = 64
Q_BLOCK = 128
N_MEM = 256
ROPE_THETA = 10000.0
EPS = 1e-5

DA_HEADS = 4
DA_HEAD_DIM = 64
DA_V_DIM = 2 * DA_HEAD_DIM
DA_WIDTH = DA_HEADS * DA_V_DIM

CONV_CH = 256
CONV_WIDTH = 31

MEM_HEADS = 4
MEM_HEAD_DIM = 64
MEM_WIDTH = MEM_HEADS * MEM_HEAD_DIM

MIX_WIDTH = DA_WIDTH + CONV_CH + MEM_WIDTH
IN_Q = DA_HEADS * 2 * DA_HEAD_DIM
IN_K = DA_HEADS * 2 * DA_HEAD_DIM
IN_V = DA_WIDTH
IN_CONV = 2 * CONV_CH
IN_MEMQ = MEM_WIDTH
IN_WIDTH = IN_Q + IN_K + IN_V + IN_CONV + IN_MEMQ

D_FF = 2816

kernel_name = "hybrid_diffattn_conformer_conv_memxattn_macaron"


def rmsnorm(x, g):
    xf = x.astype(jnp.float32)
    y = xf * lax.rsqrt(jnp.mean(xf * xf, axis=-1, keepdims=True) + EPS)
    return (y * g.astype(jnp.float32)).astype(x.dtype)


def layernorm(x, g, b):
    xf = x.astype(jnp.float32)
    mu = jnp.mean(xf, axis=-1, keepdims=True)
    var = jnp.mean(jnp.square(xf - mu), axis=-1, keepdims=True)
    y = (xf - mu) * lax.rsqrt(var + EPS)
    return (y * g.astype(jnp.float32) + b.astype(jnp.float32)).astype(x.dtype)


def swiglu(x, w_gate, w_up, w_down):
    return (jax.nn.silu(x @ w_gate) * (x @ w_up)) @ w_down


def rope_tables(seq, dim):
    inv_freq = 1.0 / (ROPE_THETA ** (jnp.arange(0, dim, 2, dtype=jnp.float32) / dim))
    ang = jnp.arange(seq, dtype=jnp.float32)[:, None] * inv_freq[None, :]
    return jnp.cos(ang), jnp.sin(ang)


def apply_rope(x, cos, sin):
    c = cos[None, :, None, None, :].astype(x.dtype)
    s = sin[None, :, None, None, :].astype(x.dtype)
    x1, x2 = jnp.split(x, 2, axis=-1)
    return jnp.concatenate([x1 * c - x2 * s, x2 * c + x1 * s], axis=-1)


def differential_attention(q, k, v, lam, subln_g, lam_init):
    b, s, h, _, d = q.shape
    nb = s // Q_BLOCK
    scale = d ** -0.5
    kf = k.astype(jnp.float32)
    vf = v.astype(jnp.float32)
    key_chunk = jnp.arange(s) // CHUNK
    qb = q.reshape(b, nb, Q_BLOCK, h, 2, d).transpose(1, 0, 2, 3, 4, 5)

    def block(args):
        q_blk, bi = args
        q_chunk = (bi * Q_BLOCK + jnp.arange(Q_BLOCK)) // CHUNK
        mask = key_chunk[None, :] <= q_chunk[:, None]
        sc = jnp.einsum('bqhcd,bkhcd->bhcqk', q_blk.astype(jnp.float32), kf) * scale
        sc = jnp.where(mask[None, None, None], sc, -jnp.inf)
        p = jax.nn.softmax(sc, axis=-1)
        a = p[:, :, 0] - lam * p[:, :, 1]
        return jnp.einsum('bhqk,bkhe->bqhe', a, vf)

    o = lax.map(block, (qb, jnp.arange(nb)))
    o = o.transpose(1, 0, 2, 3, 4).reshape(b, s, h, 2 * d)
    o = o * lax.rsqrt(jnp.mean(o * o, axis=-1, keepdims=True) + EPS)
    o = o * subln_g.astype(jnp.float32) * (1.0 - lam_init)
    return o.reshape(b, s, h * 2 * d).astype(q.dtype)


def conformer_conv(u, dw_w, dw_b, ln_g, ln_b):
    a, gate = jnp.split(u, 2, axis=-1)
    g = a * jax.nn.sigmoid(gate)
    c = g.shape[-1]
    y = lax.conv_general_dilated(
        g, dw_w[:, None, :].astype(g.dtype), window_strides=(1,),
        padding=[(CONV_WIDTH - 1, 0)], dimension_numbers=('NWC', 'WIO', 'NWC'),
        feature_group_count=c)
    y = y + dw_b
    y = layernorm(y, ln_g, ln_b)
    return jax.nn.silu(y)


def memory_cross_attention(q, mem_n, w_mem_kv):
    b, s, _ = q.shape
    kv = mem_n @ w_mem_kv
    mk, mv = jnp.split(kv, 2, axis=-1)
    qh = q.reshape(b, s, MEM_HEADS, MEM_HEAD_DIM).astype(jnp.float32)
    mk = mk.reshape(b, -1, MEM_HEADS, MEM_HEAD_DIM).astype(jnp.float32)
    mv = mv.reshape(b, -1, MEM_HEADS, MEM_HEAD_DIM).astype(jnp.float32)
    sc = jnp.einsum('bshd,bmhd->bhsm', qh, mk) * (MEM_HEAD_DIM ** -0.5)
    p = jax.nn.softmax(sc, axis=-1)
    o = jnp.einsum('bhsm,bmhd->bshd', p, mv)
    return o.reshape(b, s, MEM_WIDTH).astype(q.dtype)


def setup_inputs(seed: int = 0) -> dict:
    key = jax.random.key(seed)
    ks = jax.random.split(key, 32)
    L, D, F = DEPTH, D_MODEL, D_FF
    f32 = jnp.float32

    def nrm(k, shape, fan_in):
        return jax.random.normal(k, shape, f32) * (fan_in ** -0.5)

    def gain(k, shape):
        return 1.0 + 0.02 * jax.random.normal(k, shape, f32)

    def small(k, shape, s=0.02):
        return s * jax.random.normal(k, shape, f32)

    return {
        "x": jax.random.normal(ks[0], (BATCH, SEQ, D), f32),
        "mem": jax.random.normal(ks[1], (BATCH, N_MEM, D), f32),
        "ffn1_norm_g": gain(ks[2], (L, D)),
        "ffn1_w_gate": nrm(ks[3], (L, D, F), D),
        "ffn1_w_up": nrm(ks[4], (L, D, F), D),
        "ffn1_w_down": nrm(ks[5], (L, F, D), F),
        "mix_norm_g": gain(ks[6], (L, D)),
        "mem_norm_g": gain(ks[7], (L, D)),
        "w_in": nrm(ks[8], (L, D, IN_WIDTH), D),
        "lambda_q1": 0.1 * jax.random.normal(ks[9], (L, DA_HEAD_DIM), f32),
        "lambda_k1": 0.1 * jax.random.normal(ks[10], (L, DA_HEAD_DIM), f32),
        "lambda_q2": 0.1 * jax.random.normal(ks[11], (L, DA_HEAD_DIM), f32),
        "lambda_k2": 0.1 * jax.random.normal(ks[12], (L, DA_HEAD_DIM), f32),
        "subln_g": gain(ks[13], (L, DA_V_DIM)),
        "conv_dw_w": nrm(ks[14], (L, CONV_WIDTH, CONV_CH), CONV_WIDTH),
        "conv_dw_b": small(ks[15], (L, CONV_CH)),
        "conv_ln_g": gain(ks[16], (L, CONV_CH)),
        "conv_ln_b": small(ks[17], (L, CONV_CH)),
        "w_mem_kv": nrm(ks[18], (L, D, 2 * MEM_WIDTH), D),
        "w_out": nrm(ks[19], (L, MIX_WIDTH, D), MIX_WIDTH),
        "ffn2_norm_g": gain(ks[20], (L, D)),
        "ffn2_w_gate": nrm(ks[21], (L, D, F), D),
        "ffn2_w_up": nrm(ks[22], (L, D, F), D),
        "ffn2_w_down": nrm(ks[23], (L, F, D), F),
        "final_norm_g": gain(ks[24], (D,)),
    }


def reference(x, mem, ffn1_norm_g, ffn1_w_gate, ffn1_w_up, ffn1_w_down,
              mix_norm_g, mem_norm_g, w_in, lambda_q1, lambda_k1, lambda_q2, lambda_k2,
              subln_g, conv_dw_w, conv_dw_b, conv_ln_g, conv_ln_b, w_mem_kv, w_out,
              ffn2_norm_g, ffn2_w_gate, ffn2_w_up, ffn2_w_down, final_norm_g):
    b, s, _ = x.shape
    cos, sin = rope_tables(s, DA_HEAD_DIM)
    split_idx = np.cumsum([IN_Q, IN_K, IN_V, IN_CONV]).tolist()
    h = x
    for l in range(DEPTH):
        h = h + 0.5 * swiglu(rmsnorm(h, ffn1_norm_g[l]), ffn1_w_gate[l], ffn1_w_up[l], ffn1_w_down[l])

        n = rmsnorm(h, mix_norm_g[l])
        proj = n @ w_in[l]
        q, k, v, u_conv, q_mem = jnp.split(proj, split_idx, axis=-1)

        q = apply_rope(q.reshape(b, s, DA_HEADS, 2, DA_HEAD_DIM), cos, sin)
        k = apply_rope(k.reshape(b, s, DA_HEADS, 2, DA_HEAD_DIM), cos, sin)
        v = v.reshape(b, s, DA_HEADS, DA_V_DIM)
        lam_init = 0.8 - 0.6 * math.exp(-0.3 * l)
        lam = (jnp.exp(jnp.sum(lambda_q1[l] * lambda_k1[l]).astype(jnp.float32))
               - jnp.exp(jnp.sum(lambda_q2[l] * lambda_k2[l]).astype(jnp.float32)) + lam_init)
        o_da = differential_attention(q, k, v, lam, subln_g[l], lam_init)

        o_conv = conformer_conv(u_conv, conv_dw_w[l], conv_dw_b[l], conv_ln_g[l], conv_ln_b[l])

        o_mem = memory_cross_attention(q_mem, rmsnorm(mem, mem_norm_g[l]), w_mem_kv[l])

        mixed = jnp.concatenate([o_da, o_conv.astype(o_da.dtype), o_mem], axis=-1)
        h = h + mixed @ w_out[l]

        h = h + 0.5 * swiglu(rmsnorm(h, ffn2_norm_g[l]), ffn2_w_gate[l], ffn2_w_up[l], ffn2_w_down[l])
    return rmsnorm(h, final_norm_g)
```

```python
import functools
import math

import jax
import jax.numpy as jnp
from jax import lax
from jax.experimental import pallas as pl
from jax.experimental.pallas import tpu as pltpu

F32 = jnp.float32
BF16 = jnp.bfloat16

EPS = 1e-5
ROPE_THETA = 10000.0
CHUNK = 64
DA_HEADS = 4
DA_HEAD_DIM = 64
DA_V_DIM = 128
DA_WIDTH = 512
CONV_CH = 256
CONV_WIDTH = 31
MEM_HEADS = 4
MEM_HEAD_DIM = 64
MEM_WIDTH = 256
LAM_INIT = 0.8 - 0.6 * math.exp(-0.3 * 0)

V7X_VMEM_BYTES = 64 * 1024 * 1024
VMEM_LIMIT = V7X_VMEM_BYTES - 8 * 1024 * 1024
LANES = 128

TOKEN_TILE = 512
ATTN_TILE = 512
FFN_CHUNK = 256
CONV_HALO = 32
NEG_BIG = -1e30


def _rms(x, g):
    return x * lax.rsqrt(jnp.mean(x * x, axis=-1, keepdims=True) + EPS) * g


def _swiglu(xn, wg_ref, wu_ref, wd_ref, act_ref):
    d_ff = wg_ref.shape[1]
    for c in range(0, d_ff, FFN_CHUNK):
        gate = jnp.dot(xn, wg_ref[:, c:c + FFN_CHUNK], preferred_element_type=F32)
        up = jnp.dot(xn, wu_ref[:, c:c + FFN_CHUNK], preferred_element_type=F32)
        act_ref[:, c:c + FFN_CHUNK] = (gate * jax.nn.sigmoid(gate) * up).astype(BF16)
    return jnp.dot(act_ref[...], wd_ref[...], preferred_element_type=F32)


def _rope(x, cos, sin_signed, first_half):
    partner = jnp.where(first_half,
                        pltpu.roll(x, LANES - DA_HEAD_DIM // 2, axis=1),
                        pltpu.roll(x, DA_HEAD_DIM // 2, axis=1))
    return x * cos + partner * sin_signed


def _pre_mix_kernel(x_ref, g1_ref, wg_ref, wu_ref, wd_ref, gm_ref, win_ref,
                    cos_ref, sin_ref,
                    h_ref, q_ref, k_ref, vt_ref, glu_ref, qm_ref, act_ref):
    x = x_ref[...]
    xn = _rms(x, g1_ref[...]).astype(BF16)
    h = x + 0.5 * _swiglu(xn, wg_ref, wu_ref, wd_ref, act_ref)
    h_ref[...] = h

    n = _rms(h, gm_ref[...]).astype(BF16)
    cos = cos_ref[...]
    sin_signed = sin_ref[...]
    lane = lax.broadcasted_iota(jnp.int32, cos.shape, 1)
    first_half = (lane % DA_HEAD_DIM) < (DA_HEAD_DIM // 2)
    scale = DA_HEAD_DIM ** -0.5
    for j in range(DA_WIDTH // LANES):
        c0 = j * LANES
        qj = jnp.dot(n, win_ref[:, c0:c0 + LANES], preferred_element_type=F32)
        q_ref[:, c0:c0 + LANES] = (_rope(qj, cos, sin_signed, first_half) * scale).astype(BF16)
        kj = jnp.dot(n, win_ref[:, DA_WIDTH + c0:DA_WIDTH + c0 + LANES],
                     preferred_element_type=F32)
        k_ref[:, c0:c0 + LANES] = _rope(kj, cos, sin_signed, first_half).astype(BF16)
    v = jnp.dot(n, win_ref[:, 2 * DA_WIDTH:3 * DA_WIDTH], preferred_element_type=F32)
    vt_ref[0] = v.T.astype(BF16)
    o = 3 * DA_WIDTH
    a = jnp.dot(n, win_ref[:, o:o + CONV_CH], preferred_element_type=F32)
    gate = jnp.dot(n, win_ref[:, o + CONV_CH:o + 2 * CONV_CH], preferred_element_type=F32)
    glu_ref[...] = a * jax.nn.sigmoid(gate)
    o += 2 * CONV_CH
    qm_ref[...] = jnp.dot(n, win_ref[:, o:o + MEM_WIDTH],
                          preferred_element_type=F32).astype(BF16)


def _mem_kv_kernel(mem_ref, g_ref, w_ref, mk_ref, mv_ref):
    mn = _rms(mem_ref[0], g_ref[...]).astype(BF16)
    kv = jnp.dot(mn, w_ref[...], preferred_element_type=F32)
    mk_ref[0] = kv[:, :MEM_WIDTH].astype(BF16)
    mv_ref[0] = kv[:, MEM_WIDTH:].astype(BF16)


def _diff_attn_kernel(lq1_ref, lk1_ref, lq2_ref, lk2_ref, sg_ref,
                      q_ref, k_ref, vt_ref, o_ref, m_ref, l_ref, acc_ref):
    qi = pl.program_id(2)
    tq = q_ref.shape[1]
    tk = tq

    q = q_ref[0]
    lane = lax.broadcasted_iota(jnp.int32, q.shape, 1)
    zero = jnp.zeros_like(q)
    q_comp = (jnp.where(lane < DA_HEAD_DIM, q, zero), jnp.where(lane >= DA_HEAD_DIM, q, zero))

    m_ref[...] = jnp.full(m_ref.shape, NEG_BIG, F32)
    l_ref[...] = jnp.zeros(l_ref.shape, F32)
    acc_ref[...] = jnp.zeros(acc_ref.shape, F32)

    def block(j, mask):
        start = pl.multiple_of(j * tk, tk)
        kb = k_ref[0, pl.ds(start, tk), :]
        vt = vt_ref[0, j]
        for c in range(2):
            s = lax.dot_general(kb, q_comp[c], (((1,), (1,)), ((), ())),
                                preferred_element_type=F32)
            if mask is not None:
                s = jnp.where(mask, s, NEG_BIG)
            m_old = m_ref[c]
            m_new = jnp.maximum(m_old, jnp.max(s, axis=0, keepdims=True))
            p = jnp.exp(s - m_new)
            alpha = jnp.exp(m_old - m_new)
            m_ref[c] = m_new
            l_ref[c] = alpha * l_ref[c] + jnp.sum(p, axis=0, keepdims=True)
            acc_ref[c] = alpha * acc_ref[c] + jnp.dot(vt, p.astype(BF16),
                                                      preferred_element_type=F32)

    def body(j, carry):
        block(j, None)
        return carry

    lax.fori_loop(0, qi, body, 0)
    key_chunk = lax.broadcasted_iota(jnp.int32, (tk, tq), 0) // CHUNK
    qry_chunk = lax.broadcasted_iota(jnp.int32, (tk, tq), 1) // CHUNK
    block(qi, key_chunk <= qry_chunk)

    lam = (jnp.exp(jnp.sum(lq1_ref[...] * lk1_ref[...], axis=-1, keepdims=True))
           - jnp.exp(jnp.sum(lq2_ref[...] * lk2_ref[...], axis=-1, keepdims=True))
           + LAM_INIT)
    o = acc_ref[0] / l_ref[0] - lam * (acc_ref[1] / l_ref[1])
    o = o * lax.rsqrt(jnp.mean(o * o, axis=0, keepdims=True) + EPS)
    o = o * sg_ref[...] * (1.0 - LAM_INIT)
    o_ref[0] = o.T.astype(BF16)


def _post_mix_kernel(h_ref, oda_ref, glu_ref, halo_ref, qm_ref, mk_ref, mv_ref,
                     dww_ref, dwb_ref, lng_ref, lnb_ref, wout_ref,
                     g2_ref, wg_ref, wu_ref, wd_ref, gf_ref,
                     out_ref, cbuf_ref, act_ref):
    si = pl.program_id(1)
    tm = h_ref.shape[1]

    halo = halo_ref[0]
    cbuf_ref[0:CONV_HALO, :] = jnp.where(si > 0, halo, jnp.zeros_like(halo))
    cbuf_ref[CONV_HALO:CONV_HALO + tm, :] = glu_ref[0]
    base = CONV_HALO - (CONV_WIDTH - 1)
    y = jnp.zeros((tm, CONV_CH), F32)
    for j in range(CONV_WIDTH):
        y = y + dww_ref[j:j + 1, :] * cbuf_ref[base + j:base + j + tm, :]
    y = y + dwb_ref[...]
    mu = jnp.mean(y, axis=-1, keepdims=True)
    yc = y - mu
    var = jnp.mean(yc * yc, axis=-1, keepdims=True)
    y = yc * lax.rsqrt(var + EPS) * lng_ref[...] + lnb_ref[...]
    o_conv = (y * jax.nn.sigmoid(y)).astype(BF16)

    qm = qm_ref[0]
    mk = mk_ref[0]
    mv = mv_ref[0]
    heads = []
    for hd in range(MEM_HEADS):
        sl = slice(hd * MEM_HEAD_DIM, (hd + 1) * MEM_HEAD_DIM)
        s = lax.dot_general(qm[:, sl], mk[:, sl], (((1,), (1,)), ((), ())),
                            preferred_element_type=F32) * (MEM_HEAD_DIM ** -0.5)
        s = s - jnp.max(s, axis=-1, keepdims=True)
        p = jnp.exp(s)
        p = p / jnp.sum(p, axis=-1, keepdims=True)
        heads.append(jnp.dot(p.astype(BF16), mv[:, sl], preferred_element_type=F32))
    o_mem = jnp.concatenate(heads, axis=-1).astype(BF16)

    mix = (jnp.dot(oda_ref[0], wout_ref[0:DA_WIDTH, :], preferred_element_type=F32)
           + jnp.dot(o_conv, wout_ref[DA_WIDTH:DA_WIDTH + CONV_CH, :],
                     preferred_element_type=F32)
           + jnp.dot(o_mem, wout_ref[DA_WIDTH + CONV_CH:, :], preferred_element_type=F32))
    h = h_ref[0] + mix
    hn = _rms(h, g2_ref[...]).astype(BF16)
    h = h + 0.5 * _swiglu(hn, wg_ref, wu_ref, wd_ref, act_ref)
    out_ref[0] = _rms(h, gf_ref[...])


def _resident(shape):
    return pl.BlockSpec(shape, lambda *_: (0,) * len(shape), pipeline_mode=pl.Buffered(1))


def _rope_tables(seq):
    half = DA_HEAD_DIM // 2
    inv_freq = 1.0 / (ROPE_THETA ** (jnp.arange(0, DA_HEAD_DIM, 2, dtype=F32) / DA_HEAD_DIM))
    ang = jnp.arange(seq, dtype=F32)[:, None] * inv_freq[None, :]
    cos, sin = jnp.cos(ang), jnp.sin(ang)
    reps = LANES // DA_HEAD_DIM
    cos_t = jnp.tile(jnp.concatenate([cos, cos], axis=1), (1, reps))
    sin_t = jnp.tile(jnp.concatenate([-sin, sin], axis=1), (1, reps))
    assert cos_t.shape == (seq, LANES) and half * 2 == DA_HEAD_DIM
    return cos_t, sin_t


def kernel(x, mem, ffn1_norm_g, ffn1_w_gate, ffn1_w_up, ffn1_w_down, mix_norm_g, mem_norm_g, w_in, lambda_q1, lambda_k1, lambda_q2, lambda_k2, subln_g, conv_dw_w, conv_dw_b, conv_ln_g, conv_ln_b, w_mem_kv, w_out, ffn2_norm_g, ffn2_w_gate, ffn2_w_up, ffn2_w_down, final_norm_g):
    b, s, d = x.shape
    n_mem = mem.shape[1]
    d_ff = ffn1_w_gate.shape[-1]
    in_width = w_in.shape[-1]
    assert ffn1_norm_g.shape[0] == 1, "single layer"
    tm = TOKEN_TILE
    ta = ATTN_TILE
    assert s % tm == 0 and s % ta == 0 and tm == ta and d_ff % FFN_CHUNK == 0
    t = b * s
    n_tiles = s // tm
    params = functools.partial(pltpu.CompilerParams, vmem_limit_bytes=VMEM_LIMIT)

    cos_t, sin_t = _rope_tables(s)
    row = lambda a: a.reshape(1, -1)
    bf = lambda a: a[0].astype(BF16)

    tok = lambda w: pl.BlockSpec((tm, w), lambda i: (i, 0))
    h1, q, k, vt, glu, qm = pl.pallas_call(
        _pre_mix_kernel,
        grid=(t // tm,),
        in_specs=[tok(d), _resident((1, d)), _resident((d, d_ff)), _resident((d, d_ff)),
                  _resident((d_ff, d)), _resident((1, d)), _resident((d, in_width)),
                  pl.BlockSpec((tm, LANES), lambda i: (i % n_tiles, 0)),
                  pl.BlockSpec((tm, LANES), lambda i: (i % n_tiles, 0))],
        out_specs=[tok(d), tok(DA_WIDTH), tok(DA_WIDTH),
                   pl.BlockSpec((1, DA_WIDTH, tm), lambda i: (i, 0, 0)),
                   tok(CONV_CH), tok(MEM_WIDTH)],
        out_shape=[jax.ShapeDtypeStruct((t, d), F32),
                   jax.ShapeDtypeStruct((t, DA_WIDTH), BF16),
                   jax.ShapeDtypeStruct((t, DA_WIDTH), BF16),
                   jax.ShapeDtypeStruct((t // tm, DA_WIDTH, tm), BF16),
                   jax.ShapeDtypeStruct((t, CONV_CH), F32),
                   jax.ShapeDtypeStruct((t, MEM_WIDTH), BF16)],
        scratch_shapes=[pltpu.VMEM((tm, d_ff), BF16)],
        compiler_params=params(dimension_semantics=("parallel",)),
        name="pre_mix",
    )(x.reshape(t, d), row(ffn1_norm_g), bf(ffn1_w_gate), bf(ffn1_w_up), bf(ffn1_w_down),
      row(mix_norm_g), bf(w_in), cos_t, sin_t)

    mk, mv = pl.pallas_call(
        _mem_kv_kernel,
        grid=(b,),
        in_specs=[pl.BlockSpec((1, n_mem, d), lambda i: (i, 0, 0)), _resident((1, d)),
                  _resident((d, 2 * MEM_WIDTH))],
        out_specs=[pl.BlockSpec((1, n_mem, MEM_WIDTH), lambda i: (i, 0, 0))] * 2,
        out_shape=[jax.ShapeDtypeStruct((b, n_mem, MEM_WIDTH), BF16)] * 2,
        compiler_params=params(dimension_semantics=("parallel",)),
        name="mem_kv",
    )(mem, row(mem_norm_g), bf(w_mem_kv))

    lam_spec = _resident((1, DA_HEAD_DIM))
    o_da = pl.pallas_call(
        _diff_attn_kernel,
        grid=(b, DA_HEADS, s // ta),
        in_specs=[lam_spec, lam_spec, lam_spec, lam_spec, _resident((DA_V_DIM, 1)),
                  pl.BlockSpec((1, ta, DA_V_DIM), lambda bi, hi, qi: (bi, qi, hi)),
                  pl.BlockSpec((1, s, DA_V_DIM), lambda bi, hi, qi: (bi, 0, hi)),
                  pl.BlockSpec((1, s // ta, DA_V_DIM, ta), lambda bi, hi, qi: (bi, 0, hi, 0))],
        out_specs=pl.BlockSpec((1, ta, DA_V_DIM), lambda bi, hi, qi: (bi, qi, hi)),
        out_shape=jax.ShapeDtypeStruct((b, s, DA_WIDTH), BF16),
        scratch_shapes=[pltpu.VMEM((2, 1, ta), F32), pltpu.VMEM((2, 1, ta), F32),
                        pltpu.VMEM((2, DA_V_DIM, ta), F32)],
        compiler_params=params(dimension_semantics=("parallel", "parallel", "arbitrary")),
        name="diff_attn",
    )(lambda_q1, lambda_k1, lambda_q2, lambda_k2, subln_g.reshape(DA_V_DIM, 1),
      q.reshape(b, s, DA_WIDTH), k.reshape(b, s, DA_WIDTH),
      vt.reshape(b, s // ta, DA_WIDTH, ta))

    halo_blocks = tm // CONV_HALO
    seq_tile = lambda w: pl.BlockSpec((1, tm, w), lambda bi, si: (bi, si, 0))
    out = pl.pallas_call(
        _post_mix_kernel,
        grid=(b, n_tiles),
        in_specs=[seq_tile(d), seq_tile(DA_WIDTH), seq_tile(CONV_CH),
                  pl.BlockSpec((1, CONV_HALO, CONV_CH),
                               lambda bi, si: (bi, jnp.maximum(si * halo_blocks - 1, 0), 0)),
                  seq_tile(MEM_WIDTH),
                  pl.BlockSpec((1, n_mem, MEM_WIDTH), lambda bi, si: (bi, 0, 0)),
                  pl.BlockSpec((1, n_mem, MEM_WIDTH), lambda bi, si: (bi, 0, 0)),
                  _resident((CONV_WIDTH, CONV_CH)), _resident((1, CONV_CH)),
                  _resident((1, CONV_CH)), _resident((1, CONV_CH)), _resident((d, d)),
                  _resident((1, d)), _resident((d, d_ff)), _resident((d, d_ff)),
                  _resident((d_ff, d)), _resident((1, d))],
        out_specs=seq_tile(d),
        out_shape=jax.ShapeDtypeStruct((b, s, d), F32),
        scratch_shapes=[pltpu.VMEM((CONV_HALO + tm, CONV_CH), F32),
                        pltpu.VMEM((tm, d_ff), BF16)],
        compiler_params=params(dimension_semantics=("parallel", "arbitrary")),
        name="post_mix",
    )(h1.reshape(b, s, d), o_da, glu.reshape(b, s, CONV_CH), glu.reshape(b, s, CONV_CH),
      qm.reshape(b, s, MEM_WIDTH), mk, mv, conv_dw_w[0], conv_dw_b, conv_ln_g, conv_ln_b,
      bf(w_out), row(ffn2_norm_g), bf(ffn2_w_gate), bf(ffn2_w_up), bf(ffn2_w_down),
      row(final_norm_g))
    return out
```

```python
import functools
import math

import jax
import jax.numpy as jnp
from jax import lax
from jax.experimental import pallas as pl
from jax.experimental.pallas import tpu as pltpu

F32 = jnp.float32
BF16 = jnp.bfloat16

EPS = 1e-5
ROPE_THETA = 10000.0
CHUNK = 64
DA_HEADS = 4
DA_HEAD_DIM = 64
DA_V_DIM = 128
DA_WIDTH = 512
CONV_CH = 256
CONV_WIDTH = 31
MEM_HEADS = 4
MEM_HEAD_DIM = 64
MEM_WIDTH = 256
LAM_INIT = 0.8 - 0.6 * math.exp(-0.3 * 0)

V7X_VMEM_BYTES = 64 * 1024 * 1024
VMEM_LIMIT = V7X_VMEM_BYTES - 8 * 1024 * 1024
LANES = 128

TOKEN_TILE = 512
ATTN_TILE = 512
FFN_CHUNK = 256
CONV_HALO = 32
NEG_BIG = -1e30


def _rms(x, g):
    return x * lax.rsqrt(jnp.mean(x * x, axis=-1, keepdims=True) + EPS) * g


def _swiglu(xn, wg_ref, wu_ref, wd_ref, act_ref):
    d_ff = wg_ref.shape[1]
    for c in range(0, d_ff, FFN_CHUNK):
        gate = jnp.dot(xn, wg_ref[:, c:c + FFN_CHUNK], preferred_element_type=F32)
        up = jnp.dot(xn, wu_ref[:, c:c + FFN_CHUNK], preferred_element_type=F32)
        act_ref[:, c:c + FFN_CHUNK] = (gate * jax.nn.sigmoid(gate) * up).astype(BF16)
    return jnp.dot(act_ref[...], wd_ref[...], preferred_element_type=F32)


def _rope(x, cos, sin_signed, first_half):
    partner = jnp.where(first_half,
                        pltpu.roll(x, LANES - DA_HEAD_DIM // 2, axis=1),
                        pltpu.roll(x, DA_HEAD_DIM // 2, axis=1))
    return x * cos + partner * sin_signed


def _pre_mix_kernel(x_ref, g1_ref, wg_ref, wu_ref, wd_ref, gm_ref, win_ref,
                    cos_ref, sin_ref,
                    h_ref, q_ref, k_ref, vt_ref, glu_ref, qm_ref, act_ref):
    x = x_ref[...]
    xn = _rms(x, g1_ref[...]).astype(BF16)
    h = x + 0.5 * _swiglu(xn, wg_ref, wu_ref, wd_ref, act_ref)
    h_ref[...] = h

    n = _rms(h, gm_ref[...]).astype(BF16)
    cos = cos_ref[...]
    sin_signed = sin_ref[...]
    lane = lax.broadcasted_iota(jnp.int32, cos.shape, 1)
    first_half = (lane % DA_HEAD_DIM) < (DA_HEAD_DIM // 2)
    scale = DA_HEAD_DIM ** -0.5
    for j in range(DA_WIDTH // LANES):
        c0 = j * LANES
        qj = jnp.dot(n, win_ref[:, c0:c0 + LANES], preferred_element_type=F32)
        q_ref[:, c0:c0 + LANES] = (_rope(qj, cos, sin_signed, first_half) * scale).astype(BF16)
        kj = jnp.dot(n, win_ref[:, DA_WIDTH + c0:DA_WIDTH + c0 + LANES],
                     preferred_element_type=F32)
        k_ref[:, c0:c0 + LANES] = _rope(kj, cos, sin_signed, first_half).astype(BF16)
    v = jnp.dot(n, win_ref[:, 2 * DA_WIDTH:3 * DA_WIDTH], preferred_element_type=F32)
    vt_ref[0] = v.T.astype(BF16)
    o = 3 * DA_WIDTH
    a = jnp.dot(n, win_ref[:, o:o + CONV_CH], preferred_element_type=F32)
    gate = jnp.dot(n, win_ref[:, o + CONV_CH:o + 2 * CONV_CH], preferred_element_type=F32)
    glu_ref[...] = a * jax.nn.sigmoid(gate)
    o += 2 * CONV_CH
    qm_ref[...] = jnp.dot(n, win_ref[:, o:o + MEM_WIDTH],
                          preferred_element_type=F32).astype(BF16)


def _mem_kv_kernel(mem_ref, g_ref, w_ref, mk_ref, mv_ref):
    mn = _rms(mem_ref[0], g_ref[...]).astype(BF16)
    kv = jnp.dot(mn, w_ref[...], preferred_element_type=F32)
    mk_ref[0] = kv[:, :MEM_WIDTH].astype(BF16)
    mv_ref[0] = kv[:, MEM_WIDTH:].astype(BF16)


def _diff_attn_kernel(lq1_ref, lk1_ref, lq2_ref, lk2_ref, sg_ref,
                      q_ref, k_ref, vt_ref, o_ref, m_ref, l_ref, acc_ref):
    qi = pl.program_id(1)
    tq = q_ref.shape[1]
    tk = tq

    lane = lax.broadcasted_iota(jnp.int32, (tq, LANES), 1)
    q_comp = []
    for hd in range(DA_HEADS):
        q = q_ref[0, :, hd * LANES:(hd + 1) * LANES]
        zero = jnp.zeros_like(q)
        q_comp.append(jnp.where(lane < DA_HEAD_DIM, q, zero))
        q_comp.append(jnp.where(lane >= DA_HEAD_DIM, q, zero))

    m_ref[...] = jnp.full(m_ref.shape, NEG_BIG, F32)
    l_ref[...] = jnp.zeros(l_ref.shape, F32)
    acc_ref[...] = jnp.zeros(acc_ref.shape, F32)

    def block(j, mask):
        start = pl.multiple_of(j * tk, tk)

        def scores(i):
            hd = i // 2
            kb = k_ref[0, pl.ds(start, tk), hd * LANES:(hd + 1) * LANES]
            return lax.dot_general(kb, q_comp[i], (((1,), (1,)), ((), ())),
                                   preferred_element_type=F32)

        n_chain = 2 * DA_HEADS
        s_next = scores(0)
        for i in range(n_chain):
            s = s_next
            if i + 1 < n_chain:
                s_next = scores(i + 1)
            hd = i // 2
            vt = vt_ref[0, j, hd * DA_V_DIM:(hd + 1) * DA_V_DIM, :]
            if mask is not None:
                s = jnp.where(mask, s, NEG_BIG)
            m_old = m_ref[i]
            m_new = jnp.maximum(m_old, jnp.max(s, axis=0, keepdims=True))
            p = jnp.exp(s - m_new)
            alpha = jnp.exp(m_old - m_new)
            m_ref[i] = m_new
            l_ref[i] = alpha * l_ref[i] + jnp.sum(p, axis=0, keepdims=True)
            acc_ref[i] = alpha * acc_ref[i] + jnp.dot(vt, p.astype(BF16),
                                                      preferred_element_type=F32)

    def body(j, carry):
        block(j, None)
        return carry

    lax.fori_loop(0, qi, body, 0)
    key_chunk = lax.broadcasted_iota(jnp.int32, (tk, tq), 0) // CHUNK
    qry_chunk = lax.broadcasted_iota(jnp.int32, (tk, tq), 1) // CHUNK
    block(qi, key_chunk <= qry_chunk)

    lam = (jnp.exp(jnp.sum(lq1_ref[...] * lk1_ref[...], axis=-1, keepdims=True))
           - jnp.exp(jnp.sum(lq2_ref[...] * lk2_ref[...], axis=-1, keepdims=True))
           + LAM_INIT)
    for hd in range(DA_HEADS):
        o = (acc_ref[2 * hd] / l_ref[2 * hd]
             - lam * (acc_ref[2 * hd + 1] / l_ref[2 * hd + 1]))
        o = o * lax.rsqrt(jnp.mean(o * o, axis=0, keepdims=True) + EPS)
        o = o * sg_ref[...] * (1.0 - LAM_INIT)
        o_ref[0, :, hd * DA_V_DIM:(hd + 1) * DA_V_DIM] = o.T.astype(BF16)


def _post_mix_kernel(h_ref, oda_ref, glu_ref, halo_ref, qm_ref, mk_ref, mv_ref,
                     dww_ref, dwb_ref, lng_ref, lnb_ref, wout_ref,
                     g2_ref, wg_ref, wu_ref, wd_ref, gf_ref,
                     out_ref, cbuf_ref, act_ref):
    si = pl.program_id(1)
    tm = h_ref.shape[1]

    halo = halo_ref[0]
    cbuf_ref[0:CONV_HALO, :] = jnp.where(si > 0, halo, jnp.zeros_like(halo))
    cbuf_ref[CONV_HALO:CONV_HALO + tm, :] = glu_ref[0]
    base = CONV_HALO - (CONV_WIDTH - 1)
    y = jnp.zeros((tm, CONV_CH), F32)
    for j in range(CONV_WIDTH):
        y = y + dww_ref[j:j + 1, :] * cbuf_ref[base + j:base + j + tm, :]
    y = y + dwb_ref[...]
    mu = jnp.mean(y, axis=-1, keepdims=True)
    yc = y - mu
    var = jnp.mean(yc * yc, axis=-1, keepdims=True)
    y = yc * lax.rsqrt(var + EPS) * lng_ref[...] + lnb_ref[...]
    o_conv = (y * jax.nn.sigmoid(y)).astype(BF16)

    qm = qm_ref[0]
    mk = mk_ref[0]
    mv = mv_ref[0]
    heads = []
    for hd in range(MEM_HEADS):
        sl = slice(hd * MEM_HEAD_DIM, (hd + 1) * MEM_HEAD_DIM)
        s = lax.dot_general(qm[:, sl], mk[:, sl], (((1,), (1,)), ((), ())),
                            preferred_element_type=F32) * (MEM_HEAD_DIM ** -0.5)
        s = s - jnp.max(s, axis=-1, keepdims=True)
        p = jnp.exp(s)
        p = p / jnp.sum(p, axis=-1, keepdims=True)
        heads.append(jnp.dot(p.astype(BF16), mv[:, sl], preferred_element_type=F32))
    o_mem = jnp.concatenate(heads, axis=-1).astype(BF16)

    mix = (jnp.dot(oda_ref[0], wout_ref[0:DA_WIDTH, :], preferred_element_type=F32)
           + jnp.dot(o_conv, wout_ref[DA_WIDTH:DA_WIDTH + CONV_CH, :],
                     preferred_element_type=F32)
           + jnp.dot(o_mem, wout_ref[DA_WIDTH + CONV_CH:, :], preferred_element_type=F32))
    h = h_ref[0] + mix
    hn = _rms(h, g2_ref[...]).astype(BF16)
    h = h + 0.5 * _swiglu(hn, wg_ref, wu_ref, wd_ref, act_ref)
    out_ref[0] = _rms(h, gf_ref[...])


def _resident(shape):
    return pl.BlockSpec(shape, lambda *_: (0,) * len(shape), pipeline_mode=pl.Buffered(1))


def _rope_tables(seq):
    half = DA_HEAD_DIM // 2
    inv_freq = 1.0 / (ROPE_THETA ** (jnp.arange(0, DA_HEAD_DIM, 2, dtype=F32) / DA_HEAD_DIM))
    ang = jnp.arange(seq, dtype=F32)[:, None] * inv_freq[None, :]
    cos, sin = jnp.cos(ang), jnp.sin(ang)
    reps = LANES // DA_HEAD_DIM
    cos_t = jnp.tile(jnp.concatenate([cos, cos], axis=1), (1, reps))
    sin_t = jnp.tile(jnp.concatenate([-sin, sin], axis=1), (1, reps))
    assert cos_t.shape == (seq, LANES) and half * 2 == DA_HEAD_DIM
    return cos_t, sin_t


def kernel(x, mem, ffn1_norm_g, ffn1_w_gate, ffn1_w_up, ffn1_w_down, mix_norm_g, mem_norm_g, w_in, lambda_q1, lambda_k1, lambda_q2, lambda_k2, subln_g, conv_dw_w, conv_dw_b, conv_ln_g, conv_ln_b, w_mem_kv, w_out, ffn2_norm_g, ffn2_w_gate, ffn2_w_up, ffn2_w_down, final_norm_g):
    b, s, d = x.shape
    n_mem = mem.shape[1]
    d_ff = ffn1_w_gate.shape[-1]
    in_width = w_in.shape[-1]
    assert ffn1_norm_g.shape[0] == 1, "single layer"
    tm = TOKEN_TILE
    ta = ATTN_TILE
    assert s % tm == 0 and s % ta == 0 and tm == ta and d_ff % FFN_CHUNK == 0
    t = b * s
    n_tiles = s // tm
    params = functools.partial(pltpu.CompilerParams, vmem_limit_bytes=VMEM_LIMIT)

    cos_t, sin_t = _rope_tables(s)
    row = lambda a: a.reshape(1, -1)
    bf = lambda a: a[0].astype(BF16)

    tok = lambda w: pl.BlockSpec((tm, w), lambda i: (i, 0))
    h1, q, k, vt, glu, qm = pl.pallas_call(
        _pre_mix_kernel,
        grid=(t // tm,),
        in_specs=[tok(d), _resident((1, d)), _resident((d, d_ff)), _resident((d, d_ff)),
                  _resident((d_ff, d)), _resident((1, d)), _resident((d, in_width)),
                  pl.BlockSpec((tm, LANES), lambda i: (i % n_tiles, 0)),
                  pl.BlockSpec((tm, LANES), lambda i: (i % n_tiles, 0))],
        out_specs=[tok(d), tok(DA_WIDTH), tok(DA_WIDTH),
                   pl.BlockSpec((1, DA_WIDTH, tm), lambda i: (i, 0, 0)),
                   tok(CONV_CH), tok(MEM_WIDTH)],
        out_shape=[jax.ShapeDtypeStruct((t, d), F32),
                   jax.ShapeDtypeStruct((t, DA_WIDTH), BF16),
                   jax.ShapeDtypeStruct((t, DA_WIDTH), BF16),
                   jax.ShapeDtypeStruct((t // tm, DA_WIDTH, tm), BF16),
                   jax.ShapeDtypeStruct((t, CONV_CH), F32),
                   jax.ShapeDtypeStruct((t, MEM_WIDTH), BF16)],
        scratch_shapes=[pltpu.VMEM((tm, d_ff), BF16)],
        compiler_params=params(dimension_semantics=("parallel",)),
        name="pre_mix",
    )(x.reshape(t, d), row(ffn1_norm_g), bf(ffn1_w_gate), bf(ffn1_w_up), bf(ffn1_w_down),
      row(mix_norm_g), bf(w_in), cos_t, sin_t)

    mk, mv = pl.pallas_call(
        _mem_kv_kernel,
        grid=(b,),
        in_specs=[pl.BlockSpec((1, n_mem, d), lambda i: (i, 0, 0)), _resident((1, d)),
                  _resident((d, 2 * MEM_WIDTH))],
        out_specs=[pl.BlockSpec((1, n_mem, MEM_WIDTH), lambda i: (i, 0, 0))] * 2,
        out_shape=[jax.ShapeDtypeStruct((b, n_mem, MEM_WIDTH), BF16)] * 2,
        compiler_params=params(dimension_semantics=("parallel",)),
        name="mem_kv",
    )(mem, row(mem_norm_g), bf(w_mem_kv))

    lam_spec = _resident((1, DA_HEAD_DIM))
    o_da = pl.pallas_call(
        _diff_attn_kernel,
        grid=(b, s // ta),
        in_specs=[lam_spec, lam_spec, lam_spec, lam_spec, _resident((DA_V_DIM, 1)),
                  pl.BlockSpec((1, ta, DA_WIDTH), lambda bi, qi: (bi, qi, 0)),
                  pl.BlockSpec((1, s, DA_WIDTH), lambda bi, qi: (bi, 0, 0),
                               pipeline_mode=pl.Buffered(1)),
                  pl.BlockSpec((1, s // ta, DA_WIDTH, ta), lambda bi, qi: (bi, 0, 0, 0),
                               pipeline_mode=pl.Buffered(1))],
        out_specs=pl.BlockSpec((1, ta, DA_WIDTH), lambda bi, qi: (bi, qi, 0)),
        out_shape=jax.ShapeDtypeStruct((b, s, DA_WIDTH), BF16),
        scratch_shapes=[pltpu.VMEM((2 * DA_HEADS, 1, ta), F32),
                        pltpu.VMEM((2 * DA_HEADS, 1, ta), F32),
                        pltpu.VMEM((2 * DA_HEADS, DA_V_DIM, ta), F32)],
        compiler_params=params(dimension_semantics=("parallel", "arbitrary")),
        name="diff_attn",
    )(lambda_q1, lambda_k1, lambda_q2, lambda_k2, subln_g.reshape(DA_V_DIM, 1),
      q.reshape(b, s, DA_WIDTH), k.reshape(b, s, DA_WIDTH),
      vt.reshape(b, s // ta, DA_WIDTH, ta))

    halo_blocks = tm // CONV_HALO
    seq_tile = lambda w: pl.BlockSpec((1, tm, w), lambda bi, si: (bi, si, 0))
    out = pl.pallas_call(
        _post_mix_kernel,
        grid=(b, n_tiles),
        in_specs=[seq_tile(d), seq_tile(DA_WIDTH), seq_tile(CONV_CH),
                  pl.BlockSpec((1, CONV_HALO, CONV_CH),
                               lambda bi, si: (bi, jnp.maximum(si * halo_blocks - 1, 0), 0)),
                  seq_tile(MEM_WIDTH),
                  pl.BlockSpec((1, n_mem, MEM_WIDTH), lambda bi, si: (bi, 0, 0)),
                  pl.BlockSpec((1, n_mem, MEM_WIDTH), lambda bi, si: (bi, 0, 0)),
                  _resident((CONV_WIDTH, CONV_CH)), _resident((1, CONV_CH)),
                  _resident((1, CONV_CH)), _resident((1, CONV_CH)), _resident((d, d)),
                  _resident((1, d)), _resident((d, d_ff)), _resident((d, d_ff)),
                  _resident((d_ff, d)), _resident((1, d))],
        out_specs=seq_tile(d),
        out_shape=jax.ShapeDtypeStruct((b, s, d), F32),
        scratch_shapes=[pltpu.VMEM((CONV_HALO + tm, CONV_CH), F32),
                        pltpu.VMEM((tm, d_ff), BF16)],
        compiler_params=params(dimension_semantics=("parallel", "arbitrary")),
        name="post_mix",
    )(h1.reshape(b, s, d), o_da, glu.reshape(b, s, CONV_CH), glu.reshape(b, s, CONV_CH),
      qm.reshape(b, s, MEM_WIDTH), mk, mv, conv_dw_w[0], conv_dw_b, conv_ln_g, conv_ln_b,
      bf(w_out), row(ffn2_norm_g), bf(ffn2_w_gate), bf(ffn2_w_up), bf(ffn2_w_down),
      row(final_norm_g))
    return out
```

```python
import functools
import math

import jax
import jax.numpy as jnp
from jax import lax
from jax.experimental import pallas as pl
from jax.experimental.pallas import tpu as pltpu

F32 = jnp.float32
BF16 = jnp.bfloat16

EPS = 1e-5
ROPE_THETA = 10000.0
CHUNK = 64
DA_HEADS = 4
DA_HEAD_DIM = 64
DA_V_DIM = 128
V_AUG = DA_V_DIM + 16
DA_WIDTH = 512
CONV_CH = 256
CONV_WIDTH = 31
MEM_HEADS = 4
MEM_HEAD_DIM = 64
MEM_WIDTH = 256
LAM_INIT = 0.8 - 0.6 * math.exp(-0.3 * 0)

V7X_VMEM_BYTES = 64 * 1024 * 1024
VMEM_LIMIT = V7X_VMEM_BYTES - 8 * 1024 * 1024
LANES = 128

TOKEN_TILE = 512
ATTN_TILE = 512
FFN_CHUNK = 256
CONV_HALO = 32
SOFTMAX_SLAB = 16
SCORE_AHEAD = 2
SCORE_SLOTS = 4
NEG_BIG = -1e30
LOG2_E = math.log2(math.e)


def _rms(x, g):
    return x * lax.rsqrt(jnp.mean(x * x, axis=-1, keepdims=True) + EPS) * g


def _swiglu(xn, wg_ref, wu_ref, wd_ref, act_ref):
    d_ff = wg_ref.shape[1]
    for c in range(0, d_ff, FFN_CHUNK):
        gate = jnp.dot(xn, wg_ref[:, c:c + FFN_CHUNK], preferred_element_type=F32)
        up = jnp.dot(xn, wu_ref[:, c:c + FFN_CHUNK], preferred_element_type=F32)
        act_ref[:, c:c + FFN_CHUNK] = (gate * jax.nn.sigmoid(gate) * up).astype(BF16)
    return jnp.dot(act_ref[...], wd_ref[...], preferred_element_type=F32)


def _rope(x, cos, sin_signed, first_half):
    partner = jnp.where(first_half,
                        pltpu.roll(x, LANES - DA_HEAD_DIM // 2, axis=1),
                        pltpu.roll(x, DA_HEAD_DIM // 2, axis=1))
    return x * cos + partner * sin_signed


def _pre_mix_kernel(x_ref, g1_ref, wg_ref, wu_ref, wd_ref, gm_ref, win_ref,
                    cos_ref, sin_ref,
                    h_ref, q_ref, k_ref, vt_ref, glu_ref, qm_ref, act_ref):
    x = x_ref[...]
    xn = _rms(x, g1_ref[...]).astype(BF16)
    h = x + 0.5 * _swiglu(xn, wg_ref, wu_ref, wd_ref, act_ref)
    h_ref[...] = h

    n = _rms(h, gm_ref[...]).astype(BF16)
    cos = cos_ref[...]
    sin_signed = sin_ref[...]
    lane = lax.broadcasted_iota(jnp.int32, cos.shape, 1)
    first_half = (lane % DA_HEAD_DIM) < (DA_HEAD_DIM // 2)
    scale = DA_HEAD_DIM ** -0.5 * LOG2_E
    for j in range(DA_WIDTH // LANES):
        c0 = j * LANES
        qj = jnp.dot(n, win_ref[:, c0:c0 + LANES], preferred_element_type=F32)
        q_ref[:, c0:c0 + LANES] = (_rope(qj, cos, sin_signed, first_half) * scale).astype(BF16)
        kj = jnp.dot(n, win_ref[:, DA_WIDTH + c0:DA_WIDTH + c0 + LANES],
                     preferred_element_type=F32)
        k_ref[:, c0:c0 + LANES] = _rope(kj, cos, sin_signed, first_half).astype(BF16)
    v = jnp.dot(n, win_ref[:, 2 * DA_WIDTH:3 * DA_WIDTH], preferred_element_type=F32)
    vt = v.T.astype(BF16)
    ones = jnp.ones((V_AUG - DA_V_DIM, vt.shape[1]), BF16)
    for hd in range(DA_HEADS):
        vt_ref[0, hd * V_AUG:hd * V_AUG + DA_V_DIM, :] = vt[hd * DA_V_DIM:(hd + 1) * DA_V_DIM]
        vt_ref[0, hd * V_AUG + DA_V_DIM:(hd + 1) * V_AUG, :] = ones
    o = 3 * DA_WIDTH
    a = jnp.dot(n, win_ref[:, o:o + CONV_CH], preferred_element_type=F32)
    gate = jnp.dot(n, win_ref[:, o + CONV_CH:o + 2 * CONV_CH], preferred_element_type=F32)
    glu_ref[...] = a * jax.nn.sigmoid(gate)
    o += 2 * CONV_CH
    qm_ref[...] = jnp.dot(n, win_ref[:, o:o + MEM_WIDTH],
                          preferred_element_type=F32).astype(BF16)


def _mem_kv_kernel(mem_ref, g_ref, w_ref, mk_ref, mv_ref):
    mn = _rms(mem_ref[0], g_ref[...]).astype(BF16)
    kv = jnp.dot(mn, w_ref[...], preferred_element_type=F32)
    mk_ref[0] = kv[:, :MEM_WIDTH].astype(BF16)
    mv_ref[0] = kv[:, MEM_WIDTH:].astype(BF16)


def _diff_attn_kernel(lq1_ref, lk1_ref, lq2_ref, lk2_ref, sg_ref,
                      q_ref, k_ref, vt_ref, o_ref, m_ref, acc_ref, s_ref, p_ref):
    qi = pl.program_id(1)
    tq = q_ref.shape[1]
    tk = tq

    lane = lax.broadcasted_iota(jnp.int32, (tq, LANES), 1)
    q_comp = []
    for hd in range(DA_HEADS):
        q = q_ref[0, :, hd * LANES:(hd + 1) * LANES]
        zero = jnp.zeros_like(q)
        q_comp.append(jnp.where(lane < DA_HEAD_DIM, q, zero))
        q_comp.append(jnp.where(lane >= DA_HEAD_DIM, q, zero))

    m_ref[...] = jnp.full(m_ref.shape, NEG_BIG, F32)
    acc_ref[...] = jnp.zeros(acc_ref.shape, F32)

    def block(j, mask):
        start = pl.multiple_of(j * tk, tk)

        def scores(i):
            hd = i // 2
            kb = k_ref[0, pl.ds(start, tk), hd * LANES:(hd + 1) * LANES]
            s = lax.dot_general(kb, q_comp[i], (((1,), (1,)), ((), ())),
                                preferred_element_type=F32)
            if mask is not None:
                s = jnp.where(mask, s, NEG_BIG)
            s_ref[i % SCORE_SLOTS] = s

        n_chain = 2 * DA_HEADS
        n_slab = tk // SOFTMAX_SLAB
        rows = lambda r: slice(r * SOFTMAX_SLAB, (r + 1) * SOFTMAX_SLAB)
        for i in range(SCORE_AHEAD):
            scores(i)
        for i in range(n_chain):
            if i + SCORE_AHEAD < n_chain:
                scores(i + SCORE_AHEAD)
            slot = i % SCORE_SLOTS
            hd = i // 2
            vt = vt_ref[0, j, hd * V_AUG:(hd + 1) * V_AUG, :]
            part = s_ref[slot, rows(0), :]
            for r in range(1, n_slab):
                part = jnp.maximum(part, s_ref[slot, rows(r), :])
            m_old = m_ref[i]
            m_new = jnp.maximum(m_old, jnp.max(part, axis=0, keepdims=True))
            alpha = jnp.exp2(m_old - m_new)
            m_ref[i] = m_new
            m_slab = jnp.broadcast_to(m_new, (SOFTMAX_SLAB, tq))
            for r in range(n_slab):
                p_ref[slot, rows(r), :] = jnp.exp2(s_ref[slot, rows(r), :] - m_slab).astype(BF16)
            acc_ref[i] = alpha * acc_ref[i] + jnp.dot(vt, p_ref[slot],
                                                      preferred_element_type=F32)

    def body(j, carry):
        block(j, None)
        return carry

    lax.fori_loop(0, qi, body, 0)
    key_chunk = lax.broadcasted_iota(jnp.int32, (tk, tq), 0) // CHUNK
    qry_chunk = lax.broadcasted_iota(jnp.int32, (tk, tq), 1) // CHUNK
    block(qi, key_chunk <= qry_chunk)

    lam = (jnp.exp(jnp.sum(lq1_ref[...] * lk1_ref[...], axis=-1, keepdims=True))
           - jnp.exp(jnp.sum(lq2_ref[...] * lk2_ref[...], axis=-1, keepdims=True))
           + LAM_INIT)
    for hd in range(DA_HEADS):
        a1, a2 = acc_ref[2 * hd], acc_ref[2 * hd + 1]
        o = (a1[:DA_V_DIM] / a1[DA_V_DIM:DA_V_DIM + 1]
             - lam * (a2[:DA_V_DIM] / a2[DA_V_DIM:DA_V_DIM + 1]))
        o = o * lax.rsqrt(jnp.mean(o * o, axis=0, keepdims=True) + EPS)
        o = o * sg_ref[...] * (1.0 - LAM_INIT)
        o_ref[0, :, hd * DA_V_DIM:(hd + 1) * DA_V_DIM] = o.T.astype(BF16)


def _post_mix_kernel(h_ref, oda_ref, glu_ref, halo_ref, qm_ref, mk_ref, mv_ref,
                     dww_ref, dwb_ref, lng_ref, lnb_ref, wout_ref,
                     g2_ref, wg_ref, wu_ref, wd_ref, gf_ref,
                     out_ref, cbuf_ref, act_ref):
    si = pl.program_id(1)
    tm = h_ref.shape[1]

    halo = halo_ref[0]
    cbuf_ref[0:CONV_HALO, :] = jnp.where(si > 0, halo, jnp.zeros_like(halo))
    cbuf_ref[CONV_HALO:CONV_HALO + tm, :] = glu_ref[0]
    base = CONV_HALO - (CONV_WIDTH - 1)
    y = jnp.zeros((tm, CONV_CH), F32)
    for j in range(CONV_WIDTH):
        y = y + dww_ref[j:j + 1, :] * cbuf_ref[base + j:base + j + tm, :]
    y = y + dwb_ref[...]
    mu = jnp.mean(y, axis=-1, keepdims=True)
    yc = y - mu
    var = jnp.mean(yc * yc, axis=-1, keepdims=True)
    y = yc * lax.rsqrt(var + EPS) * lng_ref[...] + lnb_ref[...]
    o_conv = (y * jax.nn.sigmoid(y)).astype(BF16)

    qm = qm_ref[0]
    mk = mk_ref[0]
    mv = mv_ref[0]
    heads = []
    for hd in range(MEM_HEADS):
        sl = slice(hd * MEM_HEAD_DIM, (hd + 1) * MEM_HEAD_DIM)
        s = lax.dot_general(qm[:, sl], mk[:, sl], (((1,), (1,)), ((), ())),
                            preferred_element_type=F32) * (MEM_HEAD_DIM ** -0.5)
        s = s - jnp.max(s, axis=-1, keepdims=True)
        p = jnp.exp(s)
        p = p / jnp.sum(p, axis=-1, keepdims=True)
        heads.append(jnp.dot(p.astype(BF16), mv[:, sl], preferred_element_type=F32))
    o_mem = jnp.concatenate(heads, axis=-1).astype(BF16)

    mix = (jnp.dot(oda_ref[0], wout_ref[0:DA_WIDTH, :], preferred_element_type=F32)
           + jnp.dot(o_conv, wout_ref[DA_WIDTH:DA_WIDTH + CONV_CH, :],
                     preferred_element_type=F32)
           + jnp.dot(o_mem, wout_ref[DA_WIDTH + CONV_CH:, :], preferred_element_type=F32))
    h = h_ref[0] + mix
    hn = _rms(h, g2_ref[...]).astype(BF16)
    h = h + 0.5 * _swiglu(hn, wg_ref, wu_ref, wd_ref, act_ref)
    out_ref[0] = _rms(h, gf_ref[...])


def _resident(shape):
    return pl.BlockSpec(shape, lambda *_: (0,) * len(shape), pipeline_mode=pl.Buffered(1))


def _rope_tables(seq):
    half = DA_HEAD_DIM // 2
    inv_freq = 1.0 / (ROPE_THETA ** (jnp.arange(0, DA_HEAD_DIM, 2, dtype=F32) / DA_HEAD_DIM))
    ang = jnp.arange(seq, dtype=F32)[:, None] * inv_freq[None, :]
    cos, sin = jnp.cos(ang), jnp.sin(ang)
    reps = LANES // DA_HEAD_DIM
    cos_t = jnp.tile(jnp.concatenate([cos, cos], axis=1), (1, reps))
    sin_t = jnp.tile(jnp.concatenate([-sin, sin], axis=1), (1, reps))
    assert cos_t.shape == (seq, LANES) and half * 2 == DA_HEAD_DIM
    return cos_t, sin_t


def kernel(x, mem, ffn1_norm_g, ffn1_w_gate, ffn1_w_up, ffn1_w_down, mix_norm_g, mem_norm_g, w_in, lambda_q1, lambda_k1, lambda_q2, lambda_k2, subln_g, conv_dw_w, conv_dw_b, conv_ln_g, conv_ln_b, w_mem_kv, w_out, ffn2_norm_g, ffn2_w_gate, ffn2_w_up, ffn2_w_down, final_norm_g):
    b, s, d = x.shape
    n_mem = mem.shape[1]
    d_ff = ffn1_w_gate.shape[-1]
    in_width = w_in.shape[-1]
    assert ffn1_norm_g.shape[0] == 1, "single layer"
    tm = TOKEN_TILE
    ta = ATTN_TILE
    assert s % tm == 0 and s % ta == 0 and tm == ta and d_ff % FFN_CHUNK == 0
    t = b * s
    n_tiles = s // tm
    params = functools.partial(pltpu.CompilerParams, vmem_limit_bytes=VMEM_LIMIT)

    cos_t, sin_t = _rope_tables(s)
    row = lambda a: a.reshape(1, -1)
    bf = lambda a: a[0].astype(BF16)

    tok = lambda w: pl.BlockSpec((tm, w), lambda i: (i, 0))
    h1, q, k, vt, glu, qm = pl.pallas_call(
        _pre_mix_kernel,
        grid=(t // tm,),
        in_specs=[tok(d), _resident((1, d)), _resident((d, d_ff)), _resident((d, d_ff)),
                  _resident((d_ff, d)), _resident((1, d)), _resident((d, in_width)),
                  pl.BlockSpec((tm, LANES), lambda i: (i % n_tiles, 0)),
                  pl.BlockSpec((tm, LANES), lambda i: (i % n_tiles, 0))],
        out_specs=[tok(d), tok(DA_WIDTH), tok(DA_WIDTH),
                   pl.BlockSpec((1, DA_HEADS * V_AUG, tm), lambda i: (i, 0, 0)),
                   tok(CONV_CH), tok(MEM_WIDTH)],
        out_shape=[jax.ShapeDtypeStruct((t, d), F32),
                   jax.ShapeDtypeStruct((t, DA_WIDTH), BF16),
                   jax.ShapeDtypeStruct((t, DA_WIDTH), BF16),
                   jax.ShapeDtypeStruct((t // tm, DA_HEADS * V_AUG, tm), BF16),
                   jax.ShapeDtypeStruct((t, CONV_CH), F32),
                   jax.ShapeDtypeStruct((t, MEM_WIDTH), BF16)],
        scratch_shapes=[pltpu.VMEM((tm, d_ff), BF16)],
        compiler_params=params(dimension_semantics=("parallel",)),
        name="pre_mix",
    )(x.reshape(t, d), row(ffn1_norm_g), bf(ffn1_w_gate), bf(ffn1_w_up), bf(ffn1_w_down),
      row(mix_norm_g), bf(w_in), cos_t, sin_t)

    mk, mv = pl.pallas_call(
        _mem_kv_kernel,
        grid=(b,),
        in_specs=[pl.BlockSpec((1, n_mem, d), lambda i: (i, 0, 0)), _resident((1, d)),
                  _resident((d, 2 * MEM_WIDTH))],
        out_specs=[pl.BlockSpec((1, n_mem, MEM_WIDTH), lambda i: (i, 0, 0))] * 2,
        out_shape=[jax.ShapeDtypeStruct((b, n_mem, MEM_WIDTH), BF16)] * 2,
        compiler_params=params(dimension_semantics=("parallel",)),
        name="mem_kv",
    )(mem, row(mem_norm_g), bf(w_mem_kv))

    lam_spec = _resident((1, DA_HEAD_DIM))
    o_da = pl.pallas_call(
        _diff_attn_kernel,
        grid=(b, s // ta),
        in_specs=[lam_spec, lam_spec, lam_spec, lam_spec, _resident((DA_V_DIM, 1)),
                  pl.BlockSpec((1, ta, DA_WIDTH), lambda bi, qi: (bi, qi, 0)),
                  pl.BlockSpec((1, s, DA_WIDTH), lambda bi, qi: (bi, 0, 0),
                               pipeline_mode=pl.Buffered(1)),
                  pl.BlockSpec((1, s // ta, DA_HEADS * V_AUG, ta), lambda bi, qi: (bi, 0, 0, 0),
                               pipeline_mode=pl.Buffered(1))],
        out_specs=pl.BlockSpec((1, ta, DA_WIDTH), lambda bi, qi: (bi, qi, 0)),
        out_shape=jax.ShapeDtypeStruct((b, s, DA_WIDTH), BF16),
        scratch_shapes=[pltpu.VMEM((2 * DA_HEADS, 1, ta), F32),
                        pltpu.VMEM((2 * DA_HEADS, V_AUG, ta), F32),
                        pltpu.VMEM((SCORE_SLOTS, ta, ta), F32),
                        pltpu.VMEM((SCORE_SLOTS, ta, ta), BF16)],
        compiler_params=params(dimension_semantics=("parallel", "arbitrary")),
        name="diff_attn",
    )(lambda_q1, lambda_k1, lambda_q2, lambda_k2, subln_g.reshape(DA_V_DIM, 1),
      q.reshape(b, s, DA_WIDTH), k.reshape(b, s, DA_WIDTH),
      vt.reshape(b, s // ta, DA_HEADS * V_AUG, ta))

    halo_blocks = tm // CONV_HALO
    seq_tile = lambda w: pl.BlockSpec((1, tm, w), lambda bi, si: (bi, si, 0))
    out = pl.pallas_call(
        _post_mix_kernel,
        grid=(b, n_tiles),
        in_specs=[seq_tile(d), seq_tile(DA_WIDTH), seq_tile(CONV_CH),
                  pl.BlockSpec((1, CONV_HALO, CONV_CH),
                               lambda bi, si: (bi, jnp.maximum(si * halo_blocks - 1, 0), 0)),
                  seq_tile(MEM_WIDTH),
                  pl.BlockSpec((1, n_mem, MEM_WIDTH), lambda bi, si: (bi, 0, 0)),
                  pl.BlockSpec((1, n_mem, MEM_WIDTH), lambda bi, si: (bi, 0, 0)),
                  _resident((CONV_WIDTH, CONV_CH)), _resident((1, CONV_CH)),
                  _resident((1, CONV_CH)), _resident((1, CONV_CH)), _resident((d, d)),
                  _resident((1, d)), _resident((d, d_ff)), _resident((d, d_ff)),
                  _resident((d_ff, d)), _resident((1, d))],
        out_specs=seq_tile(d),
        out_shape=jax.ShapeDtypeStruct((b, s, d), F32),
        scratch_shapes=[pltpu.VMEM((CONV_HALO + tm, CONV_CH), F32),
                        pltpu.VMEM((tm, d_ff), BF16)],
        compiler_params=params(dimension_semantics=("parallel", "arbitrary")),
        name="post_mix",
    )(h1.reshape(b, s, d), o_da, glu.reshape(b, s, CONV_CH), glu.reshape(b, s, CONV_CH),
      qm.reshape(b, s, MEM_WIDTH), mk, mv, conv_dw_w[0], conv_dw_b, conv_ln_g, conv_ln_b,
      bf(w_out), row(ffn2_norm_g), bf(ffn2_w_gate), bf(ffn2_w_up), bf(ffn2_w_down),
      row(final_norm_g))
    return out
```

```python
import functools
import math

import jax
import jax.numpy as jnp
from jax import lax
from jax.experimental import pallas as pl
from jax.experimental.pallas import tpu as pltpu

F32 = jnp.float32
BF16 = jnp.bfloat16

EPS = 1e-5
ROPE_THETA = 10000.0
CHUNK = 64
DA_HEADS = 4
DA_HEAD_DIM = 64
DA_V_DIM = 128
V_AUG = DA_V_DIM + 16
DA_WIDTH = 512
CONV_CH = 256
CONV_WIDTH = 31
MEM_HEADS = 4
MEM_HEAD_DIM = 64
MEM_WIDTH = 256
LAM_INIT = 0.8 - 0.6 * math.exp(-0.3 * 0)

V7X_VMEM_BYTES = 64 * 1024 * 1024
VMEM_LIMIT = V7X_VMEM_BYTES - 8 * 1024 * 1024
LANES = 128
MXU_COLS = 256

TOKEN_TILE = 512
ATTN_TILE = 512
FFN_CHUNK = 256
CONV_HALO = 32
SOFTMAX_SLAB = 16
SCORE_AHEAD = 2
SCORE_SLOTS = 4
NEG_BIG = -1e30
LOG2_E = math.log2(math.e)


def _rms(x, g):
    return x * lax.rsqrt(jnp.mean(x * x, axis=-1, keepdims=True) + EPS) * g


def _swiglu(xn, wg_ref, wu_ref, wd_ref, act_ref):
    d_ff = wg_ref.shape[1]
    for c in range(0, d_ff, FFN_CHUNK):
        gate = jnp.dot(xn, wg_ref[:, c:c + FFN_CHUNK], preferred_element_type=F32)
        up = jnp.dot(xn, wu_ref[:, c:c + FFN_CHUNK], preferred_element_type=F32)
        act_ref[:, c:c + FFN_CHUNK] = (gate * jax.nn.sigmoid(gate) * up).astype(BF16)
    return jnp.dot(act_ref[...], wd_ref[...], preferred_element_type=F32)


def _rope(x, cos, sin_signed, first_half):
    partner = jnp.where(first_half,
                        pltpu.roll(x, LANES - DA_HEAD_DIM // 2, axis=1),
                        pltpu.roll(x, DA_HEAD_DIM // 2, axis=1))
    return x * cos + partner * sin_signed


def _pre_mix_kernel(x_ref, g1_ref, wg_ref, wu_ref, wd_ref, gm_ref, win_ref,
                    cos_ref, sin_ref,
                    h_ref, q_ref, k_ref, vt_ref, glu_ref, qm_ref, act_ref):
    x = x_ref[...]
    xn = _rms(x, g1_ref[...]).astype(BF16)
    h = x + 0.5 * _swiglu(xn, wg_ref, wu_ref, wd_ref, act_ref)
    h_ref[...] = h

    n = _rms(h, gm_ref[...]).astype(BF16)
    cos = cos_ref[...]
    sin_signed = sin_ref[...]
    lane = lax.broadcasted_iota(jnp.int32, cos.shape, 1)
    first_half = (lane % DA_HEAD_DIM) < (DA_HEAD_DIM // 2)
    scale = DA_HEAD_DIM ** -0.5 * LOG2_E
    for c0 in range(0, DA_WIDTH, MXU_COLS):
        qs = jnp.dot(n, win_ref[:, c0:c0 + MXU_COLS], preferred_element_type=F32)
        ks = jnp.dot(n, win_ref[:, DA_WIDTH + c0:DA_WIDTH + c0 + MXU_COLS],
                     preferred_element_type=F32)
        for l0 in range(0, MXU_COLS, LANES):
            q_ref[:, c0 + l0:c0 + l0 + LANES] = (
                _rope(qs[:, l0:l0 + LANES], cos, sin_signed, first_half) * scale).astype(BF16)
            k_ref[:, c0 + l0:c0 + l0 + LANES] = _rope(
                ks[:, l0:l0 + LANES], cos, sin_signed, first_half).astype(BF16)
    v = jnp.dot(n, win_ref[:, 2 * DA_WIDTH:3 * DA_WIDTH], preferred_element_type=F32)
    vt = v.T.astype(BF16)
    ones = jnp.ones((V_AUG - DA_V_DIM, vt.shape[1]), BF16)
    for hd in range(DA_HEADS):
        vt_ref[0, hd * V_AUG:hd * V_AUG + DA_V_DIM, :] = vt[hd * DA_V_DIM:(hd + 1) * DA_V_DIM]
        vt_ref[0, hd * V_AUG + DA_V_DIM:(hd + 1) * V_AUG, :] = ones
    o = 3 * DA_WIDTH
    a = jnp.dot(n, win_ref[:, o:o + CONV_CH], preferred_element_type=F32)
    gate = jnp.dot(n, win_ref[:, o + CONV_CH:o + 2 * CONV_CH], preferred_element_type=F32)
    glu_ref[...] = a * jax.nn.sigmoid(gate)
    o += 2 * CONV_CH
    qm_ref[...] = jnp.dot(n, win_ref[:, o:o + MEM_WIDTH],
                          preferred_element_type=F32).astype(BF16)


def _mem_kv_kernel(mem_ref, g_ref, w_ref, mk_ref, mv_ref):
    mn = _rms(mem_ref[0], g_ref[...]).astype(BF16)
    kv = jnp.dot(mn, w_ref[...], preferred_element_type=F32)
    mk_ref[0] = kv[:, :MEM_WIDTH].astype(BF16)
    mv_ref[0] = kv[:, MEM_WIDTH:].astype(BF16)


def _diff_attn_kernel(lq1_ref, lk1_ref, lq2_ref, lk2_ref, sg_ref,
                      q_ref, k_ref, vt_ref, o_ref, m_ref, acc_ref, s_ref, mpart_ref, p_ref):
    qi = pl.program_id(1)
    tq = q_ref.shape[1]
    tk = tq

    lane = lax.broadcasted_iota(jnp.int32, (tq, LANES), 1)
    q_comp = []
    for hd in range(DA_HEADS):
        q = q_ref[0, :, hd * LANES:(hd + 1) * LANES]
        zero = jnp.zeros_like(q)
        q_comp.append(jnp.where(lane < DA_HEAD_DIM, q, zero))
        q_comp.append(jnp.where(lane >= DA_HEAD_DIM, q, zero))

    m_ref[...] = jnp.full(m_ref.shape, NEG_BIG, F32)
    acc_ref[...] = jnp.zeros(acc_ref.shape, F32)

    n_chain = 2 * DA_HEADS
    n_slab = tk // SOFTMAX_SLAB
    rows = lambda r: slice(r * SOFTMAX_SLAB, (r + 1) * SOFTMAX_SLAB)

    def scores(j, i, mask):
        hd = i // 2
        start = pl.multiple_of(j * tk, tk)
        kb = k_ref[0, pl.ds(start, tk), hd * LANES:(hd + 1) * LANES]
        s = lax.dot_general(kb, q_comp[i], (((1,), (1,)), ((), ())),
                            preferred_element_type=F32)
        if mask is not None:
            s = jnp.where(mask, s, NEG_BIG)
        s_ref[i % SCORE_SLOTS] = s
        part = s[rows(0)]
        for r in range(1, n_slab):
            part = jnp.maximum(part, s[rows(r)])
        mpart_ref[i % SCORE_SLOTS] = part

    def softmax_pv(j, i):
        slot = i % SCORE_SLOTS
        hd = i // 2
        vt = vt_ref[0, j, hd * V_AUG:(hd + 1) * V_AUG, :]
        m_old = m_ref[i]
        m_new = jnp.maximum(m_old, jnp.max(mpart_ref[slot], axis=0, keepdims=True))
        alpha = jnp.exp2(m_old - m_new)
        m_ref[i] = m_new
        m_slab = jnp.broadcast_to(m_new, (SOFTMAX_SLAB, tq))
        for r in range(n_slab):
            p_ref[slot, rows(r), :] = jnp.exp2(s_ref[slot, rows(r), :] - m_slab).astype(BF16)
        acc_ref[i] = alpha * acc_ref[i] + jnp.dot(vt, p_ref[slot], preferred_element_type=F32)

    def block(j, j_next, mask):
        for i in range(n_chain):
            ahead = i + SCORE_AHEAD
            if ahead < n_chain:
                scores(j, ahead, mask)
            else:
                scores(j_next, ahead - n_chain, None)
            softmax_pv(j, i)

    key_chunk = lax.broadcasted_iota(jnp.int32, (tk, tq), 0) // CHUNK
    qry_chunk = lax.broadcasted_iota(jnp.int32, (tk, tq), 1) // CHUNK
    diag_mask = key_chunk <= qry_chunk
    for i in range(SCORE_AHEAD):
        scores(qi, i, diag_mask)
    block(qi, 0, diag_mask)

    def body(j, carry):
        block(j, jnp.minimum(j + 1, qi - 1), None)
        return carry

    lax.fori_loop(0, qi, body, 0)

    lam = (jnp.exp(jnp.sum(lq1_ref[...] * lk1_ref[...], axis=-1, keepdims=True))
           - jnp.exp(jnp.sum(lq2_ref[...] * lk2_ref[...], axis=-1, keepdims=True))
           + LAM_INIT)
    for hd in range(DA_HEADS):
        a1, a2 = acc_ref[2 * hd], acc_ref[2 * hd + 1]
        o = (a1[:DA_V_DIM] / a1[DA_V_DIM:DA_V_DIM + 1]
             - lam * (a2[:DA_V_DIM] / a2[DA_V_DIM:DA_V_DIM + 1]))
        o = o * lax.rsqrt(jnp.mean(o * o, axis=0, keepdims=True) + EPS)
        o = o * sg_ref[...] * (1.0 - LAM_INIT)
        o_ref[0, :, hd * DA_V_DIM:(hd + 1) * DA_V_DIM] = o.T.astype(BF16)


def _post_mix_kernel(h_ref, oda_ref, glu_ref, halo_ref, qm_ref, mk_ref, mv_ref,
                     dww_ref, dwb_ref, lng_ref, lnb_ref, wout_ref,
                     g2_ref, wg_ref, wu_ref, wd_ref, gf_ref,
                     out_ref, cbuf_ref, act_ref):
    si = pl.program_id(1)
    tm = h_ref.shape[1]

    halo = halo_ref[0]
    cbuf_ref[0:CONV_HALO, :] = jnp.where(si > 0, halo, jnp.zeros_like(halo))
    cbuf_ref[CONV_HALO:CONV_HALO + tm, :] = glu_ref[0]
    base = CONV_HALO - (CONV_WIDTH - 1)
    y = jnp.zeros((tm, CONV_CH), F32)
    for j in range(CONV_WIDTH):
        y = y + dww_ref[j:j + 1, :] * cbuf_ref[base + j:base + j + tm, :]
    y = y + dwb_ref[...]
    mu = jnp.mean(y, axis=-1, keepdims=True)
    yc = y - mu
    var = jnp.mean(yc * yc, axis=-1, keepdims=True)
    y = yc * lax.rsqrt(var + EPS) * lng_ref[...] + lnb_ref[...]
    o_conv = (y * jax.nn.sigmoid(y)).astype(BF16)

    qm = qm_ref[0]
    mk = mk_ref[0]
    mv = mv_ref[0]
    heads = []
    for hd in range(MEM_HEADS):
        sl = slice(hd * MEM_HEAD_DIM, (hd + 1) * MEM_HEAD_DIM)
        s = lax.dot_general(qm[:, sl], mk[:, sl], (((1,), (1,)), ((), ())),
                            preferred_element_type=F32) * (MEM_HEAD_DIM ** -0.5)
        s = s - jnp.max(s, axis=-1, keepdims=True)
        p = jnp.exp(s)
        p = p / jnp.sum(p, axis=-1, keepdims=True)
        heads.append(jnp.dot(p.astype(BF16), mv[:, sl], preferred_element_type=F32))
    o_mem = jnp.concatenate(heads, axis=-1).astype(BF16)

    mix = (jnp.dot(oda_ref[0], wout_ref[0:DA_WIDTH, :], preferred_element_type=F32)
           + jnp.dot(o_conv, wout_ref[DA_WIDTH:DA_WIDTH + CONV_CH, :],
                     preferred_element_type=F32)
           + jnp.dot(o_mem, wout_ref[DA_WIDTH + CONV_CH:, :], preferred_element_type=F32))
    h = h_ref[0] + mix
    hn = _rms(h, g2_ref[...]).astype(BF16)
    h = h + 0.5 * _swiglu(hn, wg_ref, wu_ref, wd_ref, act_ref)
    out_ref[0] = _rms(h, gf_ref[...])


def _resident(shape):
    return pl.BlockSpec(shape, lambda *_: (0,) * len(shape), pipeline_mode=pl.Buffered(1))


def _rope_tables(seq):
    half = DA_HEAD_DIM // 2
    inv_freq = 1.0 / (ROPE_THETA ** (jnp.arange(0, DA_HEAD_DIM, 2, dtype=F32) / DA_HEAD_DIM))
    ang = jnp.arange(seq, dtype=F32)[:, None] * inv_freq[None, :]
    cos, sin = jnp.cos(ang), jnp.sin(ang)
    reps = LANES // DA_HEAD_DIM
    cos_t = jnp.tile(jnp.concatenate([cos, cos], axis=1), (1, reps))
    sin_t = jnp.tile(jnp.concatenate([-sin, sin], axis=1), (1, reps))
    assert cos_t.shape == (seq, LANES) and half * 2 == DA_HEAD_DIM
    return cos_t, sin_t


def kernel(x, mem, ffn1_norm_g, ffn1_w_gate, ffn1_w_up, ffn1_w_down, mix_norm_g, mem_norm_g, w_in, lambda_q1, lambda_k1, lambda_q2, lambda_k2, subln_g, conv_dw_w, conv_dw_b, conv_ln_g, conv_ln_b, w_mem_kv, w_out, ffn2_norm_g, ffn2_w_gate, ffn2_w_up, ffn2_w_down, final_norm_g):
    b, s, d = x.shape
    n_mem = mem.shape[1]
    d_ff = ffn1_w_gate.shape[-1]
    in_width = w_in.shape[-1]
    assert ffn1_norm_g.shape[0] == 1, "single layer"
    tm = TOKEN_TILE
    ta = ATTN_TILE
    assert s % tm == 0 and s % ta == 0 and tm == ta and d_ff % FFN_CHUNK == 0
    t = b * s
    n_tiles = s // tm
    params = functools.partial(pltpu.CompilerParams, vmem_limit_bytes=VMEM_LIMIT)

    cos_t, sin_t = _rope_tables(s)
    row = lambda a: a.reshape(1, -1)
    bf = lambda a: a[0].astype(BF16)

    tok = lambda w: pl.BlockSpec((tm, w), lambda i: (i, 0))
    h1, q, k, vt, glu, qm = pl.pallas_call(
        _pre_mix_kernel,
        grid=(t // tm,),
        in_specs=[tok(d), _resident((1, d)), _resident((d, d_ff)), _resident((d, d_ff)),
                  _resident((d_ff, d)), _resident((1, d)), _resident((d, in_width)),
                  pl.BlockSpec((tm, LANES), lambda i: (i % n_tiles, 0)),
                  pl.BlockSpec((tm, LANES), lambda i: (i % n_tiles, 0))],
        out_specs=[tok(d), tok(DA_WIDTH), tok(DA_WIDTH),
                   pl.BlockSpec((1, DA_HEADS * V_AUG, tm), lambda i: (i, 0, 0)),
                   tok(CONV_CH), tok(MEM_WIDTH)],
        out_shape=[jax.ShapeDtypeStruct((t, d), F32),
                   jax.ShapeDtypeStruct((t, DA_WIDTH), BF16),
                   jax.ShapeDtypeStruct((t, DA_WIDTH), BF16),
                   jax.ShapeDtypeStruct((t // tm, DA_HEADS * V_AUG, tm), BF16),
                   jax.ShapeDtypeStruct((t, CONV_CH), F32),
                   jax.ShapeDtypeStruct((t, MEM_WIDTH), BF16)],
        scratch_shapes=[pltpu.VMEM((tm, d_ff), BF16)],
        compiler_params=params(dimension_semantics=("parallel",)),
        name="pre_mix",
    )(x.reshape(t, d), row(ffn1_norm_g), bf(ffn1_w_gate), bf(ffn1_w_up), bf(ffn1_w_down),
      row(mix_norm_g), bf(w_in), cos_t, sin_t)

    mk, mv = pl.pallas_call(
        _mem_kv_kernel,
        grid=(b,),
        in_specs=[pl.BlockSpec((1, n_mem, d), lambda i: (i, 0, 0)), _resident((1, d)),
                  _resident((d, 2 * MEM_WIDTH))],
        out_specs=[pl.BlockSpec((1, n_mem, MEM_WIDTH), lambda i: (i, 0, 0))] * 2,
        out_shape=[jax.ShapeDtypeStruct((b, n_mem, MEM_WIDTH), BF16)] * 2,
        compiler_params=params(dimension_semantics=("parallel",)),
        name="mem_kv",
    )(mem, row(mem_norm_g), bf(w_mem_kv))

    lam_spec = _resident((1, DA_HEAD_DIM))
    o_da = pl.pallas_call(
        _diff_attn_kernel,
        grid=(b, s // ta),
        in_specs=[lam_spec, lam_spec, lam_spec, lam_spec, _resident((DA_V_DIM, 1)),
                  pl.BlockSpec((1, ta, DA_WIDTH), lambda bi, qi: (bi, qi, 0)),
                  pl.BlockSpec((1, s, DA_WIDTH), lambda bi, qi: (bi, 0, 0),
                               pipeline_mode=pl.Buffered(1)),
                  pl.BlockSpec((1, s // ta, DA_HEADS * V_AUG, ta), lambda bi, qi: (bi, 0, 0, 0),
                               pipeline_mode=pl.Buffered(1))],
        out_specs=pl.BlockSpec((1, ta, DA_WIDTH), lambda bi, qi: (bi, qi, 0)),
        out_shape=jax.ShapeDtypeStruct((b, s, DA_WIDTH), BF16),
        scratch_shapes=[pltpu.VMEM((2 * DA_HEADS, 1, ta), F32),
                        pltpu.VMEM((2 * DA_HEADS, V_AUG, ta), F32),
                        pltpu.VMEM((SCORE_SLOTS, ta, ta), F32),
                        pltpu.VMEM((SCORE_SLOTS, SOFTMAX_SLAB, ta), F32),
                        pltpu.VMEM((SCORE_SLOTS, ta, ta), BF16)],
        compiler_params=params(dimension_semantics=("parallel", "arbitrary")),
        name="diff_attn",
    )(lambda_q1, lambda_k1, lambda_q2, lambda_k2, subln_g.reshape(DA_V_DIM, 1),
      q.reshape(b, s, DA_WIDTH), k.reshape(b, s, DA_WIDTH),
      vt.reshape(b, s // ta, DA_HEADS * V_AUG, ta))

    halo_blocks = tm // CONV_HALO
    seq_tile = lambda w: pl.BlockSpec((1, tm, w), lambda bi, si: (bi, si, 0))
    out = pl.pallas_call(
        _post_mix_kernel,
        grid=(b, n_tiles),
        in_specs=[seq_tile(d), seq_tile(DA_WIDTH), seq_tile(CONV_CH),
                  pl.BlockSpec((1, CONV_HALO, CONV_CH),
                               lambda bi, si: (bi, jnp.maximum(si * halo_blocks - 1, 0), 0)),
                  seq_tile(MEM_WIDTH),
                  pl.BlockSpec((1, n_mem, MEM_WIDTH), lambda bi, si: (bi, 0, 0)),
                  pl.BlockSpec((1, n_mem, MEM_WIDTH), lambda bi, si: (bi, 0, 0)),
                  _resident((CONV_WIDTH, CONV_CH)), _resident((1, CONV_CH)),
                  _resident((1, CONV_CH)), _resident((1, CONV_CH)), _resident((d, d)),
                  _resident((1, d)), _resident((d, d_ff)), _resident((d, d_ff)),
                  _resident((d_ff, d)), _resident((1, d))],
        out_specs=seq_tile(d),
        out_shape=jax.ShapeDtypeStruct((b, s, d), F32),
        scratch_shapes=[pltpu.VMEM((CONV_HALO + tm, CONV_CH), F32),
                        pltpu.VMEM((tm, d_ff), BF16)],
        compiler_params=params(dimension_semantics=("parallel", "arbitrary")),
        name="post_mix",
    )(h1.reshape(b, s, d), o_da, glu.reshape(b, s, CONV_CH), glu.reshape(b, s, CONV_CH),
      qm.reshape(b, s, MEM_WIDTH), mk, mv, conv_dw_w[0], conv_dw_b, conv_ln_g, conv_ln_b,
      bf(w_out), row(ffn2_norm_g), bf(ffn2_w_gate), bf(ffn2_w_up), bf(ffn2_w_down),
      row(final_norm_g))
    return out
```

```python
import functools
import math

import jax
import jax.numpy as jnp
from jax import lax
from jax.experimental import pallas as pl
from jax.experimental.pallas import tpu as pltpu

F32 = jnp.float32
BF16 = jnp.bfloat16

EPS = 1e-5
ROPE_THETA = 10000.0
CHUNK = 64
DA_HEADS = 4
DA_HEAD_DIM = 64
DA_V_DIM = 128
V_AUG = DA_V_DIM + 16
DA_WIDTH = 512
CONV_CH = 256
CONV_WIDTH = 31
MEM_HEADS = 4
MEM_HEAD_DIM = 64
MEM_WIDTH = 256
LAM_INIT = 0.8 - 0.6 * math.exp(-0.3 * 0)

V7X_VMEM_BYTES = 64 * 1024 * 1024
VMEM_LIMIT = V7X_VMEM_BYTES - 8 * 1024 * 1024
LANES = 128
MXU_COLS = 256

TOKEN_TILE = 512
ATTN_TILE = 512
FFN_CHUNK = 256
CONV_HALO = 32
SOFTMAX_SLAB = 16
SCORE_AHEAD = 2
SCORE_SLOTS = 4
NEG_BIG = -1e30
LOG2_E = math.log2(math.e)


def _rms(x, g):
    return x * lax.rsqrt(jnp.mean(x * x, axis=-1, keepdims=True) + EPS) * g


def _swiglu(xn, wg_ref, wu_ref, wd_ref, act_ref):
    d_ff = wg_ref.shape[1]
    for c in range(0, d_ff, FFN_CHUNK):
        gate = jnp.dot(xn, wg_ref[:, c:c + FFN_CHUNK], preferred_element_type=F32)
        up = jnp.dot(xn, wu_ref[:, c:c + FFN_CHUNK], preferred_element_type=F32)
        act_ref[:, c:c + FFN_CHUNK] = (gate * jax.nn.sigmoid(gate) * up).astype(BF16)
    return jnp.dot(act_ref[...], wd_ref[...], preferred_element_type=F32)


def _rope(x, cos, sin_signed, first_half):
    partner = jnp.where(first_half,
                        pltpu.roll(x, LANES - DA_HEAD_DIM // 2, axis=1),
                        pltpu.roll(x, DA_HEAD_DIM // 2, axis=1))
    return x * cos + partner * sin_signed


def _pre_mix_kernel(x_ref, g1_ref, wg_ref, wu_ref, wd_ref, gm_ref, win_ref,
                    cos_ref, sin_ref,
                    h_ref, q_ref, k_ref, vt_ref, glu_ref, qm_ref, act_ref):
    x = x_ref[...]
    xn = _rms(x, g1_ref[...]).astype(BF16)
    h = x + 0.5 * _swiglu(xn, wg_ref, wu_ref, wd_ref, act_ref)
    h_ref[...] = h

    n = _rms(h, gm_ref[...]).astype(BF16)
    cos = cos_ref[...]
    sin_signed = sin_ref[...]
    lane = lax.broadcasted_iota(jnp.int32, cos.shape, 1)
    first_half = (lane % DA_HEAD_DIM) < (DA_HEAD_DIM // 2)
    scale = DA_HEAD_DIM ** -0.5 * LOG2_E
    for c0 in range(0, DA_WIDTH, MXU_COLS):
        qs = jnp.dot(n, win_ref[:, c0:c0 + MXU_COLS], preferred_element_type=F32)
        ks = jnp.dot(n, win_ref[:, DA_WIDTH + c0:DA_WIDTH + c0 + MXU_COLS],
                     preferred_element_type=F32)
        for l0 in range(0, MXU_COLS, LANES):
            q_ref[:, c0 + l0:c0 + l0 + LANES] = (
                _rope(qs[:, l0:l0 + LANES], cos, sin_signed, first_half) * scale).astype(BF16)
            k_ref[:, c0 + l0:c0 + l0 + LANES] = _rope(
                ks[:, l0:l0 + LANES], cos, sin_signed, first_half).astype(BF16)
    v = jnp.dot(n, win_ref[:, 2 * DA_WIDTH:3 * DA_WIDTH], preferred_element_type=F32)
    vt = v.T.astype(BF16)
    ones = jnp.ones((V_AUG - DA_V_DIM, vt.shape[1]), BF16)
    for hd in range(DA_HEADS):
        vt_ref[0, hd * V_AUG:hd * V_AUG + DA_V_DIM, :] = vt[hd * DA_V_DIM:(hd + 1) * DA_V_DIM]
        vt_ref[0, hd * V_AUG + DA_V_DIM:(hd + 1) * V_AUG, :] = ones
    o = 3 * DA_WIDTH
    a = jnp.dot(n, win_ref[:, o:o + CONV_CH], preferred_element_type=F32)
    gate = jnp.dot(n, win_ref[:, o + CONV_CH:o + 2 * CONV_CH], preferred_element_type=F32)
    glu_ref[...] = a * jax.nn.sigmoid(gate)
    o += 2 * CONV_CH
    qm_ref[...] = jnp.dot(n, win_ref[:, o:o + MEM_WIDTH],
                          preferred_element_type=F32).astype(BF16)


def _mem_kv_kernel(mem_ref, g_ref, w_ref, mk_ref, mv_ref):
    mn = _rms(mem_ref[0], g_ref[...]).astype(BF16)
    kv = jnp.dot(mn, w_ref[...], preferred_element_type=F32)
    mk_ref[0] = kv[:, :MEM_WIDTH].astype(BF16)
    mv_ref[0] = kv[:, MEM_WIDTH:].astype(BF16)


def _diff_attn_kernel(lq1_ref, lk1_ref, lq2_ref, lk2_ref, sg_ref,
                      q_ref, k_ref, vt_ref, o_ref, m_ref, acc_ref, s_ref, mpart_ref, p_ref):
    qi = pl.program_id(1)
    tq = q_ref.shape[1]
    tk = tq

    lane = lax.broadcasted_iota(jnp.int32, (tq, LANES), 1)
    q_comp = []
    for hd in range(DA_HEADS):
        q = q_ref[0, :, hd * LANES:(hd + 1) * LANES]
        zero = jnp.zeros_like(q)
        q_comp.append(jnp.where(lane < DA_HEAD_DIM, q, zero))
        q_comp.append(jnp.where(lane >= DA_HEAD_DIM, q, zero))

    m_ref[...] = jnp.full(m_ref.shape, NEG_BIG, F32)
    acc_ref[...] = jnp.zeros(acc_ref.shape, F32)

    n_chain = 2 * DA_HEADS
    n_slab = tk // SOFTMAX_SLAB
    rows = lambda r: slice(r * SOFTMAX_SLAB, (r + 1) * SOFTMAX_SLAB)

    def scores(j, i, mask):
        hd = i // 2
        start = pl.multiple_of(j * tk, tk)
        kb = k_ref[0, pl.ds(start, tk), hd * LANES:(hd + 1) * LANES]
        s = lax.dot_general(kb, q_comp[i], (((1,), (1,)), ((), ())),
                            preferred_element_type=F32)
        if mask is not None:
            s = jnp.where(mask, s, NEG_BIG)
        s_ref[i % SCORE_SLOTS] = s
        part = s[rows(0)]
        for r in range(1, n_slab):
            part = jnp.maximum(part, s[rows(r)])
        mpart_ref[i % SCORE_SLOTS] = part

    def softmax_pv(j, i):
        slot = i % SCORE_SLOTS
        hd = i // 2
        vt = vt_ref[0, j, hd * V_AUG:(hd + 1) * V_AUG, :]
        m_old = m_ref[i]
        m_new = jnp.maximum(m_old, jnp.max(mpart_ref[slot], axis=0, keepdims=True))
        alpha = jnp.exp2(m_old - m_new)
        m_ref[i] = m_new
        m_slab = jnp.broadcast_to(m_new, (SOFTMAX_SLAB, tq))
        p = jnp.concatenate(
            [jnp.exp2(s_ref[slot, rows(r), :] - m_slab).astype(BF16) for r in range(n_slab)],
            axis=0)
        acc_ref[i] = alpha * acc_ref[i] + jnp.dot(vt, p, preferred_element_type=F32)

    def block(j, j_next, mask):
        for i in range(n_chain):
            ahead = i + SCORE_AHEAD
            if ahead < n_chain:
                scores(j, ahead, mask)
            else:
                scores(j_next, ahead - n_chain, None)
            softmax_pv(j, i)

    key_chunk = lax.broadcasted_iota(jnp.int32, (tk, tq), 0) // CHUNK
    qry_chunk = lax.broadcasted_iota(jnp.int32, (tk, tq), 1) // CHUNK
    diag_mask = key_chunk <= qry_chunk
    for i in range(SCORE_AHEAD):
        scores(qi, i, diag_mask)
    block(qi, 0, diag_mask)

    def body(j, carry):
        block(j, jnp.minimum(j + 1, qi - 1), None)
        return carry

    lax.fori_loop(0, qi, body, 0)

    lam = (jnp.exp(jnp.sum(lq1_ref[...] * lk1_ref[...], axis=-1, keepdims=True))
           - jnp.exp(jnp.sum(lq2_ref[...] * lk2_ref[...], axis=-1, keepdims=True))
           + LAM_INIT)
    for hd in range(DA_HEADS):
        a1, a2 = acc_ref[2 * hd], acc_ref[2 * hd + 1]
        o = (a1[:DA_V_DIM] / a1[DA_V_DIM:DA_V_DIM + 1]
             - lam * (a2[:DA_V_DIM] / a2[DA_V_DIM:DA_V_DIM + 1]))
        o = o * lax.rsqrt(jnp.mean(o * o, axis=0, keepdims=True) + EPS)
        o = o * sg_ref[...] * (1.0 - LAM_INIT)
        o_ref[0, :, hd * DA_V_DIM:(hd + 1) * DA_V_DIM] = o.T.astype(BF16)


def _post_mix_kernel(h_ref, oda_ref, glu_ref, halo_ref, qm_ref, mk_ref, mv_ref,
                     dww_ref, dwb_ref, lng_ref, lnb_ref, wout_ref,
                     g2_ref, wg_ref, wu_ref, wd_ref, gf_ref,
                     out_ref, cbuf_ref, act_ref):
    si = pl.program_id(1)
    tm = h_ref.shape[1]

    halo = halo_ref[0]
    cbuf_ref[0:CONV_HALO, :] = jnp.where(si > 0, halo, jnp.zeros_like(halo))
    cbuf_ref[CONV_HALO:CONV_HALO + tm, :] = glu_ref[0]
    base = CONV_HALO - (CONV_WIDTH - 1)
    y = jnp.zeros((tm, CONV_CH), F32)
    for j in range(CONV_WIDTH):
        y = y + dww_ref[j:j + 1, :] * cbuf_ref[base + j:base + j + tm, :]
    y = y + dwb_ref[...]
    mu = jnp.mean(y, axis=-1, keepdims=True)
    yc = y - mu
    var = jnp.mean(yc * yc, axis=-1, keepdims=True)
    y = yc * lax.rsqrt(var + EPS) * lng_ref[...] + lnb_ref[...]
    o_conv = (y * jax.nn.sigmoid(y)).astype(BF16)

    qm = qm_ref[0]
    mk = mk_ref[0]
    mv = mv_ref[0]
    heads = []
    for hd in range(MEM_HEADS):
        sl = slice(hd * MEM_HEAD_DIM, (hd + 1) * MEM_HEAD_DIM)
        s = lax.dot_general(qm[:, sl], mk[:, sl], (((1,), (1,)), ((), ())),
                            preferred_element_type=F32) * (MEM_HEAD_DIM ** -0.5)
        s = s - jnp.max(s, axis=-1, keepdims=True)
        p = jnp.exp(s)
        p = p / jnp.sum(p, axis=-1, keepdims=True)
        heads.append(jnp.dot(p.astype(BF16), mv[:, sl], preferred_element_type=F32))
    o_mem = jnp.concatenate(heads, axis=-1).astype(BF16)

    mix = (jnp.dot(oda_ref[0], wout_ref[0:DA_WIDTH, :], preferred_element_type=F32)
           + jnp.dot(o_conv, wout_ref[DA_WIDTH:DA_WIDTH + CONV_CH, :],
                     preferred_element_type=F32)
           + jnp.dot(o_mem, wout_ref[DA_WIDTH + CONV_CH:, :], preferred_element_type=F32))
    h = h_ref[0] + mix
    hn = _rms(h, g2_ref[...]).astype(BF16)
    h = h + 0.5 * _swiglu(hn, wg_ref, wu_ref, wd_ref, act_ref)
    out_ref[0] = _rms(h, gf_ref[...])


def _resident(shape):
    return pl.BlockSpec(shape, lambda *_: (0,) * len(shape), pipeline_mode=pl.Buffered(1))


def _rope_tables(seq):
    half = DA_HEAD_DIM // 2
    inv_freq = 1.0 / (ROPE_THETA ** (jnp.arange(0, DA_HEAD_DIM, 2, dtype=F32) / DA_HEAD_DIM))
    ang = jnp.arange(seq, dtype=F32)[:, None] * inv_freq[None, :]
    cos, sin = jnp.cos(ang), jnp.sin(ang)
    reps = LANES // DA_HEAD_DIM
    cos_t = jnp.tile(jnp.concatenate([cos, cos], axis=1), (1, reps))
    sin_t = jnp.tile(jnp.concatenate([-sin, sin], axis=1), (1, reps))
    assert cos_t.shape == (seq, LANES) and half * 2 == DA_HEAD_DIM
    return cos_t, sin_t


def kernel(x, mem, ffn1_norm_g, ffn1_w_gate, ffn1_w_up, ffn1_w_down, mix_norm_g, mem_norm_g, w_in, lambda_q1, lambda_k1, lambda_q2, lambda_k2, subln_g, conv_dw_w, conv_dw_b, conv_ln_g, conv_ln_b, w_mem_kv, w_out, ffn2_norm_g, ffn2_w_gate, ffn2_w_up, ffn2_w_down, final_norm_g):
    b, s, d = x.shape
    n_mem = mem.shape[1]
    d_ff = ffn1_w_gate.shape[-1]
    in_width = w_in.shape[-1]
    assert ffn1_norm_g.shape[0] == 1, "single layer"
    tm = TOKEN_TILE
    ta = ATTN_TILE
    assert s % tm == 0 and s % ta == 0 and tm == ta and d_ff % FFN_CHUNK == 0
    t = b * s
    n_tiles = s // tm
    params = functools.partial(pltpu.CompilerParams, vmem_limit_bytes=VMEM_LIMIT)

    cos_t, sin_t = _rope_tables(s)
    row = lambda a: a.reshape(1, -1)
    bf = lambda a: a[0].astype(BF16)

    tok = lambda w: pl.BlockSpec((tm, w), lambda i: (i, 0))
    h1, q, k, vt, glu, qm = pl.pallas_call(
        _pre_mix_kernel,
        grid=(t // tm,),
        in_specs=[tok(d), _resident((1, d)), _resident((d, d_ff)), _resident((d, d_ff)),
                  _resident((d_ff, d)), _resident((1, d)), _resident((d, in_width)),
                  pl.BlockSpec((tm, LANES), lambda i: (i % n_tiles, 0)),
                  pl.BlockSpec((tm, LANES), lambda i: (i % n_tiles, 0))],
        out_specs=[tok(d), tok(DA_WIDTH), tok(DA_WIDTH),
                   pl.BlockSpec((1, DA_HEADS * V_AUG, tm), lambda i: (i, 0, 0)),
                   tok(CONV_CH), tok(MEM_WIDTH)],
        out_shape=[jax.ShapeDtypeStruct((t, d), F32),
                   jax.ShapeDtypeStruct((t, DA_WIDTH), BF16),
                   jax.ShapeDtypeStruct((t, DA_WIDTH), BF16),
                   jax.ShapeDtypeStruct((t // tm, DA_HEADS * V_AUG, tm), BF16),
                   jax.ShapeDtypeStruct((t, CONV_CH), F32),
                   jax.ShapeDtypeStruct((t, MEM_WIDTH), BF16)],
        scratch_shapes=[pltpu.VMEM((tm, d_ff), BF16)],
        compiler_params=params(dimension_semantics=("parallel",)),
        name="pre_mix",
    )(x.reshape(t, d), row(ffn1_norm_g), bf(ffn1_w_gate), bf(ffn1_w_up), bf(ffn1_w_down),
      row(mix_norm_g), bf(w_in), cos_t, sin_t)

    mk, mv = pl.pallas_call(
        _mem_kv_kernel,
        grid=(b,),
        in_specs=[pl.BlockSpec((1, n_mem, d), lambda i: (i, 0, 0)), _resident((1, d)),
                  _resident((d, 2 * MEM_WIDTH))],
        out_specs=[pl.BlockSpec((1, n_mem, MEM_WIDTH), lambda i: (i, 0, 0))] * 2,
        out_shape=[jax.ShapeDtypeStruct((b, n_mem, MEM_WIDTH), BF16)] * 2,
        compiler_params=params(dimension_semantics=("parallel",)),
        name="mem_kv",
    )(mem, row(mem_norm_g), bf(w_mem_kv))

    lam_spec = _resident((1, DA_HEAD_DIM))
    o_da = pl.pallas_call(
        _diff_attn_kernel,
        grid=(b, s // ta),
        in_specs=[lam_spec, lam_spec, lam_spec, lam_spec, _resident((DA_V_DIM, 1)),
                  pl.BlockSpec((1, ta, DA_WIDTH), lambda bi, qi: (bi, qi, 0)),
                  pl.BlockSpec((1, s, DA_WIDTH), lambda bi, qi: (bi, 0, 0),
                               pipeline_mode=pl.Buffered(1)),
                  pl.BlockSpec((1, s // ta, DA_HEADS * V_AUG, ta), lambda bi, qi: (bi, 0, 0, 0),
                               pipeline_mode=pl.Buffered(1))],
        out_specs=pl.BlockSpec((1, ta, DA_WIDTH), lambda bi, qi: (bi, qi, 0)),
        out_shape=jax.ShapeDtypeStruct((b, s, DA_WIDTH), BF16),
        scratch_shapes=[pltpu.VMEM((2 * DA_HEADS, 1, ta), F32),
                        pltpu.VMEM((2 * DA_HEADS, V_AUG, ta), F32),
                        pltpu.VMEM((SCORE_SLOTS, ta, ta), F32),
                        pltpu.VMEM((SCORE_SLOTS, SOFTMAX_SLAB, ta), F32),
                        pltpu.VMEM((SCORE_SLOTS, ta, ta), BF16)],
        compiler_params=params(dimension_semantics=("parallel", "arbitrary")),
        name="diff_attn",
    )(lambda_q1, lambda_k1, lambda_q2, lambda_k2, subln_g.reshape(DA_V_DIM, 1),
      q.reshape(b, s, DA_WIDTH), k.reshape(b, s, DA_WIDTH),
      vt.reshape(b, s // ta, DA_HEADS * V_AUG, ta))

    halo_blocks = tm // CONV_HALO
    seq_tile = lambda w: pl.BlockSpec((1, tm, w), lambda bi, si: (bi, si, 0))
    out = pl.pallas_call(
        _post_mix_kernel,
        grid=(b, n_tiles),
        in_specs=[seq_tile(d), seq_tile(DA_WIDTH), seq_tile(CONV_CH),
                  pl.BlockSpec((1, CONV_HALO, CONV_CH),
                               lambda bi, si: (bi, jnp.maximum(si * halo_blocks - 1, 0), 0)),
                  seq_tile(MEM_WIDTH),
                  pl.BlockSpec((1, n_mem, MEM_WIDTH), lambda bi, si: (bi, 0, 0)),
                  pl.BlockSpec((1, n_mem, MEM_WIDTH), lambda bi, si: (bi, 0, 0)),
                  _resident((CONV_WIDTH, CONV_CH)), _resident((1, CONV_CH)),
                  _resident((1, CONV_CH)), _resident((1, CONV_CH)), _resident((d, d)),
                  _resident((1, d)), _resident((d, d_ff)), _resident((d, d_ff)),
                  _resident((d_ff, d)), _resident((1, d))],
        out_specs=seq_tile(d),
        out_shape=jax.ShapeDtypeStruct((b, s, d), F32),
        scratch_shapes=[pltpu.VMEM((CONV_HALO + tm, CONV_CH), F32),
                        pltpu.VMEM((tm, d_ff), BF16)],
        compiler_params=params(dimension_semantics=("parallel", "arbitrary")),
        name="post_mix",
    )(h1.reshape(b, s, d), o_da, glu.reshape(b, s, CONV_CH), glu.reshape(b, s, CONV_CH),
      qm.reshape(b, s, MEM_WIDTH), mk, mv, conv_dw_w[0], conv_dw_b, conv_ln_g, conv_ln_b,
      bf(w_out), row(ffn2_norm_g), bf(ffn2_w_gate), bf(ffn2_w_up), bf(ffn2_w_down),
      row(final_norm_g))
    return out
```

```python
import functools
import math

import jax
import jax.numpy as jnp
from jax import lax
from jax.experimental import pallas as pl
from jax.experimental.pallas import tpu as pltpu

F32 = jnp.float32
BF16 = jnp.bfloat16

EPS = 1e-5
ROPE_THETA = 10000.0
CHUNK = 64
DA_HEADS = 4
DA_HEAD_DIM = 64
DA_V_DIM = 128
V_AUG = DA_V_DIM + 16
DA_WIDTH = 512
CONV_CH = 256
CONV_WIDTH = 31
MEM_HEADS = 4
MEM_HEAD_DIM = 64
MEM_WIDTH = 256
LAM_INIT = 0.8 - 0.6 * math.exp(-0.3 * 0)

V7X_VMEM_BYTES = 64 * 1024 * 1024
VMEM_LIMIT = V7X_VMEM_BYTES - 8 * 1024 * 1024
LANES = 128
SUBLANES = 8
MXU_COLS = 256

TOKEN_TILE = 512
ATTN_TILE = 512
FFN_CHUNK = 256
CONV_HALO = 32
CONV_ROWS = 64
SOFTMAX_SLAB = 16
SCORE_AHEAD = 2
SCORE_SLOTS = 4
NEG_BIG = -1e30
LOG2_E = math.log2(math.e)


def _rms(x, g):
    return x * lax.rsqrt(jnp.mean(x * x, axis=-1, keepdims=True) + EPS) * g


def _swiglu(xn, wgu_ref, wd_ref, act_ref):
    d_ff = wd_ref.shape[0]
    for c in range(0, d_ff, FFN_CHUNK):
        gu = jnp.dot(xn, wgu_ref[:, 2 * c:2 * c + 2 * FFN_CHUNK], preferred_element_type=F32)
        gate, up = gu[:, :FFN_CHUNK], gu[:, FFN_CHUNK:]
        act_ref[:, c:c + FFN_CHUNK] = (gate * jax.nn.sigmoid(gate) * up).astype(BF16)
    return jnp.dot(act_ref[...], wd_ref[...], preferred_element_type=F32)


def _interleave_gate_up(w_gate, w_up):
    d, f = w_gate.shape
    g = w_gate.astype(BF16).reshape(d, f // FFN_CHUNK, FFN_CHUNK)
    u = w_up.astype(BF16).reshape(d, f // FFN_CHUNK, FFN_CHUNK)
    return jnp.concatenate([g, u], axis=2).reshape(d, 2 * f)


def _rope(x, cos, sin_signed, first_half):
    partner = jnp.where(first_half,
                        pltpu.roll(x, LANES - DA_HEAD_DIM // 2, axis=1),
                        pltpu.roll(x, DA_HEAD_DIM // 2, axis=1))
    return x * cos + partner * sin_signed


def _pre_mix_kernel(x_ref, g1_ref, wgu_ref, wd_ref, gm_ref, win_ref,
                    cos_ref, sin_ref,
                    h_ref, q_ref, k_ref, vt_ref, glu_ref, qm_ref, act_ref):
    x = x_ref[...]
    xn = _rms(x, g1_ref[...]).astype(BF16)
    h = x + 0.5 * _swiglu(xn, wgu_ref, wd_ref, act_ref)
    h_ref[...] = h

    n = _rms(h, gm_ref[...]).astype(BF16)
    cos = cos_ref[...]
    sin_signed = sin_ref[...]
    lane = lax.broadcasted_iota(jnp.int32, cos.shape, 1)
    first_half = (lane % DA_HEAD_DIM) < (DA_HEAD_DIM // 2)
    scale = DA_HEAD_DIM ** -0.5 * LOG2_E
    for c0 in range(0, DA_WIDTH, MXU_COLS):
        qs = jnp.dot(n, win_ref[:, c0:c0 + MXU_COLS], preferred_element_type=F32)
        ks = jnp.dot(n, win_ref[:, DA_WIDTH + c0:DA_WIDTH + c0 + MXU_COLS],
                     preferred_element_type=F32)
        for l0 in range(0, MXU_COLS, LANES):
            q_ref[:, c0 + l0:c0 + l0 + LANES] = (
                _rope(qs[:, l0:l0 + LANES], cos, sin_signed, first_half) * scale).astype(BF16)
            k_ref[:, c0 + l0:c0 + l0 + LANES] = _rope(
                ks[:, l0:l0 + LANES], cos, sin_signed, first_half).astype(BF16)
    v = jnp.dot(n, win_ref[:, 2 * DA_WIDTH:3 * DA_WIDTH], preferred_element_type=F32)
    vt = v.T.astype(BF16)
    ones = jnp.ones((V_AUG - DA_V_DIM, vt.shape[1]), BF16)
    for hd in range(DA_HEADS):
        vt_ref[0, hd * V_AUG:hd * V_AUG + DA_V_DIM, :] = vt[hd * DA_V_DIM:(hd + 1) * DA_V_DIM]
        vt_ref[0, hd * V_AUG + DA_V_DIM:(hd + 1) * V_AUG, :] = ones
    o = 3 * DA_WIDTH
    a = jnp.dot(n, win_ref[:, o:o + CONV_CH], preferred_element_type=F32)
    gate = jnp.dot(n, win_ref[:, o + CONV_CH:o + 2 * CONV_CH], preferred_element_type=F32)
    glu_ref[...] = a * jax.nn.sigmoid(gate)
    o += 2 * CONV_CH
    qm = jnp.dot(n, win_ref[:, o:o + MEM_WIDTH], preferred_element_type=F32)
    qm_ref[...] = (qm * MEM_HEAD_DIM ** -0.5).astype(BF16)


def _mem_kv_kernel(mem_ref, g_ref, w_ref, mk_ref, mv_ref):
    n_mem = mem_ref.shape[1]
    mn = _rms(mem_ref[0], g_ref[...]).astype(BF16)
    kv = jnp.dot(mn, w_ref[...], preferred_element_type=F32)
    mk_ref[0] = kv[:, :MEM_WIDTH].astype(BF16)
    mv = kv[:, MEM_WIDTH:]
    head_of_lane = lax.broadcasted_iota(jnp.int32, mv.shape, 1) // MEM_HEAD_DIM
    for hd in range(MEM_HEADS):
        mv_ref[0, hd * n_mem:(hd + 1) * n_mem, :] = jnp.where(
            head_of_lane == hd, mv, 0.0).astype(BF16)


def _diff_attn_kernel(lq1_ref, lk1_ref, lq2_ref, lk2_ref, sg_ref,
                      q_ref, k_ref, vt_ref, o_ref, m_ref, acc_ref, s_ref, mpart_ref):
    qi = pl.program_id(1)
    tq = q_ref.shape[1]
    tk = tq

    lane = lax.broadcasted_iota(jnp.int32, (tq, LANES), 1)
    q_comp = []
    for hd in range(DA_HEADS):
        q = q_ref[0, :, hd * LANES:(hd + 1) * LANES]
        zero = jnp.zeros_like(q)
        q_comp.append(jnp.where(lane < DA_HEAD_DIM, q, zero))
        q_comp.append(jnp.where(lane >= DA_HEAD_DIM, q, zero))

    m_ref[...] = jnp.full(m_ref.shape, NEG_BIG, F32)
    acc_ref[...] = jnp.zeros(acc_ref.shape, F32)

    n_chain = 2 * DA_HEADS
    n_slab = tk // SOFTMAX_SLAB
    rows = lambda r: slice(r * SOFTMAX_SLAB, (r + 1) * SOFTMAX_SLAB)

    def scores(j, i, mask):
        hd = i // 2
        start = pl.multiple_of(j * tk, tk)
        kb = k_ref[0, pl.ds(start, tk), hd * LANES:(hd + 1) * LANES]
        s = lax.dot_general(kb, q_comp[i], (((1,), (1,)), ((), ())),
                            preferred_element_type=F32)
        if mask is not None:
            s = jnp.where(mask, s, NEG_BIG)
        s_ref[i % SCORE_SLOTS] = s
        part = s[rows(0)]
        for r in range(1, n_slab):
            part = jnp.maximum(part, s[rows(r)])
        mpart_ref[i % SCORE_SLOTS] = part

    def softmax_pv(j, i):
        slot = i % SCORE_SLOTS
        hd = i // 2
        vt = vt_ref[0, j, hd * V_AUG:(hd + 1) * V_AUG, :]
        m_old = m_ref[i]
        m_new = jnp.maximum(m_old, jnp.max(mpart_ref[slot], axis=0, keepdims=True))
        alpha = jnp.exp2(m_old - m_new)
        m_ref[i] = m_new
        m_slab = jnp.broadcast_to(m_new, (SOFTMAX_SLAB, tq))
        p = jnp.concatenate(
            [jnp.exp2(s_ref[slot, rows(r), :] - m_slab).astype(BF16) for r in range(n_slab)],
            axis=0)
        acc_ref[i] = alpha * acc_ref[i] + jnp.dot(vt, p, preferred_element_type=F32)

    def block(j, j_next, mask):
        for i in range(n_chain):
            ahead = i + SCORE_AHEAD
            if ahead < n_chain:
                scores(j, ahead, mask)
            else:
                scores(j_next, ahead - n_chain, None)
            softmax_pv(j, i)

    key_chunk = lax.broadcasted_iota(jnp.int32, (tk, tq), 0) // CHUNK
    qry_chunk = lax.broadcasted_iota(jnp.int32, (tk, tq), 1) // CHUNK
    diag_mask = key_chunk <= qry_chunk
    for i in range(SCORE_AHEAD):
        scores(qi, i, diag_mask)
    block(qi, 0, diag_mask)

    def body(j, carry):
        block(j, jnp.minimum(j + 1, qi - 1), None)
        return carry

    lax.fori_loop(0, qi, body, 0)

    lam = (jnp.exp(jnp.sum(lq1_ref[...] * lk1_ref[...], axis=-1, keepdims=True))
           - jnp.exp(jnp.sum(lq2_ref[...] * lk2_ref[...], axis=-1, keepdims=True))
           + LAM_INIT)
    for hd in range(DA_HEADS):
        a1, a2 = acc_ref[2 * hd], acc_ref[2 * hd + 1]
        o = (a1[:DA_V_DIM] / a1[DA_V_DIM:DA_V_DIM + 1]
             - lam * (a2[:DA_V_DIM] / a2[DA_V_DIM:DA_V_DIM + 1]))
        o = o * lax.rsqrt(jnp.mean(o * o, axis=0, keepdims=True) + EPS)
        o = o * sg_ref[...] * (1.0 - LAM_INIT)
        o_ref[0, :, hd * DA_V_DIM:(hd + 1) * DA_V_DIM] = o.T.astype(BF16)


def _post_mix_kernel(h_ref, oda_ref, glu_ref, halo_ref, qm_ref, mk_ref, mv_ref,
                     dww_ref, dwb_ref, lng_ref, lnb_ref, wout_ref,
                     g2_ref, wgu_ref, wd_ref, gf_ref,
                     out_ref, cbuf_ref, shift_ref, pmem_ref, act_ref):
    si = pl.program_id(1)
    tm = h_ref.shape[1]

    halo = halo_ref[0]
    cbuf_ref[0:CONV_HALO, :] = jnp.where(si > 0, halo, jnp.zeros_like(halo))
    cbuf_ref[CONV_HALO:CONV_HALO + tm, :] = glu_ref[0]
    span = tm + CONV_HALO - SUBLANES
    for r in range(1, SUBLANES):
        shift_ref[r - 1] = cbuf_ref[r:r + span, :]
    base = CONV_HALO - (CONV_WIDTH - 1)

    def conv_rows(r0):
        y = jnp.zeros((CONV_ROWS, CONV_CH), F32)
        for j in range(CONV_WIDTH):
            phase = (base + j) % SUBLANES
            start = base + j - phase + r0
            if phase == 0:
                window = cbuf_ref[start:start + CONV_ROWS, :]
            else:
                window = shift_ref[phase - 1, start:start + CONV_ROWS, :]
            y = y + dww_ref[j:j + 1, :] * window
        y = y + dwb_ref[...]
        mu = jnp.mean(y, axis=-1, keepdims=True)
        yc = y - mu
        var = jnp.mean(yc * yc, axis=-1, keepdims=True)
        y = yc * lax.rsqrt(var + EPS) * lng_ref[...] + lnb_ref[...]
        return (y * jax.nn.sigmoid(y)).astype(BF16)

    o_conv = jnp.concatenate([conv_rows(r0) for r0 in range(0, tm, CONV_ROWS)], axis=0)

    qm = qm_ref[0]
    n_mem = mk_ref.shape[1]
    head_of_lane = lax.broadcasted_iota(jnp.int32, qm.shape, 1) // MEM_HEAD_DIM
    for hd in range(MEM_HEADS):
        qh = jnp.where(head_of_lane == hd, qm, jnp.zeros_like(qm))
        sc = lax.dot_general(qh, mk_ref[0], (((1,), (1,)), ((), ())),
                             preferred_element_type=F32)
        sc = sc - jnp.max(sc, axis=-1, keepdims=True)
        p = jnp.exp(sc)
        p = p / jnp.sum(p, axis=-1, keepdims=True)
        pmem_ref[:, hd * n_mem:(hd + 1) * n_mem] = p.astype(BF16)
    o_mem = jnp.dot(pmem_ref[...], mv_ref[0], preferred_element_type=F32).astype(BF16)

    mix = (jnp.dot(oda_ref[0], wout_ref[0:DA_WIDTH, :], preferred_element_type=F32)
           + jnp.dot(o_conv, wout_ref[DA_WIDTH:DA_WIDTH + CONV_CH, :],
                     preferred_element_type=F32)
           + jnp.dot(o_mem, wout_ref[DA_WIDTH + CONV_CH:, :], preferred_element_type=F32))
    h = h_ref[0] + mix
    hn = _rms(h, g2_ref[...]).astype(BF16)
    h = h + 0.5 * _swiglu(hn, wgu_ref, wd_ref, act_ref)
    out_ref[0] = _rms(h, gf_ref[...])


def _resident(shape):
    return pl.BlockSpec(shape, lambda *_: (0,) * len(shape), pipeline_mode=pl.Buffered(1))


def _rope_tables(seq):
    half = DA_HEAD_DIM // 2
    inv_freq = 1.0 / (ROPE_THETA ** (jnp.arange(0, DA_HEAD_DIM, 2, dtype=F32) / DA_HEAD_DIM))
    ang = jnp.arange(seq, dtype=F32)[:, None] * inv_freq[None, :]
    cos, sin = jnp.cos(ang), jnp.sin(ang)
    reps = LANES // DA_HEAD_DIM
    cos_t = jnp.tile(jnp.concatenate([cos, cos], axis=1), (1, reps))
    sin_t = jnp.tile(jnp.concatenate([-sin, sin], axis=1), (1, reps))
    assert cos_t.shape == (seq, LANES) and half * 2 == DA_HEAD_DIM
    return cos_t, sin_t


def kernel(x, mem, ffn1_norm_g, ffn1_w_gate, ffn1_w_up, ffn1_w_down, mix_norm_g, mem_norm_g, w_in, lambda_q1, lambda_k1, lambda_q2, lambda_k2, subln_g, conv_dw_w, conv_dw_b, conv_ln_g, conv_ln_b, w_mem_kv, w_out, ffn2_norm_g, ffn2_w_gate, ffn2_w_up, ffn2_w_down, final_norm_g):
    b, s, d = x.shape
    n_mem = mem.shape[1]
    d_ff = ffn1_w_gate.shape[-1]
    in_width = w_in.shape[-1]
    assert ffn1_norm_g.shape[0] == 1, "single layer"
    tm = TOKEN_TILE
    ta = ATTN_TILE
    assert s % tm == 0 and s % ta == 0 and tm == ta and d_ff % FFN_CHUNK == 0
    t = b * s
    n_tiles = s // tm
    params = functools.partial(pltpu.CompilerParams, vmem_limit_bytes=VMEM_LIMIT)

    cos_t, sin_t = _rope_tables(s)
    row = lambda a: a.reshape(1, -1)
    bf = lambda a: a[0].astype(BF16)

    tok = lambda w: pl.BlockSpec((tm, w), lambda i: (i, 0))
    h1, q, k, vt, glu, qm = pl.pallas_call(
        _pre_mix_kernel,
        grid=(t // tm,),
        in_specs=[tok(d), _resident((1, d)), _resident((d, 2 * d_ff)),
                  _resident((d_ff, d)), _resident((1, d)), _resident((d, in_width)),
                  pl.BlockSpec((tm, LANES), lambda i: (i % n_tiles, 0)),
                  pl.BlockSpec((tm, LANES), lambda i: (i % n_tiles, 0))],
        out_specs=[tok(d), tok(DA_WIDTH), tok(DA_WIDTH),
                   pl.BlockSpec((1, DA_HEADS * V_AUG, tm), lambda i: (i, 0, 0)),
                   tok(CONV_CH), tok(MEM_WIDTH)],
        out_shape=[jax.ShapeDtypeStruct((t, d), F32),
                   jax.ShapeDtypeStruct((t, DA_WIDTH), BF16),
                   jax.ShapeDtypeStruct((t, DA_WIDTH), BF16),
                   jax.ShapeDtypeStruct((t // tm, DA_HEADS * V_AUG, tm), BF16),
                   jax.ShapeDtypeStruct((t, CONV_CH), F32),
                   jax.ShapeDtypeStruct((t, MEM_WIDTH), BF16)],
        scratch_shapes=[pltpu.VMEM((tm, d_ff), BF16)],
        compiler_params=params(dimension_semantics=("parallel",)),
        name="pre_mix",
    )(x.reshape(t, d), row(ffn1_norm_g), _interleave_gate_up(ffn1_w_gate[0], ffn1_w_up[0]),
      bf(ffn1_w_down), row(mix_norm_g), bf(w_in), cos_t, sin_t)

    mk, mv = pl.pallas_call(
        _mem_kv_kernel,
        grid=(b,),
        in_specs=[pl.BlockSpec((1, n_mem, d), lambda i: (i, 0, 0)), _resident((1, d)),
                  _resident((d, 2 * MEM_WIDTH))],
        out_specs=[pl.BlockSpec((1, n_mem, MEM_WIDTH), lambda i: (i, 0, 0)),
                   pl.BlockSpec((1, MEM_HEADS * n_mem, MEM_WIDTH), lambda i: (i, 0, 0))],
        out_shape=[jax.ShapeDtypeStruct((b, n_mem, MEM_WIDTH), BF16),
                   jax.ShapeDtypeStruct((b, MEM_HEADS * n_mem, MEM_WIDTH), BF16)],
        compiler_params=params(dimension_semantics=("parallel",)),
        name="mem_kv",
    )(mem, row(mem_norm_g), bf(w_mem_kv))

    lam_spec = _resident((1, DA_HEAD_DIM))
    o_da = pl.pallas_call(
        _diff_attn_kernel,
        grid=(b, s // ta),
        in_specs=[lam_spec, lam_spec, lam_spec, lam_spec, _resident((DA_V_DIM, 1)),
                  pl.BlockSpec((1, ta, DA_WIDTH), lambda bi, qi: (bi, qi, 0)),
                  pl.BlockSpec((1, s, DA_WIDTH), lambda bi, qi: (bi, 0, 0),
                               pipeline_mode=pl.Buffered(1)),
                  pl.BlockSpec((1, s // ta, DA_HEADS * V_AUG, ta), lambda bi, qi: (bi, 0, 0, 0),
                               pipeline_mode=pl.Buffered(1))],
        out_specs=pl.BlockSpec((1, ta, DA_WIDTH), lambda bi, qi: (bi, qi, 0)),
        out_shape=jax.ShapeDtypeStruct((b, s, DA_WIDTH), BF16),
        scratch_shapes=[pltpu.VMEM((2 * DA_HEADS, 1, ta), F32),
                        pltpu.VMEM((2 * DA_HEADS, V_AUG, ta), F32),
                        pltpu.VMEM((SCORE_SLOTS, ta, ta), F32),
                        pltpu.VMEM((SCORE_SLOTS, SOFTMAX_SLAB, ta), F32)],
        compiler_params=params(dimension_semantics=("parallel", "arbitrary")),
        name="diff_attn",
    )(lambda_q1, lambda_k1, lambda_q2, lambda_k2, subln_g.reshape(DA_V_DIM, 1),
      q.reshape(b, s, DA_WIDTH), k.reshape(b, s, DA_WIDTH),
      vt.reshape(b, s // ta, DA_HEADS * V_AUG, ta))

    halo_blocks = tm // CONV_HALO
    seq_tile = lambda w: pl.BlockSpec((1, tm, w), lambda bi, si: (bi, si, 0))
    out = pl.pallas_call(
        _post_mix_kernel,
        grid=(b, n_tiles),
        in_specs=[seq_tile(d), seq_tile(DA_WIDTH), seq_tile(CONV_CH),
                  pl.BlockSpec((1, CONV_HALO, CONV_CH),
                               lambda bi, si: (bi, jnp.maximum(si * halo_blocks - 1, 0), 0)),
                  seq_tile(MEM_WIDTH),
                  pl.BlockSpec((1, n_mem, MEM_WIDTH), lambda bi, si: (bi, 0, 0)),
                  pl.BlockSpec((1, MEM_HEADS * n_mem, MEM_WIDTH), lambda bi, si: (bi, 0, 0)),
                  _resident((CONV_WIDTH, CONV_CH)), _resident((1, CONV_CH)),
                  _resident((1, CONV_CH)), _resident((1, CONV_CH)), _resident((d, d)),
                  _resident((1, d)), _resident((d, 2 * d_ff)),
                  _resident((d_ff, d)), _resident((1, d))],
        out_specs=seq_tile(d),
        out_shape=jax.ShapeDtypeStruct((b, s, d), F32),
        scratch_shapes=[pltpu.VMEM((CONV_HALO + tm, CONV_CH), F32),
                        pltpu.VMEM((SUBLANES - 1, tm + CONV_HALO - SUBLANES, CONV_CH), F32),
                        pltpu.VMEM((tm, MEM_HEADS * n_mem), BF16),
                        pltpu.VMEM((tm, d_ff), BF16)],
        compiler_params=params(dimension_semantics=("parallel", "arbitrary")),
        name="post_mix",
    )(h1.reshape(b, s, d), o_da, glu.reshape(b, s, CONV_CH), glu.reshape(b, s, CONV_CH),
      qm.reshape(b, s, MEM_WIDTH), mk, mv, conv_dw_w[0], conv_dw_b, conv_ln_g, conv_ln_b,
      bf(w_out), row(ffn2_norm_g), _interleave_gate_up(ffn2_w_gate[0], ffn2_w_up[0]),
      bf(ffn2_w_down), row(final_norm_g))
    return out
```

```python
import functools
import math

import jax
import jax.numpy as jnp
from jax import lax
from jax.experimental import pallas as pl
from jax.experimental.pallas import tpu as pltpu

F32 = jnp.float32
BF16 = jnp.bfloat16

EPS = 1e-5
ROPE_THETA = 10000.0
CHUNK = 64
DA_HEADS = 4
DA_HEAD_DIM = 64
DA_V_DIM = 128
V_AUG = DA_V_DIM + 16
DA_WIDTH = 512
CONV_CH = 256
CONV_WIDTH = 31
MEM_HEADS = 4
MEM_HEAD_DIM = 64
MEM_WIDTH = 256
LAM_INIT = 0.8 - 0.6 * math.exp(-0.3 * 0)

V7X_VMEM_BYTES = 64 * 1024 * 1024
VMEM_LIMIT = V7X_VMEM_BYTES - 8 * 1024 * 1024
LANES = 128
SUBLANES = 8
MXU_COLS = 256

TOKEN_TILE = 512
ATTN_TILE = 512
FFN_CHUNK = 256
CONV_HALO = 32
CONV_ROWS = 64
SOFTMAX_SLAB = 16
SCORE_AHEAD = 2
SCORE_SLOTS = 4
NEG_BIG = -1e30
LOG2_E = math.log2(math.e)


def _rms(x, g):
    return x * lax.rsqrt(jnp.mean(x * x, axis=-1, keepdims=True) + EPS) * g


def _swiglu(xn, wg_ref, wu_ref, wd_ref, act_ref):
    d_ff = wg_ref.shape[1]
    for c in range(0, d_ff, FFN_CHUNK):
        gate = jnp.dot(xn, wg_ref[:, c:c + FFN_CHUNK], preferred_element_type=F32)
        up = jnp.dot(xn, wu_ref[:, c:c + FFN_CHUNK], preferred_element_type=F32)
        act_ref[:, c:c + FFN_CHUNK] = (gate * jax.nn.sigmoid(gate) * up).astype(BF16)
    return jnp.dot(act_ref[...], wd_ref[...], preferred_element_type=F32)


def _rope(x, cos, sin_signed, first_half):
    partner = jnp.where(first_half,
                        pltpu.roll(x, LANES - DA_HEAD_DIM // 2, axis=1),
                        pltpu.roll(x, DA_HEAD_DIM // 2, axis=1))
    return x * cos + partner * sin_signed


def _pre_mix_kernel(x_ref, g1_ref, wg_ref, wu_ref, wd_ref, gm_ref, win_ref,
                    cos_ref, sin_ref,
                    h_ref, q_ref, k_ref, vt_ref, glu_ref, qm_ref, act_ref):
    x = x_ref[...]
    xn = _rms(x, g1_ref[...]).astype(BF16)
    h = x + 0.5 * _swiglu(xn, wg_ref, wu_ref, wd_ref, act_ref)
    h_ref[...] = h

    n = _rms(h, gm_ref[...]).astype(BF16)
    cos = cos_ref[...]
    sin_signed = sin_ref[...]
    lane = lax.broadcasted_iota(jnp.int32, cos.shape, 1)
    first_half = (lane % DA_HEAD_DIM) < (DA_HEAD_DIM // 2)
    scale = DA_HEAD_DIM ** -0.5 * LOG2_E
    for c0 in range(0, DA_WIDTH, MXU_COLS):
        qs = jnp.dot(n, win_ref[:, c0:c0 + MXU_COLS], preferred_element_type=F32)
        ks = jnp.dot(n, win_ref[:, DA_WIDTH + c0:DA_WIDTH + c0 + MXU_COLS],
                     preferred_element_type=F32)
        for l0 in range(0, MXU_COLS, LANES):
            q_ref[:, c0 + l0:c0 + l0 + LANES] = (
                _rope(qs[:, l0:l0 + LANES], cos, sin_signed, first_half) * scale).astype(BF16)
            k_ref[:, c0 + l0:c0 + l0 + LANES] = _rope(
                ks[:, l0:l0 + LANES], cos, sin_signed, first_half).astype(BF16)
    v = jnp.dot(n, win_ref[:, 2 * DA_WIDTH:3 * DA_WIDTH], preferred_element_type=F32)
    vt = v.T.astype(BF16)
    ones = jnp.ones((V_AUG - DA_V_DIM, vt.shape[1]), BF16)
    for hd in range(DA_HEADS):
        vt_ref[0, hd * V_AUG:hd * V_AUG + DA_V_DIM, :] = vt[hd * DA_V_DIM:(hd + 1) * DA_V_DIM]
        vt_ref[0, hd * V_AUG + DA_V_DIM:(hd + 1) * V_AUG, :] = ones
    o = 3 * DA_WIDTH
    a = jnp.dot(n, win_ref[:, o:o + CONV_CH], preferred_element_type=F32)
    gate = jnp.dot(n, win_ref[:, o + CONV_CH:o + 2 * CONV_CH], preferred_element_type=F32)
    glu_ref[...] = a * jax.nn.sigmoid(gate)
    o += 2 * CONV_CH
    qm = jnp.dot(n, win_ref[:, o:o + MEM_WIDTH], preferred_element_type=F32)
    qm_ref[...] = (qm * MEM_HEAD_DIM ** -0.5).astype(BF16)


def _mem_kv_kernel(mem_ref, g_ref, w_ref, mk_ref, mv_ref):
    n_mem = mem_ref.shape[1]
    mn = _rms(mem_ref[0], g_ref[...]).astype(BF16)
    kv = jnp.dot(mn, w_ref[...], preferred_element_type=F32)
    mk_ref[0] = kv[:, :MEM_WIDTH].astype(BF16)
    mv = kv[:, MEM_WIDTH:]
    head_of_lane = lax.broadcasted_iota(jnp.int32, mv.shape, 1) // MEM_HEAD_DIM
    for hd in range(MEM_HEADS):
        mv_ref[0, hd * n_mem:(hd + 1) * n_mem, :] = jnp.where(
            head_of_lane == hd, mv, 0.0).astype(BF16)


def _diff_attn_kernel(lq1_ref, lk1_ref, lq2_ref, lk2_ref, sg_ref,
                      q_ref, k_ref, vt_ref, o_ref, m_ref, acc_ref, s_ref, mpart_ref):
    qi = pl.program_id(1)
    tq = q_ref.shape[1]
    tk = tq

    lane = lax.broadcasted_iota(jnp.int32, (tq, LANES), 1)
    q_comp = []
    for hd in range(DA_HEADS):
        q = q_ref[0, :, hd * LANES:(hd + 1) * LANES]
        zero = jnp.zeros_like(q)
        q_comp.append(jnp.where(lane < DA_HEAD_DIM, q, zero))
        q_comp.append(jnp.where(lane >= DA_HEAD_DIM, q, zero))

    m_ref[...] = jnp.full(m_ref.shape, NEG_BIG, F32)
    acc_ref[...] = jnp.zeros(acc_ref.shape, F32)

    n_chain = 2 * DA_HEADS
    n_slab = tk // SOFTMAX_SLAB
    rows = lambda r: slice(r * SOFTMAX_SLAB, (r + 1) * SOFTMAX_SLAB)

    def scores(j, i, mask):
        hd = i // 2
        start = pl.multiple_of(j * tk, tk)
        kb = k_ref[0, pl.ds(start, tk), hd * LANES:(hd + 1) * LANES]
        s = lax.dot_general(kb, q_comp[i], (((1,), (1,)), ((), ())),
                            preferred_element_type=F32)
        if mask is not None:
            s = jnp.where(mask, s, NEG_BIG)
        s_ref[i % SCORE_SLOTS] = s
        part = s[rows(0)]
        for r in range(1, n_slab):
            part = jnp.maximum(part, s[rows(r)])
        mpart_ref[i % SCORE_SLOTS] = part

    def softmax_pv(j, i):
        slot = i % SCORE_SLOTS
        hd = i // 2
        vt = vt_ref[0, j, hd * V_AUG:(hd + 1) * V_AUG, :]
        m_old = m_ref[i]
        m_new = jnp.maximum(m_old, jnp.max(mpart_ref[slot], axis=0, keepdims=True))
        alpha = jnp.exp2(m_old - m_new)
        m_ref[i] = m_new
        m_slab = jnp.broadcast_to(m_new, (SOFTMAX_SLAB, tq))
        p = jnp.concatenate(
            [jnp.exp2(s_ref[slot, rows(r), :] - m_slab).astype(BF16) for r in range(n_slab)],
            axis=0)
        acc_ref[i] = alpha * acc_ref[i] + jnp.dot(vt, p, preferred_element_type=F32)

    def block(j, j_next, mask):
        for i in range(n_chain):
            ahead = i + SCORE_AHEAD
            if ahead < n_chain:
                scores(j, ahead, mask)
            else:
                scores(j_next, ahead - n_chain, None)
            softmax_pv(j, i)

    key_chunk = lax.broadcasted_iota(jnp.int32, (tk, tq), 0) // CHUNK
    qry_chunk = lax.broadcasted_iota(jnp.int32, (tk, tq), 1) // CHUNK
    diag_mask = key_chunk <= qry_chunk
    for i in range(SCORE_AHEAD):
        scores(qi, i, diag_mask)
    block(qi, 0, diag_mask)

    def body(j, carry):
        block(j, jnp.minimum(j + 1, qi - 1), None)
        return carry

    lax.fori_loop(0, qi, body, 0)

    lam = (jnp.exp(jnp.sum(lq1_ref[...] * lk1_ref[...], axis=-1, keepdims=True))
           - jnp.exp(jnp.sum(lq2_ref[...] * lk2_ref[...], axis=-1, keepdims=True))
           + LAM_INIT)
    for hd in range(DA_HEADS):
        a1, a2 = acc_ref[2 * hd], acc_ref[2 * hd + 1]
        o = (a1[:DA_V_DIM] / a1[DA_V_DIM:DA_V_DIM + 1]
             - lam * (a2[:DA_V_DIM] / a2[DA_V_DIM:DA_V_DIM + 1]))
        o = o * lax.rsqrt(jnp.mean(o * o, axis=0, keepdims=True) + EPS)
        o = o * sg_ref[...] * (1.0 - LAM_INIT)
        o_ref[0, :, hd * DA_V_DIM:(hd + 1) * DA_V_DIM] = o.T.astype(BF16)


def _post_mix_kernel(h_ref, oda_ref, glu_ref, halo_ref, qm_ref, mk_ref, mv_ref,
                     dww_ref, dwb_ref, lng_ref, lnb_ref, wout_ref,
                     g2_ref, wg_ref, wu_ref, wd_ref, gf_ref,
                     out_ref, cbuf_ref, shift_ref, pmem_ref, act_ref):
    si = pl.program_id(1)
    tm = h_ref.shape[1]

    halo = halo_ref[0]
    cbuf_ref[0:CONV_HALO, :] = jnp.where(si > 0, halo, jnp.zeros_like(halo))
    cbuf_ref[CONV_HALO:CONV_HALO + tm, :] = glu_ref[0]
    span = tm + CONV_HALO - SUBLANES
    for r in range(1, SUBLANES):
        shift_ref[r - 1] = cbuf_ref[r:r + span, :]
    base = CONV_HALO - (CONV_WIDTH - 1)

    def conv_rows(r0):
        y = jnp.zeros((CONV_ROWS, CONV_CH), F32)
        for j in range(CONV_WIDTH):
            phase = (base + j) % SUBLANES
            start = base + j - phase + r0
            if phase == 0:
                window = cbuf_ref[start:start + CONV_ROWS, :]
            else:
                window = shift_ref[phase - 1, start:start + CONV_ROWS, :]
            y = y + dww_ref[j:j + 1, :] * window
        y = y + dwb_ref[...]
        mu = jnp.mean(y, axis=-1, keepdims=True)
        yc = y - mu
        var = jnp.mean(yc * yc, axis=-1, keepdims=True)
        y = yc * lax.rsqrt(var + EPS) * lng_ref[...] + lnb_ref[...]
        return (y * jax.nn.sigmoid(y)).astype(BF16)

    mix = jnp.dot(oda_ref[0], wout_ref[0:DA_WIDTH, :], preferred_element_type=F32)
    qm = qm_ref[0]
    n_mem = mk_ref.shape[1]
    head_of_lane = lax.broadcasted_iota(jnp.int32, qm.shape, 1) // MEM_HEAD_DIM
    logits = []
    for hd in range(MEM_HEADS):
        qh = jnp.where(head_of_lane == hd, qm, jnp.zeros_like(qm))
        logits.append(lax.dot_general(qh, mk_ref[0], (((1,), (1,)), ((), ())),
                                      preferred_element_type=F32))

    def mem_softmax(hd):
        sc = logits[hd] - jnp.max(logits[hd], axis=-1, keepdims=True)
        p = jnp.exp(sc)
        p = p / jnp.sum(p, axis=-1, keepdims=True)
        pmem_ref[:, hd * n_mem:(hd + 1) * n_mem] = p.astype(BF16)

    pieces = []
    pieces_per_head = (tm // CONV_ROWS) // MEM_HEADS
    for n, r0 in enumerate(range(0, tm, CONV_ROWS)):
        pieces.append(conv_rows(r0))
        if (n + 1) % pieces_per_head == 0:
            mem_softmax(n // pieces_per_head)
    o_conv = jnp.concatenate(pieces, axis=0)
    o_mem = jnp.dot(pmem_ref[...], mv_ref[0], preferred_element_type=F32).astype(BF16)
    mix = (mix
           + jnp.dot(o_conv, wout_ref[DA_WIDTH:DA_WIDTH + CONV_CH, :],
                     preferred_element_type=F32)
           + jnp.dot(o_mem, wout_ref[DA_WIDTH + CONV_CH:, :], preferred_element_type=F32))
    h = h_ref[0] + mix
    hn = _rms(h, g2_ref[...]).astype(BF16)
    h = h + 0.5 * _swiglu(hn, wg_ref, wu_ref, wd_ref, act_ref)
    out_ref[0] = _rms(h, gf_ref[...])


def _resident(shape):
    return pl.BlockSpec(shape, lambda *_: (0,) * len(shape), pipeline_mode=pl.Buffered(1))


def _rope_tables(seq):
    half = DA_HEAD_DIM // 2
    inv_freq = 1.0 / (ROPE_THETA ** (jnp.arange(0, DA_HEAD_DIM, 2, dtype=F32) / DA_HEAD_DIM))
    ang = jnp.arange(seq, dtype=F32)[:, None] * inv_freq[None, :]
    cos, sin = jnp.cos(ang), jnp.sin(ang)
    reps = LANES // DA_HEAD_DIM
    cos_t = jnp.tile(jnp.concatenate([cos, cos], axis=1), (1, reps))
    sin_t = jnp.tile(jnp.concatenate([-sin, sin], axis=1), (1, reps))
    assert cos_t.shape == (seq, LANES) and half * 2 == DA_HEAD_DIM
    return cos_t, sin_t


def kernel(x, mem, ffn1_norm_g, ffn1_w_gate, ffn1_w_up, ffn1_w_down, mix_norm_g, mem_norm_g, w_in, lambda_q1, lambda_k1, lambda_q2, lambda_k2, subln_g, conv_dw_w, conv_dw_b, conv_ln_g, conv_ln_b, w_mem_kv, w_out, ffn2_norm_g, ffn2_w_gate, ffn2_w_up, ffn2_w_down, final_norm_g):
    b, s, d = x.shape
    n_mem = mem.shape[1]
    d_ff = ffn1_w_gate.shape[-1]
    in_width = w_in.shape[-1]
    assert ffn1_norm_g.shape[0] == 1, "single layer"
    tm = TOKEN_TILE
    ta = ATTN_TILE
    assert s % tm == 0 and s % ta == 0 and tm == ta and d_ff % FFN_CHUNK == 0
    t = b * s
    n_tiles = s // tm
    params = functools.partial(pltpu.CompilerParams, vmem_limit_bytes=VMEM_LIMIT)

    cos_t, sin_t = _rope_tables(s)
    row = lambda a: a.reshape(1, -1)
    bf = lambda a: a[0].astype(BF16)

    tok = lambda w: pl.BlockSpec((tm, w), lambda i: (i, 0))
    h1, q, k, vt, glu, qm = pl.pallas_call(
        _pre_mix_kernel,
        grid=(t // tm,),
        in_specs=[tok(d), _resident((1, d)), _resident((d, d_ff)), _resident((d, d_ff)),
                  _resident((d_ff, d)), _resident((1, d)), _resident((d, in_width)),
                  pl.BlockSpec((tm, LANES), lambda i: (i % n_tiles, 0)),
                  pl.BlockSpec((tm, LANES), lambda i: (i % n_tiles, 0))],
        out_specs=[tok(d), tok(DA_WIDTH), tok(DA_WIDTH),
                   pl.BlockSpec((1, DA_HEADS * V_AUG, tm), lambda i: (i, 0, 0)),
                   tok(CONV_CH), tok(MEM_WIDTH)],
        out_shape=[jax.ShapeDtypeStruct((t, d), F32),
                   jax.ShapeDtypeStruct((t, DA_WIDTH), BF16),
                   jax.ShapeDtypeStruct((t, DA_WIDTH), BF16),
                   jax.ShapeDtypeStruct((t // tm, DA_HEADS * V_AUG, tm), BF16),
                   jax.ShapeDtypeStruct((t, CONV_CH), F32),
                   jax.ShapeDtypeStruct((t, MEM_WIDTH), BF16)],
        scratch_shapes=[pltpu.VMEM((tm, d_ff), BF16)],
        compiler_params=params(dimension_semantics=("parallel",)),
        name="pre_mix",
    )(x.reshape(t, d), row(ffn1_norm_g), bf(ffn1_w_gate), bf(ffn1_w_up), bf(ffn1_w_down),
      row(mix_norm_g), bf(w_in), cos_t, sin_t)

    mk, mv = pl.pallas_call(
        _mem_kv_kernel,
        grid=(b,),
        in_specs=[pl.BlockSpec((1, n_mem, d), lambda i: (i, 0, 0)), _resident((1, d)),
                  _resident((d, 2 * MEM_WIDTH))],
        out_specs=[pl.BlockSpec((1, n_mem, MEM_WIDTH), lambda i: (i, 0, 0)),
                   pl.BlockSpec((1, MEM_HEADS * n_mem, MEM_WIDTH), lambda i: (i, 0, 0))],
        out_shape=[jax.ShapeDtypeStruct((b, n_mem, MEM_WIDTH), BF16),
                   jax.ShapeDtypeStruct((b, MEM_HEADS * n_mem, MEM_WIDTH), BF16)],
        compiler_params=params(dimension_semantics=("parallel",)),
        name="mem_kv",
    )(mem, row(mem_norm_g), bf(w_mem_kv))

    lam_spec = _resident((1, DA_HEAD_DIM))
    o_da = pl.pallas_call(
        _diff_attn_kernel,
        grid=(b, s // ta),
        in_specs=[lam_spec, lam_spec, lam_spec, lam_spec, _resident((DA_V_DIM, 1)),
                  pl.BlockSpec((1, ta, DA_WIDTH), lambda bi, qi: (bi, qi, 0)),
                  pl.BlockSpec((1, s, DA_WIDTH), lambda bi, qi: (bi, 0, 0),
                               pipeline_mode=pl.Buffered(1)),
                  pl.BlockSpec((1, s // ta, DA_HEADS * V_AUG, ta), lambda bi, qi: (bi, 0, 0, 0),
                               pipeline_mode=pl.Buffered(1))],
        out_specs=pl.BlockSpec((1, ta, DA_WIDTH), lambda bi, qi: (bi, qi, 0)),
        out_shape=jax.ShapeDtypeStruct((b, s, DA_WIDTH), BF16),
        scratch_shapes=[pltpu.VMEM((2 * DA_HEADS, 1, ta), F32),
                        pltpu.VMEM((2 * DA_HEADS, V_AUG, ta), F32),
                        pltpu.VMEM((SCORE_SLOTS, ta, ta), F32),
                        pltpu.VMEM((SCORE_SLOTS, SOFTMAX_SLAB, ta), F32)],
        compiler_params=params(dimension_semantics=("parallel", "arbitrary")),
        name="diff_attn",
    )(lambda_q1, lambda_k1, lambda_q2, lambda_k2, subln_g.reshape(DA_V_DIM, 1),
      q.reshape(b, s, DA_WIDTH), k.reshape(b, s, DA_WIDTH),
      vt.reshape(b, s // ta, DA_HEADS * V_AUG, ta))

    halo_blocks = tm // CONV_HALO
    seq_tile = lambda w: pl.BlockSpec((1, tm, w), lambda bi, si: (bi, si, 0))
    out = pl.pallas_call(
        _post_mix_kernel,
        grid=(b, n_tiles),
        in_specs=[seq_tile(d), seq_tile(DA_WIDTH), seq_tile(CONV_CH),
                  pl.BlockSpec((1, CONV_HALO, CONV_CH),
                               lambda bi, si: (bi, jnp.maximum(si * halo_blocks - 1, 0), 0)),
                  seq_tile(MEM_WIDTH),
                  pl.BlockSpec((1, n_mem, MEM_WIDTH), lambda bi, si: (bi, 0, 0)),
                  pl.BlockSpec((1, MEM_HEADS * n_mem, MEM_WIDTH), lambda bi, si: (bi, 0, 0)),
                  _resident((CONV_WIDTH, CONV_CH)), _resident((1, CONV_CH)),
                  _resident((1, CONV_CH)), _resident((1, CONV_CH)), _resident((d, d)),
                  _resident((1, d)), _resident((d, d_ff)), _resident((d, d_ff)),
                  _resident((d_ff, d)), _resident((1, d))],
        out_specs=seq_tile(d),
        out_shape=jax.ShapeDtypeStruct((b, s, d), F32),
        scratch_shapes=[pltpu.VMEM((CONV_HALO + tm, CONV_CH), F32),
                        pltpu.VMEM((SUBLANES - 1, tm + CONV_HALO - SUBLANES, CONV_CH), F32),
                        pltpu.VMEM((tm, MEM_HEADS * n_mem), BF16),
                        pltpu.VMEM((tm, d_ff), BF16)],
        compiler_params=params(dimension_semantics=("parallel", "arbitrary")),
        name="post_mix",
    )(h1.reshape(b, s, d), o_da, glu.reshape(b, s, CONV_CH), glu.reshape(b, s, CONV_CH),
      qm.reshape(b, s, MEM_WIDTH), mk, mv, conv_dw_w[0], conv_dw_b, conv_ln_g, conv_ln_b,
      bf(w_out), row(ffn2_norm_g), bf(ffn2_w_gate), bf(ffn2_w_up), bf(ffn2_w_down),
      row(final_norm_g))
    return out
```

```python
import functools
import math

import jax
import jax.numpy as jnp
from jax import lax
from jax.experimental import pallas as pl
from jax.experimental.pallas import tpu as pltpu

F32 = jnp.float32
BF16 = jnp.bfloat16

EPS = 1e-5
ROPE_THETA = 10000.0
CHUNK = 64
DA_HEADS = 4
DA_HEAD_DIM = 64
DA_V_DIM = 128
V_AUG = DA_V_DIM + 16
DA_WIDTH = 512
CONV_CH = 256
CONV_WIDTH = 31
MEM_HEADS = 4
MEM_HEAD_DIM = 64
MEM_WIDTH = 256
LAM_INIT = 0.8 - 0.6 * math.exp(-0.3 * 0)

V7X_VMEM_BYTES = 64 * 1024 * 1024
VMEM_LIMIT = V7X_VMEM_BYTES - 8 * 1024 * 1024
LANES = 128
SUBLANES = 8
MXU_COLS = 256

TOKEN_TILE = 512
ATTN_TILE = 512
FFN_CHUNK = 256
CONV_HALO = 32
CONV_ROWS = 64
SOFTMAX_SLAB = 16
SCORE_AHEAD = 2
SCORE_SLOTS = 4
NEG_BIG = -1e30
LOG2_E = math.log2(math.e)


def _rms(x, g):
    return x * lax.rsqrt(jnp.mean(x * x, axis=-1, keepdims=True) + EPS) * g


def _swiglu(xn, wg_ref, wu_ref, wd_ref, act_ref):
    d_ff = wg_ref.shape[1]
    for c in range(0, d_ff, FFN_CHUNK):
        gate = jnp.dot(xn, wg_ref[:, c:c + FFN_CHUNK], preferred_element_type=F32)
        up = jnp.dot(xn, wu_ref[:, c:c + FFN_CHUNK], preferred_element_type=F32)
        act_ref[:, c:c + FFN_CHUNK] = (gate * jax.nn.sigmoid(gate) * up).astype(BF16)
    return jnp.dot(act_ref[...], wd_ref[...], preferred_element_type=F32)


def _rope(x, cos, sin_signed, first_half):
    partner = jnp.where(first_half,
                        pltpu.roll(x, LANES - DA_HEAD_DIM // 2, axis=1),
                        pltpu.roll(x, DA_HEAD_DIM // 2, axis=1))
    return x * cos + partner * sin_signed


def _pre_mix_kernel(x_ref, g1_ref, wg_ref, wu_ref, wd_ref, gm_ref, win_ref,
                    cos_ref, sin_ref,
                    h_ref, q_ref, k_ref, vt_ref, glu_ref, qm_ref, act_ref):
    x = x_ref[...]
    xn = _rms(x, g1_ref[...]).astype(BF16)
    h = x + 0.5 * _swiglu(xn, wg_ref, wu_ref, wd_ref, act_ref)
    h_ref[...] = h

    n = _rms(h, gm_ref[...]).astype(BF16)
    cos = cos_ref[...]
    sin_signed = sin_ref[...]
    lane = lax.broadcasted_iota(jnp.int32, cos.shape, 1)
    first_half = (lane % DA_HEAD_DIM) < (DA_HEAD_DIM // 2)
    scale = DA_HEAD_DIM ** -0.5 * LOG2_E
    for c0 in range(0, DA_WIDTH, MXU_COLS):
        qs = jnp.dot(n, win_ref[:, c0:c0 + MXU_COLS], preferred_element_type=F32)
        ks = jnp.dot(n, win_ref[:, DA_WIDTH + c0:DA_WIDTH + c0 + MXU_COLS],
                     preferred_element_type=F32)
        for l0 in range(0, MXU_COLS, LANES):
            q_ref[:, c0 + l0:c0 + l0 + LANES] = (
                _rope(qs[:, l0:l0 + LANES], cos, sin_signed, first_half) * scale).astype(BF16)
            k_ref[:, c0 + l0:c0 + l0 + LANES] = _rope(
                ks[:, l0:l0 + LANES], cos, sin_signed, first_half).astype(BF16)
    v = jnp.dot(n, win_ref[:, 2 * DA_WIDTH:3 * DA_WIDTH], preferred_element_type=F32)
    vt = v.T.astype(BF16)
    ones = jnp.ones((V_AUG - DA_V_DIM, vt.shape[1]), BF16)
    for hd in range(DA_HEADS):
        vt_ref[0, hd * V_AUG:hd * V_AUG + DA_V_DIM, :] = vt[hd * DA_V_DIM:(hd + 1) * DA_V_DIM]
        vt_ref[0, hd * V_AUG + DA_V_DIM:(hd + 1) * V_AUG, :] = ones
    o = 3 * DA_WIDTH
    a = jnp.dot(n, win_ref[:, o:o + CONV_CH], preferred_element_type=F32)
    gate = jnp.dot(n, win_ref[:, o + CONV_CH:o + 2 * CONV_CH], preferred_element_type=F32)
    glu_ref[...] = a * jax.nn.sigmoid(gate)
    o += 2 * CONV_CH
    qm = jnp.dot(n, win_ref[:, o:o + MEM_WIDTH], preferred_element_type=F32)
    qm_ref[...] = (qm * MEM_HEAD_DIM ** -0.5).astype(BF16)


def _mem_kv_kernel(mem_ref, g_ref, w_ref, mk_ref, mv_ref):
    n_mem = mem_ref.shape[1]
    mn = _rms(mem_ref[0], g_ref[...]).astype(BF16)
    kv = jnp.dot(mn, w_ref[...], preferred_element_type=F32)
    mk_ref[0] = kv[:, :MEM_WIDTH].astype(BF16)
    mv = kv[:, MEM_WIDTH:]
    head_of_lane = lax.broadcasted_iota(jnp.int32, mv.shape, 1) // MEM_HEAD_DIM
    for hd in range(MEM_HEADS):
        mv_ref[0, hd * n_mem:(hd + 1) * n_mem, :] = jnp.where(
            head_of_lane == hd, mv, 0.0).astype(BF16)


def _diff_attn_kernel(lq1_ref, lk1_ref, lq2_ref, lk2_ref, sg_ref,
                      q_ref, k_ref, vt_ref, o_ref, m_ref, acc_ref, s_ref, mpart_ref):
    qi = pl.program_id(1)
    tq = q_ref.shape[1]
    tk = tq

    lane = lax.broadcasted_iota(jnp.int32, (tq, LANES), 1)
    q_comp = []
    for hd in range(DA_HEADS):
        q = q_ref[0, :, hd * LANES:(hd + 1) * LANES]
        zero = jnp.zeros_like(q)
        q_comp.append(jnp.where(lane < DA_HEAD_DIM, q, zero))
        q_comp.append(jnp.where(lane >= DA_HEAD_DIM, q, zero))

    m_ref[...] = jnp.full(m_ref.shape, NEG_BIG, F32)
    acc_ref[...] = jnp.zeros(acc_ref.shape, F32)

    n_chain = 2 * DA_HEADS
    n_slab = tk // SOFTMAX_SLAB
    rows = lambda r: slice(r * SOFTMAX_SLAB, (r + 1) * SOFTMAX_SLAB)

    def scores(j, i, mask):
        hd = i // 2
        start = pl.multiple_of(j * tk, tk)
        kb = k_ref[0, pl.ds(start, tk), hd * LANES:(hd + 1) * LANES]
        s = lax.dot_general(kb, q_comp[i], (((1,), (1,)), ((), ())),
                            preferred_element_type=F32)
        if mask is not None:
            s = jnp.where(mask, s, NEG_BIG)
        s_ref[i % SCORE_SLOTS] = s
        part = s[rows(0)]
        for r in range(1, n_slab):
            part = jnp.maximum(part, s[rows(r)])
        mpart_ref[i % SCORE_SLOTS] = part

    def softmax_pv(j, i):
        slot = i % SCORE_SLOTS
        hd = i // 2
        vt = vt_ref[0, j, hd * V_AUG:(hd + 1) * V_AUG, :]
        m_old = m_ref[i]
        m_new = jnp.maximum(m_old, jnp.max(mpart_ref[slot], axis=0, keepdims=True))
        alpha = jnp.exp2(m_old - m_new)
        m_ref[i] = m_new
        m_slab = jnp.broadcast_to(m_new, (SOFTMAX_SLAB, tq))
        p = jnp.concatenate(
            [jnp.exp2(s_ref[slot, rows(r), :] - m_slab).astype(BF16) for r in range(n_slab)],
            axis=0)
        acc_ref[i] = alpha * acc_ref[i] + jnp.dot(vt, p, preferred_element_type=F32)

    def block(j, j_next, mask):
        for i in range(n_chain):
            ahead = i + SCORE_AHEAD
            if ahead < n_chain:
                scores(j, ahead, mask)
            else:
                scores(j_next, ahead - n_chain, None)
            softmax_pv(j, i)

    key_chunk = lax.broadcasted_iota(jnp.int32, (tk, tq), 0) // CHUNK
    qry_chunk = lax.broadcasted_iota(jnp.int32, (tk, tq), 1) // CHUNK
    diag_mask = key_chunk <= qry_chunk
    for i in range(SCORE_AHEAD):
        scores(qi, i, diag_mask)
    block(qi, 0, diag_mask)

    def body(j, carry):
        block(j, jnp.minimum(j + 1, qi - 1), None)
        return carry

    lax.fori_loop(0, qi, body, 0)

    lam = (jnp.exp(jnp.sum(lq1_ref[...] * lk1_ref[...], axis=-1, keepdims=True))
           - jnp.exp(jnp.sum(lq2_ref[...] * lk2_ref[...], axis=-1, keepdims=True))
           + LAM_INIT)
    for hd in range(DA_HEADS):
        a1, a2 = acc_ref[2 * hd], acc_ref[2 * hd + 1]
        o = (a1[:DA_V_DIM] / a1[DA_V_DIM:DA_V_DIM + 1]
             - lam * (a2[:DA_V_DIM] / a2[DA_V_DIM:DA_V_DIM + 1]))
        o = o * lax.rsqrt(jnp.mean(o * o, axis=0, keepdims=True) + EPS)
        o = o * sg_ref[...] * (1.0 - LAM_INIT)
        o_ref[0, :, hd * DA_V_DIM:(hd + 1) * DA_V_DIM] = o.T.astype(BF16)


def _post_mix_kernel(h_ref, oda_ref, glu_ref, halo_ref, qm_ref, mk_ref, mv_ref,
                     dww_ref, dwb_ref, lng_ref, lnb_ref, wout_ref,
                     g2_ref, wg_ref, wu_ref, wd_ref, gf_ref,
                     out_ref, cbuf_ref, shift_ref, yconv_ref, pmem_ref, h2_ref, xn_ref, act_ref,
                     *, tiles_per_seq):
    step = pl.program_id(0)
    tm = h_ref.shape[0]
    n_mem = mk_ref.shape[1]
    d_ff = wg_ref.shape[1]
    tile = jnp.minimum(step, pl.num_programs(0) - 2)
    seq_start = (tile % tiles_per_seq) == 0
    base = CONV_HALO - (CONV_WIDTH - 1)

    @pl.when(step == 0)
    def _():
        h2_ref[1] = jnp.zeros(h2_ref.shape[1:], F32)

    def after(c):
        tile_ = act_ref[0:2 * SUBLANES, c * FFN_CHUNK:c * FFN_CHUNK + LANES]
        return jnp.sum(tile_.astype(F32) * 0.0, keepdims=True)

    def before_next_slab(done_bf16):
        sl = (slice(0, 2 * SUBLANES), slice(0, LANES))
        xn_ref[sl] = xn_ref[sl] + done_bf16 * jnp.zeros_like(done_bf16)

    def conv_setup(c):
        halo = halo_ref[...]
        cbuf_ref[0:CONV_HALO, :] = jnp.where(seq_start, jnp.zeros_like(halo), halo)
        cbuf_ref[CONV_HALO:CONV_HALO + tm, :] = glu_ref[...]
        span = tm + CONV_HALO - SUBLANES
        for r in range(1, SUBLANES):
            shift_ref[r - 1] = cbuf_ref[r:r + span, :]
        before_next_slab(shift_ref[SUBLANES - 2, 0:2 * SUBLANES, 0:LANES].astype(BF16))

    def conv_rows(r0, c):
        y = jnp.zeros((CONV_ROWS, CONV_CH), F32) + after(c - 1)
        for j in range(CONV_WIDTH):
            phase = (base + j) % SUBLANES
            start = base + j - phase + r0
            if phase == 0:
                window = cbuf_ref[start:start + CONV_ROWS, :]
            else:
                window = shift_ref[phase - 1, start:start + CONV_ROWS, :]
            y = y + dww_ref[j:j + 1, :] * window
        y = y + dwb_ref[...]
        mu = jnp.mean(y, axis=-1, keepdims=True)
        yc = y - mu
        var = jnp.mean(yc * yc, axis=-1, keepdims=True)
        y = yc * lax.rsqrt(var + EPS) * lng_ref[...] + lnb_ref[...]
        piece = (y * jax.nn.sigmoid(y)).astype(BF16)
        yconv_ref[r0:r0 + CONV_ROWS, :] = piece
        before_next_slab(piece[0:2 * SUBLANES, 0:LANES])

    def mem_heads(heads, c):
        qm = qm_ref[...]
        head_of_lane = lax.broadcasted_iota(jnp.int32, qm.shape, 1) // MEM_HEAD_DIM
        wait = after(c - 1)
        for hd in heads:
            qh = jnp.where(head_of_lane == hd, qm, jnp.zeros_like(qm))
            sc = lax.dot_general(qh, mk_ref[0], (((1,), (1,)), ((), ())),
                                 preferred_element_type=F32)
            sc = sc - (jnp.max(sc, axis=-1, keepdims=True) + wait)
            p = jnp.exp(sc)
            p = (p / jnp.sum(p, axis=-1, keepdims=True)).astype(BF16)
            pmem_ref[:, hd * n_mem:(hd + 1) * n_mem] = p
        before_next_slab(p[0:2 * SUBLANES, 0:LANES])

    pieces = [conv_setup]
    pieces += [functools.partial(conv_rows, r0) for r0 in range(0, tm, CONV_ROWS)]
    pieces += [functools.partial(mem_heads, (0, 1)), functools.partial(mem_heads, (2, 3))]
    assert len(pieces) == d_ff // FFN_CHUNK, "one mixing piece per SwiGLU slab"

    h_prev = h2_ref[(step + 1) % 2]
    xn_ref[...] = _rms(h_prev, g2_ref[...]).astype(BF16)
    for n, c0 in enumerate(range(0, d_ff, FFN_CHUNK)):
        xn = xn_ref[...]
        gate = jnp.dot(xn, wg_ref[:, c0:c0 + FFN_CHUNK], preferred_element_type=F32)
        up = jnp.dot(xn, wu_ref[:, c0:c0 + FFN_CHUNK], preferred_element_type=F32)
        pieces[n](n)
        act_ref[:, c0:c0 + FFN_CHUNK] = (gate * jax.nn.sigmoid(gate) * up).astype(BF16)
    ffn = jnp.dot(act_ref[...], wd_ref[...], preferred_element_type=F32)
    out_ref[...] = _rms(h_prev + 0.5 * ffn, gf_ref[...])

    o_mem = jnp.dot(pmem_ref[...], mv_ref[0], preferred_element_type=F32).astype(BF16)
    mix = (jnp.dot(oda_ref[...], wout_ref[0:DA_WIDTH, :], preferred_element_type=F32)
           + jnp.dot(yconv_ref[...], wout_ref[DA_WIDTH:DA_WIDTH + CONV_CH, :],
                     preferred_element_type=F32)
           + jnp.dot(o_mem, wout_ref[DA_WIDTH + CONV_CH:, :], preferred_element_type=F32))
    h2_ref[step % 2] = h_ref[...] + mix


def _resident(shape):
    return pl.BlockSpec(shape, lambda *_: (0,) * len(shape), pipeline_mode=pl.Buffered(1))


def _rope_tables(seq):
    half = DA_HEAD_DIM // 2
    inv_freq = 1.0 / (ROPE_THETA ** (jnp.arange(0, DA_HEAD_DIM, 2, dtype=F32) / DA_HEAD_DIM))
    ang = jnp.arange(seq, dtype=F32)[:, None] * inv_freq[None, :]
    cos, sin = jnp.cos(ang), jnp.sin(ang)
    reps = LANES // DA_HEAD_DIM
    cos_t = jnp.tile(jnp.concatenate([cos, cos], axis=1), (1, reps))
    sin_t = jnp.tile(jnp.concatenate([-sin, sin], axis=1), (1, reps))
    assert cos_t.shape == (seq, LANES) and half * 2 == DA_HEAD_DIM
    return cos_t, sin_t


def kernel(x, mem, ffn1_norm_g, ffn1_w_gate, ffn1_w_up, ffn1_w_down, mix_norm_g, mem_norm_g, w_in, lambda_q1, lambda_k1, lambda_q2, lambda_k2, subln_g, conv_dw_w, conv_dw_b, conv_ln_g, conv_ln_b, w_mem_kv, w_out, ffn2_norm_g, ffn2_w_gate, ffn2_w_up, ffn2_w_down, final_norm_g):
    b, s, d = x.shape
    n_mem = mem.shape[1]
    d_ff = ffn1_w_gate.shape[-1]
    in_width = w_in.shape[-1]
    assert ffn1_norm_g.shape[0] == 1, "single layer"
    tm = TOKEN_TILE
    ta = ATTN_TILE
    assert s % tm == 0 and s % ta == 0 and tm == ta and d_ff % FFN_CHUNK == 0
    t = b * s
    n_tiles = s // tm
    params = functools.partial(pltpu.CompilerParams, vmem_limit_bytes=VMEM_LIMIT)

    cos_t, sin_t = _rope_tables(s)
    row = lambda a: a.reshape(1, -1)
    bf = lambda a: a[0].astype(BF16)

    tok = lambda w: pl.BlockSpec((tm, w), lambda i: (i, 0))
    h1, q, k, vt, glu, qm = pl.pallas_call(
        _pre_mix_kernel,
        grid=(t // tm,),
        in_specs=[tok(d), _resident((1, d)), _resident((d, d_ff)), _resident((d, d_ff)),
                  _resident((d_ff, d)), _resident((1, d)), _resident((d, in_width)),
                  pl.BlockSpec((tm, LANES), lambda i: (i % n_tiles, 0)),
                  pl.BlockSpec((tm, LANES), lambda i: (i % n_tiles, 0))],
        out_specs=[tok(d), tok(DA_WIDTH), tok(DA_WIDTH),
                   pl.BlockSpec((1, DA_HEADS * V_AUG, tm), lambda i: (i, 0, 0)),
                   tok(CONV_CH), tok(MEM_WIDTH)],
        out_shape=[jax.ShapeDtypeStruct((t, d), F32),
                   jax.ShapeDtypeStruct((t, DA_WIDTH), BF16),
                   jax.ShapeDtypeStruct((t, DA_WIDTH), BF16),
                   jax.ShapeDtypeStruct((t // tm, DA_HEADS * V_AUG, tm), BF16),
                   jax.ShapeDtypeStruct((t, CONV_CH), F32),
                   jax.ShapeDtypeStruct((t, MEM_WIDTH), BF16)],
        scratch_shapes=[pltpu.VMEM((tm, d_ff), BF16)],
        compiler_params=params(dimension_semantics=("parallel",)),
        name="pre_mix",
    )(x.reshape(t, d), row(ffn1_norm_g), bf(ffn1_w_gate), bf(ffn1_w_up), bf(ffn1_w_down),
      row(mix_norm_g), bf(w_in), cos_t, sin_t)

    mk, mv = pl.pallas_call(
        _mem_kv_kernel,
        grid=(b,),
        in_specs=[pl.BlockSpec((1, n_mem, d), lambda i: (i, 0, 0)), _resident((1, d)),
                  _resident((d, 2 * MEM_WIDTH))],
        out_specs=[pl.BlockSpec((1, n_mem, MEM_WIDTH), lambda i: (i, 0, 0)),
                   pl.BlockSpec((1, MEM_HEADS * n_mem, MEM_WIDTH), lambda i: (i, 0, 0))],
        out_shape=[jax.ShapeDtypeStruct((b, n_mem, MEM_WIDTH), BF16),
                   jax.ShapeDtypeStruct((b, MEM_HEADS * n_mem, MEM_WIDTH), BF16)],
        compiler_params=params(dimension_semantics=("parallel",)),
        name="mem_kv",
    )(mem, row(mem_norm_g), bf(w_mem_kv))

    lam_spec = _resident((1, DA_HEAD_DIM))
    o_da = pl.pallas_call(
        _diff_attn_kernel,
        grid=(b, s // ta),
        in_specs=[lam_spec, lam_spec, lam_spec, lam_spec, _resident((DA_V_DIM, 1)),
                  pl.BlockSpec((1, ta, DA_WIDTH), lambda bi, qi: (bi, qi, 0)),
                  pl.BlockSpec((1, s, DA_WIDTH), lambda bi, qi: (bi, 0, 0),
                               pipeline_mode=pl.Buffered(1)),
                  pl.BlockSpec((1, s // ta, DA_HEADS * V_AUG, ta), lambda bi, qi: (bi, 0, 0, 0),
                               pipeline_mode=pl.Buffered(1))],
        out_specs=pl.BlockSpec((1, ta, DA_WIDTH), lambda bi, qi: (bi, qi, 0)),
        out_shape=jax.ShapeDtypeStruct((b, s, DA_WIDTH), BF16),
        scratch_shapes=[pltpu.VMEM((2 * DA_HEADS, 1, ta), F32),
                        pltpu.VMEM((2 * DA_HEADS, V_AUG, ta), F32),
                        pltpu.VMEM((SCORE_SLOTS, ta, ta), F32),
                        pltpu.VMEM((SCORE_SLOTS, SOFTMAX_SLAB, ta), F32)],
        compiler_params=params(dimension_semantics=("parallel", "arbitrary")),
        name="diff_attn",
    )(lambda_q1, lambda_k1, lambda_q2, lambda_k2, subln_g.reshape(DA_V_DIM, 1),
      q.reshape(b, s, DA_WIDTH), k.reshape(b, s, DA_WIDTH),
      vt.reshape(b, s // ta, DA_HEADS * V_AUG, ta))

    halo_blocks = tm // CONV_HALO
    n_steps = t // tm + 1
    last = t // tm - 1
    mixed = lambda w: pl.BlockSpec((tm, w), lambda i: (jnp.minimum(i, last), 0))
    per_seq = lambda rows: pl.BlockSpec((1, rows, MEM_WIDTH),
                                        lambda i: (jnp.minimum(i, last) // n_tiles, 0, 0))
    out = pl.pallas_call(
        functools.partial(_post_mix_kernel, tiles_per_seq=n_tiles),
        grid=(n_steps,),
        in_specs=[mixed(d), mixed(DA_WIDTH), mixed(CONV_CH),
                  pl.BlockSpec((CONV_HALO, CONV_CH),
                               lambda i: (jnp.maximum(jnp.minimum(i, last) * halo_blocks - 1, 0), 0)),
                  mixed(MEM_WIDTH), per_seq(n_mem), per_seq(MEM_HEADS * n_mem),
                  _resident((CONV_WIDTH, CONV_CH)), _resident((1, CONV_CH)),
                  _resident((1, CONV_CH)), _resident((1, CONV_CH)), _resident((d, d)),
                  _resident((1, d)), _resident((d, d_ff)), _resident((d, d_ff)),
                  _resident((d_ff, d)), _resident((1, d))],
        out_specs=pl.BlockSpec((tm, d), lambda i: (jnp.maximum(i - 1, 0), 0)),
        out_shape=jax.ShapeDtypeStruct((t, d), F32),
        scratch_shapes=[pltpu.VMEM((CONV_HALO + tm, CONV_CH), F32),
                        pltpu.VMEM((SUBLANES - 1, tm + CONV_HALO - SUBLANES, CONV_CH), F32),
                        pltpu.VMEM((tm, CONV_CH), BF16),
                        pltpu.VMEM((tm, MEM_HEADS * n_mem), BF16),
                        pltpu.VMEM((2, tm, d), F32),
                        pltpu.VMEM((tm, d), BF16),
                        pltpu.VMEM((tm, d_ff), BF16)],
        compiler_params=params(dimension_semantics=("arbitrary",)),
        name="post_mix",
    )(h1, o_da.reshape(t, DA_WIDTH), glu, glu, qm, mk, mv, conv_dw_w[0], conv_dw_b,
      conv_ln_g, conv_ln_b, bf(w_out), row(ffn2_norm_g), bf(ffn2_w_gate), bf(ffn2_w_up),
      bf(ffn2_w_down), row(final_norm_g))
    return out.reshape(b, s, d)
```

```python
import functools
import math

import jax
import jax.numpy as jnp
from jax import lax
from jax.experimental import pallas as pl
from jax.experimental.pallas import tpu as pltpu

F32 = jnp.float32
BF16 = jnp.bfloat16

EPS = 1e-5
ROPE_THETA = 10000.0
CHUNK = 64
DA_HEADS = 4
DA_HEAD_DIM = 64
DA_V_DIM = 128
V_AUG = DA_V_DIM + 16
DA_WIDTH = 512
CONV_CH = 256
CONV_WIDTH = 31
MEM_HEADS = 4
MEM_HEAD_DIM = 64
MEM_WIDTH = 256
LAM_INIT = 0.8 - 0.6 * math.exp(-0.3 * 0)

V7X_VMEM_BYTES = 64 * 1024 * 1024
VMEM_LIMIT = V7X_VMEM_BYTES - 8 * 1024 * 1024
LANES = 128
SUBLANES = 8
MXU_COLS = 256

TOKEN_TILE = 512
ATTN_TILE = 512
FFN_CHUNK = 256
CONV_HALO = 32
CONV_ROWS = 64
SOFTMAX_SLAB = 16
SCORE_AHEAD = 2
SCORE_SLOTS = 4
NEG_BIG = -1e30
LOG2_E = math.log2(math.e)


def _rms(x, g):
    return x * lax.rsqrt(jnp.mean(x * x, axis=-1, keepdims=True) + EPS) * g


def _swiglu(xn, wg_ref, wu_ref, wd_ref, act_ref):
    for n in range(wg_ref.shape[0]):
        c = n * FFN_CHUNK
        gate = jnp.dot(xn, wg_ref[n], preferred_element_type=F32)
        up = jnp.dot(xn, wu_ref[n], preferred_element_type=F32)
        act_ref[:, c:c + FFN_CHUNK] = (gate * jax.nn.sigmoid(gate) * up).astype(BF16)
    return jnp.dot(act_ref[...], wd_ref[...], preferred_element_type=F32)


def _stage_weight_slabs(step, wg32_ref, wu32_ref, wd32_ref, wg_ref, wu_ref, wd_ref):
    wg_ref[step] = wg32_ref[...].astype(BF16)
    wu_ref[step] = wu32_ref[...].astype(BF16)
    rows = pl.ds(pl.multiple_of(step * FFN_CHUNK, FFN_CHUNK), FFN_CHUNK)
    wd_ref[rows, :] = wd32_ref[...].astype(BF16)


def _rope(x, cos, sin_signed, first_half):
    partner = jnp.where(first_half,
                        pltpu.roll(x, LANES - DA_HEAD_DIM // 2, axis=1),
                        pltpu.roll(x, DA_HEAD_DIM // 2, axis=1))
    return x * cos + partner * sin_signed


def _pre_mix_kernel(x_ref, g1_ref, wg32_ref, wu32_ref, wd32_ref, gm_ref, win32_ref,
                    cos_ref, sin_ref,
                    h_ref, q_ref, k_ref, vt_ref, glu_ref, qm_ref,
                    wg_ref, wu_ref, wd_ref, win_ref, act_ref):
    step = pl.program_id(0)
    n_stage = wg_ref.shape[0]

    @pl.when(step < n_stage)
    def _():
        _stage_weight_slabs(step, wg32_ref, wu32_ref, wd32_ref, wg_ref, wu_ref, wd_ref)
        win_ref[jnp.minimum(step, win_ref.shape[0] - 1)] = win32_ref[...].astype(BF16)

    @pl.when(step >= n_stage)
    def _():
        _pre_mix_tile(x_ref, g1_ref, wg_ref, wu_ref, wd_ref, gm_ref, win_ref, cos_ref, sin_ref,
                      h_ref, q_ref, k_ref, vt_ref, glu_ref, qm_ref, act_ref)


def _pre_mix_tile(x_ref, g1_ref, wg_ref, wu_ref, wd_ref, gm_ref, win_ref, cos_ref, sin_ref,
                  h_ref, q_ref, k_ref, vt_ref, glu_ref, qm_ref, act_ref):
    x = x_ref[...]
    xn = _rms(x, g1_ref[...]).astype(BF16)
    h = x + 0.5 * _swiglu(xn, wg_ref, wu_ref, wd_ref, act_ref)
    h_ref[...] = h

    n = _rms(h, gm_ref[...]).astype(BF16)
    proj = lambda slab: jnp.dot(n, win_ref[slab], preferred_element_type=F32)
    cos = cos_ref[...]
    sin_signed = sin_ref[...]
    lane = lax.broadcasted_iota(jnp.int32, cos.shape, 1)
    first_half = (lane % DA_HEAD_DIM) < (DA_HEAD_DIM // 2)
    scale = DA_HEAD_DIM ** -0.5 * LOG2_E
    q_slabs = DA_WIDTH // MXU_COLS
    for sl in range(q_slabs):
        c0 = sl * MXU_COLS
        qs = proj(sl)
        ks = proj(q_slabs + sl)
        for l0 in range(0, MXU_COLS, LANES):
            q_ref[:, c0 + l0:c0 + l0 + LANES] = (
                _rope(qs[:, l0:l0 + LANES], cos, sin_signed, first_half) * scale).astype(BF16)
            k_ref[:, c0 + l0:c0 + l0 + LANES] = _rope(
                ks[:, l0:l0 + LANES], cos, sin_signed, first_half).astype(BF16)
    vt = jnp.concatenate([proj(2 * q_slabs + sl).T for sl in range(q_slabs)],
                         axis=0).astype(BF16)
    ones = jnp.ones((V_AUG - DA_V_DIM, vt.shape[1]), BF16)
    for hd in range(DA_HEADS):
        vt_ref[0, hd * V_AUG:hd * V_AUG + DA_V_DIM, :] = vt[hd * DA_V_DIM:(hd + 1) * DA_V_DIM]
        vt_ref[0, hd * V_AUG + DA_V_DIM:(hd + 1) * V_AUG, :] = ones
    sl = 3 * q_slabs
    glu_ref[...] = proj(sl) * jax.nn.sigmoid(proj(sl + 1))
    qm_ref[...] = (proj(sl + 2) * MEM_HEAD_DIM ** -0.5).astype(BF16)


def _mem_kv_kernel(mem_ref, g_ref, w_ref, mk_ref, mv_ref):
    n_mem = mem_ref.shape[1]
    mn = _rms(mem_ref[0], g_ref[...]).astype(BF16)
    kv = jnp.dot(mn, w_ref[...], preferred_element_type=F32)
    mk_ref[0] = kv[:, :MEM_WIDTH].astype(BF16)
    mv = kv[:, MEM_WIDTH:]
    head_of_lane = lax.broadcasted_iota(jnp.int32, mv.shape, 1) // MEM_HEAD_DIM
    for hd in range(MEM_HEADS):
        mv_ref[0, hd * n_mem:(hd + 1) * n_mem, :] = jnp.where(
            head_of_lane == hd, mv, 0.0).astype(BF16)


def _diff_attn_kernel(lq1_ref, lk1_ref, lq2_ref, lk2_ref, sg_ref,
                      q_ref, k_ref, vt_ref, o_ref, m_ref, acc_ref, s_ref, mpart_ref):
    qi = pl.program_id(1)
    tq = q_ref.shape[1]
    tk = tq

    lane = lax.broadcasted_iota(jnp.int32, (tq, LANES), 1)
    q_comp = []
    for hd in range(DA_HEADS):
        q = q_ref[0, :, hd * LANES:(hd + 1) * LANES]
        zero = jnp.zeros_like(q)
        q_comp.append(jnp.where(lane < DA_HEAD_DIM, q, zero))
        q_comp.append(jnp.where(lane >= DA_HEAD_DIM, q, zero))

    m_ref[...] = jnp.full(m_ref.shape, NEG_BIG, F32)
    acc_ref[...] = jnp.zeros(acc_ref.shape, F32)

    n_chain = 2 * DA_HEADS
    n_slab = tk // SOFTMAX_SLAB
    rows = lambda r: slice(r * SOFTMAX_SLAB, (r + 1) * SOFTMAX_SLAB)

    def scores(j, i, mask):
        hd = i // 2
        start = pl.multiple_of(j * tk, tk)
        kb = k_ref[0, pl.ds(start, tk), hd * LANES:(hd + 1) * LANES]
        s = lax.dot_general(kb, q_comp[i], (((1,), (1,)), ((), ())),
                            preferred_element_type=F32)
        if mask is not None:
            s = jnp.where(mask, s, NEG_BIG)
        s_ref[i % SCORE_SLOTS] = s
        part = s[rows(0)]
        for r in range(1, n_slab):
            part = jnp.maximum(part, s[rows(r)])
        mpart_ref[i % SCORE_SLOTS] = part

    def softmax_pv(j, i):
        slot = i % SCORE_SLOTS
        hd = i // 2
        vt = vt_ref[0, j, hd * V_AUG:(hd + 1) * V_AUG, :]
        m_old = m_ref[i]
        m_new = jnp.maximum(m_old, jnp.max(mpart_ref[slot], axis=0, keepdims=True))
        alpha = jnp.exp2(m_old - m_new)
        m_ref[i] = m_new
        m_slab = jnp.broadcast_to(m_new, (SOFTMAX_SLAB, tq))
        p = jnp.concatenate(
            [jnp.exp2(s_ref[slot, rows(r), :] - m_slab).astype(BF16) for r in range(n_slab)],
            axis=0)
        acc_ref[i] = alpha * acc_ref[i] + jnp.dot(vt, p, preferred_element_type=F32)

    def block(j, j_next, mask):
        for i in range(n_chain):
            ahead = i + SCORE_AHEAD
            if ahead < n_chain:
                scores(j, ahead, mask)
            else:
                scores(j_next, ahead - n_chain, None)
            softmax_pv(j, i)

    key_chunk = lax.broadcasted_iota(jnp.int32, (tk, tq), 0) // CHUNK
    qry_chunk = lax.broadcasted_iota(jnp.int32, (tk, tq), 1) // CHUNK
    diag_mask = key_chunk <= qry_chunk
    for i in range(SCORE_AHEAD):
        scores(qi, i, diag_mask)
    block(qi, 0, diag_mask)

    def body(j, carry):
        block(j, jnp.minimum(j + 1, qi - 1), None)
        return carry

    lax.fori_loop(0, qi, body, 0)

    lam = (jnp.exp(jnp.sum(lq1_ref[...] * lk1_ref[...], axis=-1, keepdims=True))
           - jnp.exp(jnp.sum(lq2_ref[...] * lk2_ref[...], axis=-1, keepdims=True))
           + LAM_INIT)
    for hd in range(DA_HEADS):
        a1, a2 = acc_ref[2 * hd], acc_ref[2 * hd + 1]
        o = (a1[:DA_V_DIM] / a1[DA_V_DIM:DA_V_DIM + 1]
             - lam * (a2[:DA_V_DIM] / a2[DA_V_DIM:DA_V_DIM + 1]))
        o = o * lax.rsqrt(jnp.mean(o * o, axis=0, keepdims=True) + EPS)
        o = o * sg_ref[...] * (1.0 - LAM_INIT)
        o_ref[0, :, hd * DA_V_DIM:(hd + 1) * DA_V_DIM] = o.T.astype(BF16)


def _post_mix_kernel(h_ref, oda_ref, glu_ref, halo_ref, qm_ref, mk_ref, mv_ref,
                     dww_ref, dwb_ref, lng_ref, lnb_ref, wout32_ref,
                     g2_ref, wg32_ref, wu32_ref, wd32_ref, gf_ref,
                     out_ref, wout_ref, wg_ref, wu_ref, wd_ref,
                     cbuf_ref, shift_ref, yconv_ref, pmem_ref, h2_ref, xn_ref, act_ref,
                     *, tiles_per_seq):
    n_stage = wg_ref.shape[0]
    pid = pl.program_id(0)

    @pl.when(pid < n_stage)
    def _():
        _stage_weight_slabs(pid, wg32_ref, wu32_ref, wd32_ref, wg_ref, wu_ref, wd_ref)
        slab = jnp.minimum(pid, wout_ref.shape[0] // FFN_CHUNK - 1)
        rows = pl.ds(pl.multiple_of(slab * FFN_CHUNK, FFN_CHUNK), FFN_CHUNK)
        wout_ref[rows, :] = wout32_ref[...].astype(BF16)

    @pl.when(pid >= n_stage)
    def _():
        _post_mix_step(pid - n_stage, pl.num_programs(0) - n_stage,
                       h_ref, oda_ref, glu_ref, halo_ref, qm_ref, mk_ref, mv_ref,
                       dww_ref, dwb_ref, lng_ref, lnb_ref, wout_ref,
                       g2_ref, wg_ref, wu_ref, wd_ref, gf_ref,
                       out_ref, cbuf_ref, shift_ref, yconv_ref, pmem_ref, h2_ref, xn_ref,
                       act_ref, tiles_per_seq)


def _post_mix_step(step, n_steps, h_ref, oda_ref, glu_ref, halo_ref, qm_ref, mk_ref, mv_ref,
                   dww_ref, dwb_ref, lng_ref, lnb_ref, wout_ref,
                   g2_ref, wg_ref, wu_ref, wd_ref, gf_ref,
                   out_ref, cbuf_ref, shift_ref, yconv_ref, pmem_ref, h2_ref, xn_ref, act_ref,
                   tiles_per_seq):
    tm = h_ref.shape[0]
    n_mem = mk_ref.shape[1]
    n_slabs = wg_ref.shape[0]
    tile = jnp.minimum(step, n_steps - 2)
    seq_start = (tile % tiles_per_seq) == 0
    base = CONV_HALO - (CONV_WIDTH - 1)

    @pl.when(step == 0)
    def _():
        h2_ref[1] = jnp.zeros(h2_ref.shape[1:], F32)

    def after(c):
        tile_ = act_ref[0:2 * SUBLANES, c * FFN_CHUNK:c * FFN_CHUNK + LANES]
        return jnp.sum(tile_.astype(F32) * 0.0, keepdims=True)

    def before_next_slab(done_bf16):
        sl = (slice(0, 2 * SUBLANES), slice(0, LANES))
        xn_ref[sl] = xn_ref[sl] + done_bf16 * jnp.zeros_like(done_bf16)

    def conv_setup(c):
        halo = halo_ref[...]
        cbuf_ref[0:CONV_HALO, :] = jnp.where(seq_start, jnp.zeros_like(halo), halo)
        cbuf_ref[CONV_HALO:CONV_HALO + tm, :] = glu_ref[...]
        span = tm + CONV_HALO - SUBLANES
        for r in range(1, SUBLANES):
            shift_ref[r - 1] = cbuf_ref[r:r + span, :]
        before_next_slab(shift_ref[SUBLANES - 2, 0:2 * SUBLANES, 0:LANES].astype(BF16))

    def conv_rows(r0, wait):
        y = jnp.zeros((CONV_ROWS, CONV_CH), F32) + wait
        for j in range(CONV_WIDTH):
            phase = (base + j) % SUBLANES
            start = base + j - phase + r0
            if phase == 0:
                window = cbuf_ref[start:start + CONV_ROWS, :]
            else:
                window = shift_ref[phase - 1, start:start + CONV_ROWS, :]
            y = y + dww_ref[j:j + 1, :] * window
        y = y + dwb_ref[...]
        mu = jnp.mean(y, axis=-1, keepdims=True)
        yc = y - mu
        var = jnp.mean(yc * yc, axis=-1, keepdims=True)
        y = yc * lax.rsqrt(var + EPS) * lng_ref[...] + lnb_ref[...]
        piece = (y * jax.nn.sigmoid(y)).astype(BF16)
        yconv_ref[r0:r0 + CONV_ROWS, :] = piece
        return piece

    def conv_pieces(r0s, c):
        wait = after(c - 1)
        for r0 in r0s:
            piece = conv_rows(r0, wait)
        before_next_slab(piece[0:2 * SUBLANES, 0:LANES])

    def mem_heads(heads, c):
        qm = qm_ref[...]
        head_of_lane = lax.broadcasted_iota(jnp.int32, qm.shape, 1) // MEM_HEAD_DIM
        wait = after(c - 1)
        for hd in heads:
            qh = jnp.where(head_of_lane == hd, qm, jnp.zeros_like(qm))
            sc = lax.dot_general(qh, mk_ref[0], (((1,), (1,)), ((), ())),
                                 preferred_element_type=F32)
            sc = sc - (jnp.max(sc, axis=-1, keepdims=True) + wait)
            p = jnp.exp(sc)
            p = (p / jnp.sum(p, axis=-1, keepdims=True)).astype(BF16)
            pmem_ref[:, hd * n_mem:(hd + 1) * n_mem] = p
        before_next_slab(p[0:2 * SUBLANES, 0:LANES])

    pieces = [conv_setup]
    pieces += [functools.partial(conv_pieces, (r0,)) for r0 in range(0, tm, CONV_ROWS)]
    pieces += [functools.partial(mem_heads, (0, 1)), functools.partial(mem_heads, (2, 3))]
    assert len(pieces) <= n_slabs, "at most one mixing piece per SwiGLU slab"
    pieces += [None] * (n_slabs - len(pieces))

    h2_ref[step % 2] = h_ref[...] + jnp.dot(oda_ref[...], wout_ref[0:DA_WIDTH, :],
                                            preferred_element_type=F32)

    h_prev = h2_ref[(step + 1) % 2]
    xn_ref[...] = _rms(h_prev, g2_ref[...]).astype(BF16)
    for n in range(n_slabs):
        c0 = n * FFN_CHUNK
        xn = xn_ref[...]
        gate = jnp.dot(xn, wg_ref[n], preferred_element_type=F32)
        up = jnp.dot(xn, wu_ref[n], preferred_element_type=F32)
        if pieces[n] is not None:
            pieces[n](n)
        act_ref[:, c0:c0 + FFN_CHUNK] = (gate * jax.nn.sigmoid(gate) * up).astype(BF16)
    ffn = jnp.dot(act_ref[...], wd_ref[...], preferred_element_type=F32)

    o_mem = jnp.dot(pmem_ref[...], mv_ref[0], preferred_element_type=F32).astype(BF16)
    mix = (jnp.dot(yconv_ref[...], wout_ref[DA_WIDTH:DA_WIDTH + CONV_CH, :],
                   preferred_element_type=F32)
           + jnp.dot(o_mem, wout_ref[DA_WIDTH + CONV_CH:, :], preferred_element_type=F32))
    h2_ref[step % 2] = h2_ref[step % 2] + mix
    out_ref[...] = _rms(h_prev + 0.5 * ffn, gf_ref[...])


def _resident(shape):
    return pl.BlockSpec(shape, lambda *_: (0,) * len(shape), pipeline_mode=pl.Buffered(1))


def _rope_tables(seq):
    half = DA_HEAD_DIM // 2
    inv_freq = 1.0 / (ROPE_THETA ** (jnp.arange(0, DA_HEAD_DIM, 2, dtype=F32) / DA_HEAD_DIM))
    ang = jnp.arange(seq, dtype=F32)[:, None] * inv_freq[None, :]
    cos, sin = jnp.cos(ang), jnp.sin(ang)
    reps = LANES // DA_HEAD_DIM
    cos_t = jnp.tile(jnp.concatenate([cos, cos], axis=1), (1, reps))
    sin_t = jnp.tile(jnp.concatenate([-sin, sin], axis=1), (1, reps))
    assert cos_t.shape == (seq, LANES) and half * 2 == DA_HEAD_DIM
    return cos_t, sin_t


def kernel(x, mem, ffn1_norm_g, ffn1_w_gate, ffn1_w_up, ffn1_w_down, mix_norm_g, mem_norm_g, w_in, lambda_q1, lambda_k1, lambda_q2, lambda_k2, subln_g, conv_dw_w, conv_dw_b, conv_ln_g, conv_ln_b, w_mem_kv, w_out, ffn2_norm_g, ffn2_w_gate, ffn2_w_up, ffn2_w_down, final_norm_g):
    b, s, d = x.shape
    n_mem = mem.shape[1]
    d_ff = ffn1_w_gate.shape[-1]
    in_width = w_in.shape[-1]
    assert ffn1_norm_g.shape[0] == 1, "single layer"
    tm = TOKEN_TILE
    ta = ATTN_TILE
    assert s % tm == 0 and s % ta == 0 and tm == ta and d_ff % FFN_CHUNK == 0
    t = b * s
    n_tiles = s // tm
    params = functools.partial(pltpu.CompilerParams, vmem_limit_bytes=VMEM_LIMIT)

    cos_t, sin_t = _rope_tables(s)
    row = lambda a: a.reshape(1, -1)
    bf = lambda a: a[0].astype(BF16)

    n_stage = d_ff // FFN_CHUNK
    in_slabs = in_width // MXU_COLS
    assert FFN_CHUNK == MXU_COLS and in_slabs <= n_stage
    tile_of = lambda i: jnp.maximum(i - n_stage, 0)
    slab_of = lambda i: jnp.minimum(i, n_stage - 1)
    tok = lambda w: pl.BlockSpec((tm, w), lambda i: (tile_of(i), 0))
    col_slab = pl.BlockSpec((d, FFN_CHUNK), lambda i: (0, slab_of(i)))
    h1, q, k, vt, glu, qm = pl.pallas_call(
        _pre_mix_kernel,
        grid=(n_stage + t // tm,),
        in_specs=[tok(d), _resident((1, d)), col_slab, col_slab,
                  pl.BlockSpec((FFN_CHUNK, d), lambda i: (slab_of(i), 0)), _resident((1, d)),
                  pl.BlockSpec((d, MXU_COLS), lambda i: (0, jnp.minimum(i, in_slabs - 1))),
                  pl.BlockSpec((tm, LANES), lambda i: (tile_of(i) % n_tiles, 0)),
                  pl.BlockSpec((tm, LANES), lambda i: (tile_of(i) % n_tiles, 0))],
        out_specs=[tok(d), tok(DA_WIDTH), tok(DA_WIDTH),
                   pl.BlockSpec((1, DA_HEADS * V_AUG, tm), lambda i: (tile_of(i), 0, 0)),
                   tok(CONV_CH), tok(MEM_WIDTH)],
        out_shape=[jax.ShapeDtypeStruct((t, d), F32),
                   jax.ShapeDtypeStruct((t, DA_WIDTH), BF16),
                   jax.ShapeDtypeStruct((t, DA_WIDTH), BF16),
                   jax.ShapeDtypeStruct((t // tm, DA_HEADS * V_AUG, tm), BF16),
                   jax.ShapeDtypeStruct((t, CONV_CH), F32),
                   jax.ShapeDtypeStruct((t, MEM_WIDTH), BF16)],
        scratch_shapes=[pltpu.VMEM((n_stage, d, FFN_CHUNK), BF16),
                        pltpu.VMEM((n_stage, d, FFN_CHUNK), BF16),
                        pltpu.VMEM((d_ff, d), BF16),
                        pltpu.VMEM((in_slabs, d, MXU_COLS), BF16),
                        pltpu.VMEM((tm, d_ff), BF16)],
        compiler_params=params(dimension_semantics=("arbitrary",)),
        name="pre_mix",
    )(x.reshape(t, d), row(ffn1_norm_g), ffn1_w_gate.reshape(d, d_ff),
      ffn1_w_up.reshape(d, d_ff), ffn1_w_down.reshape(d_ff, d), row(mix_norm_g),
      w_in.reshape(d, in_width), cos_t, sin_t)

    mk, mv = pl.pallas_call(
        _mem_kv_kernel,
        grid=(b,),
        in_specs=[pl.BlockSpec((1, n_mem, d), lambda i: (i, 0, 0)), _resident((1, d)),
                  _resident((d, 2 * MEM_WIDTH))],
        out_specs=[pl.BlockSpec((1, n_mem, MEM_WIDTH), lambda i: (i, 0, 0)),
                   pl.BlockSpec((1, MEM_HEADS * n_mem, MEM_WIDTH), lambda i: (i, 0, 0))],
        out_shape=[jax.ShapeDtypeStruct((b, n_mem, MEM_WIDTH), BF16),
                   jax.ShapeDtypeStruct((b, MEM_HEADS * n_mem, MEM_WIDTH), BF16)],
        compiler_params=params(dimension_semantics=("parallel",)),
        name="mem_kv",
    )(mem, row(mem_norm_g), bf(w_mem_kv))

    lam_spec = _resident((1, DA_HEAD_DIM))
    o_da = pl.pallas_call(
        _diff_attn_kernel,
        grid=(b, s // ta),
        in_specs=[lam_spec, lam_spec, lam_spec, lam_spec, _resident((DA_V_DIM, 1)),
                  pl.BlockSpec((1, ta, DA_WIDTH), lambda bi, qi: (bi, qi, 0)),
                  pl.BlockSpec((1, s, DA_WIDTH), lambda bi, qi: (bi, 0, 0),
                               pipeline_mode=pl.Buffered(1)),
                  pl.BlockSpec((1, s // ta, DA_HEADS * V_AUG, ta), lambda bi, qi: (bi, 0, 0, 0),
                               pipeline_mode=pl.Buffered(1))],
        out_specs=pl.BlockSpec((1, ta, DA_WIDTH), lambda bi, qi: (bi, qi, 0)),
        out_shape=jax.ShapeDtypeStruct((b, s, DA_WIDTH), BF16),
        scratch_shapes=[pltpu.VMEM((2 * DA_HEADS, 1, ta), F32),
                        pltpu.VMEM((2 * DA_HEADS, V_AUG, ta), F32),
                        pltpu.VMEM((SCORE_SLOTS, ta, ta), F32),
                        pltpu.VMEM((SCORE_SLOTS, SOFTMAX_SLAB, ta), F32)],
        compiler_params=params(dimension_semantics=("parallel", "arbitrary")),
        name="diff_attn",
    )(lambda_q1, lambda_k1, lambda_q2, lambda_k2, subln_g.reshape(DA_V_DIM, 1),
      q.reshape(b, s, DA_WIDTH), k.reshape(b, s, DA_WIDTH),
      vt.reshape(b, s // ta, DA_HEADS * V_AUG, ta))

    halo_blocks = tm // CONV_HALO
    n_steps = t // tm + 1
    last = t // tm - 1
    out_slabs = d // FFN_CHUNK
    assert out_slabs <= n_stage
    mixed_tile = lambda i: jnp.clip(i - n_stage, 0, last)
    mixed = lambda w: pl.BlockSpec((tm, w), lambda i: (mixed_tile(i), 0))
    per_seq = lambda rows: pl.BlockSpec((1, rows, MEM_WIDTH),
                                        lambda i: (mixed_tile(i) // n_tiles, 0, 0))
    out = pl.pallas_call(
        functools.partial(_post_mix_kernel, tiles_per_seq=n_tiles),
        grid=(n_stage + n_steps,),
        in_specs=[mixed(d), mixed(DA_WIDTH), mixed(CONV_CH),
                  pl.BlockSpec((CONV_HALO, CONV_CH),
                               lambda i: (jnp.maximum(mixed_tile(i) * halo_blocks - 1, 0), 0)),
                  mixed(MEM_WIDTH), per_seq(n_mem), per_seq(MEM_HEADS * n_mem),
                  _resident((CONV_WIDTH, CONV_CH)), _resident((1, CONV_CH)),
                  _resident((1, CONV_CH)), _resident((1, CONV_CH)),
                  pl.BlockSpec((FFN_CHUNK, d), lambda i: (jnp.minimum(i, out_slabs - 1), 0)),
                  _resident((1, d)), col_slab, col_slab,
                  pl.BlockSpec((FFN_CHUNK, d), lambda i: (slab_of(i), 0)), _resident((1, d))],
        out_specs=pl.BlockSpec((tm, d), lambda i: (jnp.maximum(i - n_stage - 1, 0), 0)),
        out_shape=jax.ShapeDtypeStruct((t, d), F32),
        scratch_shapes=[pltpu.VMEM((d, d), BF16),
                        pltpu.VMEM((n_stage, d, FFN_CHUNK), BF16),
                        pltpu.VMEM((n_stage, d, FFN_CHUNK), BF16),
                        pltpu.VMEM((d_ff, d), BF16),
                        pltpu.VMEM((CONV_HALO + tm, CONV_CH), F32),
                        pltpu.VMEM((SUBLANES - 1, tm + CONV_HALO - SUBLANES, CONV_CH), F32),
                        pltpu.VMEM((tm, CONV_CH), BF16),
                        pltpu.VMEM((tm, MEM_HEADS * n_mem), BF16),
                        pltpu.VMEM((2, tm, d), F32),
                        pltpu.VMEM((tm, d), BF16),
                        pltpu.VMEM((tm, d_ff), BF16)],
        compiler_params=params(dimension_semantics=("arbitrary",)),
        name="post_mix",
    )(h1, o_da.reshape(t, DA_WIDTH), glu, glu, qm, mk, mv, conv_dw_w[0], conv_dw_b,
      conv_ln_g, conv_ln_b, w_out.reshape(d, d), row(ffn2_norm_g),
      ffn2_w_gate.reshape(d, d_ff), ffn2_w_up.reshape(d, d_ff),
      ffn2_w_down.reshape(d_ff, d), row(final_norm_g))
    return out.reshape(b, s, d)
```

```python
import functools
import math

import jax
import jax.numpy as jnp
from jax import lax
from jax.experimental import pallas as pl
from jax.experimental.pallas import tpu as pltpu

F32 = jnp.float32
BF16 = jnp.bfloat16

EPS = 1e-5
ROPE_THETA = 10000.0
CHUNK = 64
DA_HEADS = 4
DA_HEAD_DIM = 64
DA_V_DIM = 128
V_AUG = DA_V_DIM + 16
DA_WIDTH = 512
CONV_CH = 256
CONV_WIDTH = 31
MEM_HEADS = 4
MEM_HEAD_DIM = 64
MEM_WIDTH = 256
LAM_INIT = 0.8 - 0.6 * math.exp(-0.3 * 0)

V7X_VMEM_BYTES = 64 * 1024 * 1024
VMEM_LIMIT = V7X_VMEM_BYTES - 8 * 1024 * 1024
LANES = 128
SUBLANES = 8
MXU_COLS = 256

TOKEN_TILE = 512
ATTN_TILE = 512
FFN_CHUNK = 256
CONV_HALO = 32
CONV_ROWS = 64
SOFTMAX_SLAB = 16
SCORE_AHEAD = 2
SCORE_SLOTS = 4
NEG_BIG = -1e30
LOG2_E = math.log2(math.e)


def _rms(x, g):
    return x * lax.rsqrt(jnp.mean(x * x, axis=-1, keepdims=True) + EPS) * g


def _swiglu(xn, wg_ref, wu_ref, wd_ref, act_ref):
    for n in range(wg_ref.shape[0]):
        c = n * FFN_CHUNK
        gate = jnp.dot(xn, wg_ref[n], preferred_element_type=F32)
        up = jnp.dot(xn, wu_ref[n], preferred_element_type=F32)
        act_ref[:, c:c + FFN_CHUNK] = (gate * jax.nn.sigmoid(gate) * up).astype(BF16)
    return jnp.dot(act_ref[...], wd_ref[...], preferred_element_type=F32)


def _stage_weight_slabs(step, wg32_ref, wu32_ref, wd32_ref, wg_ref, wu_ref, wd_ref):
    wg_ref[step] = wg32_ref[...].astype(BF16)
    wu_ref[step] = wu32_ref[...].astype(BF16)
    rows = pl.ds(pl.multiple_of(step * FFN_CHUNK, FFN_CHUNK), FFN_CHUNK)
    wd_ref[rows, :] = wd32_ref[...].astype(BF16)


def _rope(x, cos, sin_signed, first_half):
    partner = jnp.where(first_half,
                        pltpu.roll(x, LANES - DA_HEAD_DIM // 2, axis=1),
                        pltpu.roll(x, DA_HEAD_DIM // 2, axis=1))
    return x * cos + partner * sin_signed


def _pre_mix_kernel(x_ref, g1_ref, wg32_ref, wu32_ref, wd32_ref, gm_ref, win32_ref,
                    cos_ref, sin_ref,
                    h_ref, q_ref, k_ref, vt_ref, glu_ref, qm_ref,
                    wg_ref, wu_ref, wd_ref, win_ref, act_ref):
    step = pl.program_id(0)
    n_stage = wg_ref.shape[0]

    @pl.when(step < n_stage)
    def _():
        _stage_weight_slabs(step, wg32_ref, wu32_ref, wd32_ref, wg_ref, wu_ref, wd_ref)
        win_ref[jnp.minimum(step, win_ref.shape[0] - 1)] = win32_ref[...].astype(BF16)

    @pl.when(step >= n_stage)
    def _():
        _pre_mix_tile(x_ref, g1_ref, wg_ref, wu_ref, wd_ref, gm_ref, win_ref, cos_ref, sin_ref,
                      h_ref, q_ref, k_ref, vt_ref, glu_ref, qm_ref, act_ref)


def _pre_mix_tile(x_ref, g1_ref, wg_ref, wu_ref, wd_ref, gm_ref, win_ref, cos_ref, sin_ref,
                  h_ref, q_ref, k_ref, vt_ref, glu_ref, qm_ref, act_ref):
    x = x_ref[...]
    xn = _rms(x, g1_ref[...]).astype(BF16)
    h = x + 0.5 * _swiglu(xn, wg_ref, wu_ref, wd_ref, act_ref)
    h_ref[...] = h

    n = _rms(h, gm_ref[...]).astype(BF16)
    proj = lambda slab: jnp.dot(n, win_ref[slab], preferred_element_type=F32)
    cos = cos_ref[...]
    sin_signed = sin_ref[...]
    lane = lax.broadcasted_iota(jnp.int32, cos.shape, 1)
    first_half = (lane % DA_HEAD_DIM) < (DA_HEAD_DIM // 2)
    scale = DA_HEAD_DIM ** -0.5 * LOG2_E
    q_slabs = DA_WIDTH // MXU_COLS
    for sl in range(q_slabs):
        c0 = sl * MXU_COLS
        qs = proj(sl)
        ks = proj(q_slabs + sl)
        for l0 in range(0, MXU_COLS, LANES):
            q_ref[:, c0 + l0:c0 + l0 + LANES] = (
                _rope(qs[:, l0:l0 + LANES], cos, sin_signed, first_half) * scale).astype(BF16)
            k_ref[:, c0 + l0:c0 + l0 + LANES] = _rope(
                ks[:, l0:l0 + LANES], cos, sin_signed, first_half).astype(BF16)
    vt = jnp.concatenate([proj(2 * q_slabs + sl).T for sl in range(q_slabs)],
                         axis=0).astype(BF16)
    ones = jnp.ones((V_AUG - DA_V_DIM, vt.shape[1]), BF16)
    for hd in range(DA_HEADS):
        vt_ref[0, hd * V_AUG:hd * V_AUG + DA_V_DIM, :] = vt[hd * DA_V_DIM:(hd + 1) * DA_V_DIM]
        vt_ref[0, hd * V_AUG + DA_V_DIM:(hd + 1) * V_AUG, :] = ones
    sl = 3 * q_slabs
    glu_ref[...] = proj(sl) * jax.nn.sigmoid(proj(sl + 1))
    qm_ref[...] = (proj(sl + 2) * MEM_HEAD_DIM ** -0.5).astype(BF16)


def _mem_kv_kernel(mem_ref, g_ref, w_ref, mk_ref, mv_ref):
    n_mem = mem_ref.shape[1]
    mn = _rms(mem_ref[0], g_ref[...]).astype(BF16)
    kv = jnp.dot(mn, w_ref[...], preferred_element_type=F32)
    mk_ref[0] = kv[:, :MEM_WIDTH].astype(BF16)
    mv = kv[:, MEM_WIDTH:]
    head_of_lane = lax.broadcasted_iota(jnp.int32, mv.shape, 1) // MEM_HEAD_DIM
    for hd in range(MEM_HEADS):
        mv_ref[0, hd * n_mem:(hd + 1) * n_mem, :] = jnp.where(
            head_of_lane == hd, mv, 0.0).astype(BF16)


def _diff_attn_kernel(lq1_ref, lk1_ref, lq2_ref, lk2_ref, sg_ref,
                      q_ref, k_ref, vt_ref, o_ref, m_ref, acc_ref, s_ref, mpart_ref):
    qi = pl.program_id(1)
    tq = q_ref.shape[1]
    tk = tq

    lane = lax.broadcasted_iota(jnp.int32, (tq, LANES), 1)
    q_comp = []
    for hd in range(DA_HEADS):
        q = q_ref[0, :, hd * LANES:(hd + 1) * LANES]
        zero = jnp.zeros_like(q)
        q_comp.append(jnp.where(lane < DA_HEAD_DIM, q, zero))
        q_comp.append(jnp.where(lane >= DA_HEAD_DIM, q, zero))

    m_ref[...] = jnp.full(m_ref.shape, NEG_BIG, F32)
    acc_ref[...] = jnp.zeros(acc_ref.shape, F32)

    n_chain = 2 * DA_HEADS
    n_slab = tk // SOFTMAX_SLAB
    rows = lambda r: slice(r * SOFTMAX_SLAB, (r + 1) * SOFTMAX_SLAB)

    def scores(j, i, mask):
        hd = i // 2
        start = pl.multiple_of(j * tk, tk)
        kb = k_ref[0, pl.ds(start, tk), hd * LANES:(hd + 1) * LANES]
        s = lax.dot_general(kb, q_comp[i], (((1,), (1,)), ((), ())),
                            preferred_element_type=F32)
        if mask is not None:
            s = jnp.where(mask, s, NEG_BIG)
        s_ref[i % SCORE_SLOTS] = s
        part = s[rows(0)]
        for r in range(1, n_slab):
            part = jnp.maximum(part, s[rows(r)])
        mpart_ref[i % SCORE_SLOTS] = part

    def softmax_pv(j, i):
        slot = i % SCORE_SLOTS
        hd = i // 2
        vt = vt_ref[0, j, hd * V_AUG:(hd + 1) * V_AUG, :]
        m_old = m_ref[i]
        m_new = jnp.maximum(m_old, jnp.max(mpart_ref[slot], axis=0, keepdims=True))
        alpha = jnp.exp2(m_old - m_new)
        m_ref[i] = m_new
        m_slab = jnp.broadcast_to(m_new, (SOFTMAX_SLAB, tq))
        p = jnp.concatenate(
            [jnp.exp2(s_ref[slot, rows(r), :] - m_slab).astype(BF16) for r in range(n_slab)],
            axis=0)
        acc_ref[i] = alpha * acc_ref[i] + jnp.dot(vt, p, preferred_element_type=F32)

    def block(j, j_next, mask):
        for i in range(n_chain):
            ahead = i + SCORE_AHEAD
            if ahead < n_chain:
                scores(j, ahead, mask)
            else:
                scores(j_next, ahead - n_chain, None)
            softmax_pv(j, i)

    key_chunk = lax.broadcasted_iota(jnp.int32, (tk, tq), 0) // CHUNK
    qry_chunk = lax.broadcasted_iota(jnp.int32, (tk, tq), 1) // CHUNK
    diag_mask = key_chunk <= qry_chunk
    for i in range(SCORE_AHEAD):
        scores(qi, i, diag_mask)
    block(qi, 0, diag_mask)

    def body(j, carry):
        block(j, jnp.minimum(j + 1, qi - 1), None)
        return carry

    lax.fori_loop(0, qi, body, 0)

    lam = (jnp.exp(jnp.sum(lq1_ref[...] * lk1_ref[...], axis=-1, keepdims=True))
           - jnp.exp(jnp.sum(lq2_ref[...] * lk2_ref[...], axis=-1, keepdims=True))
           + LAM_INIT)
    for hd in range(DA_HEADS):
        a1, a2 = acc_ref[2 * hd], acc_ref[2 * hd + 1]
        o = (a1[:DA_V_DIM] / a1[DA_V_DIM:DA_V_DIM + 1]
             - lam * (a2[:DA_V_DIM] / a2[DA_V_DIM:DA_V_DIM + 1]))
        o = o * lax.rsqrt(jnp.mean(o * o, axis=0, keepdims=True) + EPS)
        o = o * sg_ref[...] * (1.0 - LAM_INIT)
        o_ref[0, :, hd * DA_V_DIM:(hd + 1) * DA_V_DIM] = o.T.astype(BF16)


def _post_mix_kernel(h_ref, oda_ref, glu_ref, halo_ref, qm_ref, mk_ref, mv_ref,
                     dww_ref, dwb_ref, lng_ref, lnb_ref, wout32_ref,
                     g2_ref, wg32_ref, wu32_ref, wd32_ref, gf_ref,
                     out_ref, wout_ref, wg_ref, wu_ref, wd_ref,
                     cbuf_ref, shift_ref, yconv_ref, pmem_ref, h2_ref, xn_ref, act_ref,
                     *, tiles_per_seq):
    n_stage = wg_ref.shape[0]
    pid = pl.program_id(0)

    @pl.when(pid < n_stage)
    def _():
        _stage_weight_slabs(pid, wg32_ref, wu32_ref, wd32_ref, wg_ref, wu_ref, wd_ref)
        slab = jnp.minimum(pid, wout_ref.shape[0] // FFN_CHUNK - 1)
        rows = pl.ds(pl.multiple_of(slab * FFN_CHUNK, FFN_CHUNK), FFN_CHUNK)
        wout_ref[rows, :] = wout32_ref[...].astype(BF16)

    @pl.when(pid >= n_stage)
    def _():
        _post_mix_step(pid - n_stage, pl.num_programs(0) - n_stage,
                       h_ref, oda_ref, glu_ref, halo_ref, qm_ref, mk_ref, mv_ref,
                       dww_ref, dwb_ref, lng_ref, lnb_ref, wout_ref,
                       g2_ref, wg_ref, wu_ref, wd_ref, gf_ref,
                       out_ref, cbuf_ref, shift_ref, yconv_ref, pmem_ref, h2_ref, xn_ref,
                       act_ref, tiles_per_seq)


def _post_mix_step(step, n_steps, h_ref, oda_ref, glu_ref, halo_ref, qm_ref, mk_ref, mv_ref,
                   dww_ref, dwb_ref, lng_ref, lnb_ref, wout_ref,
                   g2_ref, wg_ref, wu_ref, wd_ref, gf_ref,
                   out_ref, cbuf_ref, shift_ref, yconv_ref, pmem_ref, h2_ref, xn_ref, act_ref,
                   tiles_per_seq):
    tm = h_ref.shape[0]
    n_mem = mk_ref.shape[1]
    n_slabs = wg_ref.shape[0]
    tile = jnp.minimum(step, n_steps - 2)
    seq_start = (tile % tiles_per_seq) == 0
    base = CONV_HALO - (CONV_WIDTH - 1)

    @pl.when(step == 0)
    def _():
        h2_ref[1] = jnp.zeros(h2_ref.shape[1:], F32)

    def after(c):
        tile_ = act_ref[0:2 * SUBLANES, c * FFN_CHUNK:c * FFN_CHUNK + LANES]
        return jnp.sum(tile_.astype(F32) * 0.0, keepdims=True)

    def before_next_slab(done_bf16):
        sl = (slice(0, 2 * SUBLANES), slice(0, LANES))
        xn_ref[sl] = xn_ref[sl] + done_bf16 * jnp.zeros_like(done_bf16)

    def conv_setup(c):
        halo = halo_ref[...]
        cbuf_ref[0:CONV_HALO, :] = jnp.where(seq_start, jnp.zeros_like(halo), halo)
        cbuf_ref[CONV_HALO:CONV_HALO + tm, :] = glu_ref[...]
        span = tm + CONV_HALO - SUBLANES
        for r in range(1, SUBLANES):
            shift_ref[r - 1] = cbuf_ref[r:r + span, :]
        before_next_slab(shift_ref[SUBLANES - 2, 0:2 * SUBLANES, 0:LANES].astype(BF16))

    def conv_rows(r0, wait):
        y = jnp.zeros((CONV_ROWS, CONV_CH), F32) + wait
        for j in range(CONV_WIDTH):
            phase = (base + j) % SUBLANES
            start = base + j - phase + r0
            if phase == 0:
                window = cbuf_ref[start:start + CONV_ROWS, :]
            else:
                window = shift_ref[phase - 1, start:start + CONV_ROWS, :]
            y = y + dww_ref[j:j + 1, :] * window
        y = y + dwb_ref[...]
        mu = jnp.mean(y, axis=-1, keepdims=True)
        yc = y - mu
        var = jnp.mean(yc * yc, axis=-1, keepdims=True)
        y = yc * lax.rsqrt(var + EPS) * lng_ref[...] + lnb_ref[...]
        piece = (y * jax.nn.sigmoid(y)).astype(BF16)
        yconv_ref[r0:r0 + CONV_ROWS, :] = piece
        return piece

    def conv_pieces(r0s, c):
        wait = after(c - 1)
        for r0 in r0s:
            piece = conv_rows(r0, wait)
        before_next_slab(piece[0:2 * SUBLANES, 0:LANES])

    def mem_heads(heads, c):
        qm = qm_ref[...]
        head_of_lane = lax.broadcasted_iota(jnp.int32, qm.shape, 1) // MEM_HEAD_DIM
        wait = after(c - 1)
        for hd in heads:
            qh = jnp.where(head_of_lane == hd, qm, jnp.zeros_like(qm))
            sc = lax.dot_general(qh, mk_ref[0], (((1,), (1,)), ((), ())),
                                 preferred_element_type=F32)
            sc = sc - (jnp.max(sc, axis=-1, keepdims=True) + wait)
            p = jnp.exp(sc)
            p = (p / jnp.sum(p, axis=-1, keepdims=True)).astype(BF16)
            pmem_ref[:, hd * n_mem:(hd + 1) * n_mem] = p
        before_next_slab(p[0:2 * SUBLANES, 0:LANES])

    pieces = [conv_setup]
    pieces += [functools.partial(conv_pieces, (r0,)) for r0 in range(0, tm, CONV_ROWS)]
    pieces += [functools.partial(mem_heads, (0, 1)), functools.partial(mem_heads, (2, 3))]
    assert len(pieces) <= n_slabs, "at most one mixing piece per SwiGLU slab"
    pieces += [None] * (n_slabs - len(pieces))

    h_prev = h2_ref[(step + 1) % 2]
    xn_ref[...] = _rms(h_prev, g2_ref[...]).astype(BF16)

    h2_ref[step % 2] = h_ref[...] + jnp.dot(oda_ref[...], wout_ref[0:DA_WIDTH, :],
                                            preferred_element_type=F32)
    for n in range(n_slabs):
        c0 = n * FFN_CHUNK
        xn = xn_ref[...]
        gate = jnp.dot(xn, wg_ref[n], preferred_element_type=F32)
        up = jnp.dot(xn, wu_ref[n], preferred_element_type=F32)
        if pieces[n] is not None:
            pieces[n](n)
        act_ref[:, c0:c0 + FFN_CHUNK] = (gate * jax.nn.sigmoid(gate) * up).astype(BF16)
    ffn = jnp.dot(act_ref[...], wd_ref[...], preferred_element_type=F32)

    o_mem = jnp.dot(pmem_ref[...], mv_ref[0], preferred_element_type=F32).astype(BF16)
    mix = (jnp.dot(yconv_ref[...], wout_ref[DA_WIDTH:DA_WIDTH + CONV_CH, :],
                   preferred_element_type=F32)
           + jnp.dot(o_mem, wout_ref[DA_WIDTH + CONV_CH:, :], preferred_element_type=F32))
    h2_ref[step % 2] = h2_ref[step % 2] + mix
    out_ref[...] = _rms(h_prev + 0.5 * ffn, gf_ref[...])


def _resident(shape):
    return pl.BlockSpec(shape, lambda *_: (0,) * len(shape), pipeline_mode=pl.Buffered(1))


def _rope_tables(seq):
    half = DA_HEAD_DIM // 2
    inv_freq = 1.0 / (ROPE_THETA ** (jnp.arange(0, DA_HEAD_DIM, 2, dtype=F32) / DA_HEAD_DIM))
    inv_lane = jnp.tile(inv_freq, LANES // half)
    ang = jnp.arange(seq, dtype=F32)[:, None] * inv_lane[None, :]
    sign = jnp.where((jnp.arange(LANES) % DA_HEAD_DIM) < half, -1.0, 1.0).astype(F32)
    return jnp.cos(ang), jnp.sin(ang) * sign[None, :]


def kernel(x, mem, ffn1_norm_g, ffn1_w_gate, ffn1_w_up, ffn1_w_down, mix_norm_g, mem_norm_g, w_in, lambda_q1, lambda_k1, lambda_q2, lambda_k2, subln_g, conv_dw_w, conv_dw_b, conv_ln_g, conv_ln_b, w_mem_kv, w_out, ffn2_norm_g, ffn2_w_gate, ffn2_w_up, ffn2_w_down, final_norm_g):
    b, s, d = x.shape
    n_mem = mem.shape[1]
    d_ff = ffn1_w_gate.shape[-1]
    in_width = w_in.shape[-1]
    assert ffn1_norm_g.shape[0] == 1, "single layer"
    tm = TOKEN_TILE
    ta = ATTN_TILE
    assert s % tm == 0 and s % ta == 0 and tm == ta and d_ff % FFN_CHUNK == 0
    t = b * s
    n_tiles = s // tm
    params = functools.partial(pltpu.CompilerParams, vmem_limit_bytes=VMEM_LIMIT)

    cos_t, sin_t = _rope_tables(s)
    row = lambda a: a.reshape(1, -1)
    bf = lambda a: a[0].astype(BF16)

    n_stage = d_ff // FFN_CHUNK
    in_slabs = in_width // MXU_COLS
    assert FFN_CHUNK == MXU_COLS and in_slabs <= n_stage
    tile_of = lambda i: jnp.maximum(i - n_stage, 0)
    slab_of = lambda i: jnp.minimum(i, n_stage - 1)
    tok = lambda w: pl.BlockSpec((tm, w), lambda i: (tile_of(i), 0))
    col_slab = pl.BlockSpec((d, FFN_CHUNK), lambda i: (0, slab_of(i)))
    h1, q, k, vt, glu, qm = pl.pallas_call(
        _pre_mix_kernel,
        grid=(n_stage + t // tm,),
        in_specs=[tok(d), _resident((1, d)), col_slab, col_slab,
                  pl.BlockSpec((FFN_CHUNK, d), lambda i: (slab_of(i), 0)), _resident((1, d)),
                  pl.BlockSpec((d, MXU_COLS), lambda i: (0, jnp.minimum(i, in_slabs - 1))),
                  pl.BlockSpec((tm, LANES), lambda i: (tile_of(i) % n_tiles, 0)),
                  pl.BlockSpec((tm, LANES), lambda i: (tile_of(i) % n_tiles, 0))],
        out_specs=[tok(d), tok(DA_WIDTH), tok(DA_WIDTH),
                   pl.BlockSpec((1, DA_HEADS * V_AUG, tm), lambda i: (tile_of(i), 0, 0)),
                   tok(CONV_CH), tok(MEM_WIDTH)],
        out_shape=[jax.ShapeDtypeStruct((t, d), F32),
                   jax.ShapeDtypeStruct((t, DA_WIDTH), BF16),
                   jax.ShapeDtypeStruct((t, DA_WIDTH), BF16),
                   jax.ShapeDtypeStruct((t // tm, DA_HEADS * V_AUG, tm), BF16),
                   jax.ShapeDtypeStruct((t, CONV_CH), F32),
                   jax.ShapeDtypeStruct((t, MEM_WIDTH), BF16)],
        scratch_shapes=[pltpu.VMEM((n_stage, d, FFN_CHUNK), BF16),
                        pltpu.VMEM((n_stage, d, FFN_CHUNK), BF16),
                        pltpu.VMEM((d_ff, d), BF16),
                        pltpu.VMEM((in_slabs, d, MXU_COLS), BF16),
                        pltpu.VMEM((tm, d_ff), BF16)],
        compiler_params=params(dimension_semantics=("arbitrary",)),
        name="pre_mix",
    )(x.reshape(t, d), row(ffn1_norm_g), ffn1_w_gate.reshape(d, d_ff),
      ffn1_w_up.reshape(d, d_ff), ffn1_w_down.reshape(d_ff, d), row(mix_norm_g),
      w_in.reshape(d, in_width), cos_t, sin_t)

    mk, mv = pl.pallas_call(
        _mem_kv_kernel,
        grid=(b,),
        in_specs=[pl.BlockSpec((1, n_mem, d), lambda i: (i, 0, 0)), _resident((1, d)),
                  _resident((d, 2 * MEM_WIDTH))],
        out_specs=[pl.BlockSpec((1, n_mem, MEM_WIDTH), lambda i: (i, 0, 0)),
                   pl.BlockSpec((1, MEM_HEADS * n_mem, MEM_WIDTH), lambda i: (i, 0, 0))],
        out_shape=[jax.ShapeDtypeStruct((b, n_mem, MEM_WIDTH), BF16),
                   jax.ShapeDtypeStruct((b, MEM_HEADS * n_mem, MEM_WIDTH), BF16)],
        compiler_params=params(dimension_semantics=("parallel",)),
        name="mem_kv",
    )(mem, row(mem_norm_g), bf(w_mem_kv))

    lam_spec = _resident((1, DA_HEAD_DIM))
    o_da = pl.pallas_call(
        _diff_attn_kernel,
        grid=(b, s // ta),
        in_specs=[lam_spec, lam_spec, lam_spec, lam_spec, _resident((DA_V_DIM, 1)),
                  pl.BlockSpec((1, ta, DA_WIDTH), lambda bi, qi: (bi, qi, 0)),
                  pl.BlockSpec((1, s, DA_WIDTH), lambda bi, qi: (bi, 0, 0),
                               pipeline_mode=pl.Buffered(1)),
                  pl.BlockSpec((1, s // ta, DA_HEADS * V_AUG, ta), lambda bi, qi: (bi, 0, 0, 0),
                               pipeline_mode=pl.Buffered(1))],
        out_specs=pl.BlockSpec((1, ta, DA_WIDTH), lambda bi, qi: (bi, qi, 0)),
        out_shape=jax.ShapeDtypeStruct((b, s, DA_WIDTH), BF16),
        scratch_shapes=[pltpu.VMEM((2 * DA_HEADS, 1, ta), F32),
                        pltpu.VMEM((2 * DA_HEADS, V_AUG, ta), F32),
                        pltpu.VMEM((SCORE_SLOTS, ta, ta), F32),
                        pltpu.VMEM((SCORE_SLOTS, SOFTMAX_SLAB, ta), F32)],
        compiler_params=params(dimension_semantics=("parallel", "arbitrary")),
        name="diff_attn",
    )(lambda_q1, lambda_k1, lambda_q2, lambda_k2, subln_g.reshape(DA_V_DIM, 1),
      q.reshape(b, s, DA_WIDTH), k.reshape(b, s, DA_WIDTH),
      vt.reshape(b, s // ta, DA_HEADS * V_AUG, ta))

    halo_blocks = tm // CONV_HALO
    n_steps = t // tm + 1
    last = t // tm - 1
    out_slabs = d // FFN_CHUNK
    assert out_slabs <= n_stage
    mixed_tile = lambda i: jnp.clip(i - n_stage, 0, last)
    mixed = lambda w: pl.BlockSpec((tm, w), lambda i: (mixed_tile(i), 0))
    per_seq = lambda rows: pl.BlockSpec((1, rows, MEM_WIDTH),
                                        lambda i: (mixed_tile(i) // n_tiles, 0, 0))
    out = pl.pallas_call(
        functools.partial(_post_mix_kernel, tiles_per_seq=n_tiles),
        grid=(n_stage + n_steps,),
        in_specs=[mixed(d), mixed(DA_WIDTH), mixed(CONV_CH),
                  pl.BlockSpec((CONV_HALO, CONV_CH),
                               lambda i: (jnp.maximum(mixed_tile(i) * halo_blocks - 1, 0), 0)),
                  mixed(MEM_WIDTH), per_seq(n_mem), per_seq(MEM_HEADS * n_mem),
                  _resident((CONV_WIDTH, CONV_CH)), _resident((1, CONV_CH)),
                  _resident((1, CONV_CH)), _resident((1, CONV_CH)),
                  pl.BlockSpec((FFN_CHUNK, d), lambda i: (jnp.minimum(i, out_slabs - 1), 0)),
                  _resident((1, d)), col_slab, col_slab,
                  pl.BlockSpec((FFN_CHUNK, d), lambda i: (slab_of(i), 0)), _resident((1, d))],
        out_specs=pl.BlockSpec((tm, d), lambda i: (jnp.maximum(i - n_stage - 1, 0), 0)),
        out_shape=jax.ShapeDtypeStruct((t, d), F32),
        scratch_shapes=[pltpu.VMEM((d, d), BF16),
                        pltpu.VMEM((n_stage, d, FFN_CHUNK), BF16),
                        pltpu.VMEM((n_stage, d, FFN_CHUNK), BF16),
                        pltpu.VMEM((d_ff, d), BF16),
                        pltpu.VMEM((CONV_HALO + tm, CONV_CH), F32),
                        pltpu.VMEM((SUBLANES - 1, tm + CONV_HALO - SUBLANES, CONV_CH), F32),
                        pltpu.VMEM((tm, CONV_CH), BF16),
                        pltpu.VMEM((tm, MEM_HEADS * n_mem), BF16),
                        pltpu.VMEM((2, tm, d), F32),
                        pltpu.VMEM((tm, d), BF16),
                        pltpu.VMEM((tm, d_ff), BF16)],
        compiler_params=params(dimension_semantics=("arbitrary",)),
        name="post_mix",
    )(h1, o_da.reshape(t, DA_WIDTH), glu, glu, qm, mk, mv, conv_dw_w[0], conv_dw_b,
      conv_ln_g, conv_ln_b, w_out.reshape(d, d), row(ffn2_norm_g),
      ffn2_w_gate.reshape(d, d_ff), ffn2_w_up.reshape(d, d_ff),
      ffn2_w_down.reshape(d_ff, d), row(final_norm_g))
    return out.reshape(b, s, d)
```

```python
import functools
import math

import jax
import jax.numpy as jnp
from jax import lax
from jax.experimental import pallas as pl
from jax.experimental.pallas import tpu as pltpu

F32 = jnp.float32
BF16 = jnp.bfloat16

EPS = 1e-5
ROPE_THETA = 10000.0
CHUNK = 64
DA_HEADS = 4
DA_HEAD_DIM = 64
DA_V_DIM = 128
V_AUG = DA_V_DIM + 16
DA_WIDTH = 512
CONV_CH = 256
CONV_WIDTH = 31
MEM_HEADS = 4
MEM_HEAD_DIM = 64
MEM_WIDTH = 256
LAM_INIT = 0.8 - 0.6 * math.exp(-0.3 * 0)

V7X_VMEM_BYTES = 64 * 1024 * 1024
VMEM_LIMIT = V7X_VMEM_BYTES - 8 * 1024 * 1024
LANES = 128
SUBLANES = 8
MXU_COLS = 256

TOKEN_TILE = 512
ATTN_TILE = 512
FFN_CHUNK = 256
CONV_HALO = 32
CONV_ROWS = 64
SOFTMAX_SLAB = 16
SCORE_AHEAD = 2
SCORE_SLOTS = 4
NEG_BIG = -1e30
LOG2_E = math.log2(math.e)


def _rms(x, g):
    return x * lax.rsqrt(jnp.mean(x * x, axis=-1, keepdims=True) + EPS) * g


def _swiglu(xn, wg_ref, wu_ref, wd_ref, act_ref):
    for n in range(wg_ref.shape[0]):
        c = n * FFN_CHUNK
        gate = jnp.dot(xn, wg_ref[n], preferred_element_type=F32)
        up = jnp.dot(xn, wu_ref[n], preferred_element_type=F32)
        act_ref[:, c:c + FFN_CHUNK] = (gate * jax.nn.sigmoid(gate) * up).astype(BF16)
    return jnp.dot(act_ref[...], wd_ref[...], preferred_element_type=F32)


def _stage_weight_slabs(step, wg32_ref, wu32_ref, wd32_ref, wg_ref, wu_ref, wd_ref):
    wg_ref[step] = wg32_ref[...].astype(BF16)
    wu_ref[step] = wu32_ref[...].astype(BF16)
    rows = pl.ds(pl.multiple_of(step * FFN_CHUNK, FFN_CHUNK), FFN_CHUNK)
    wd_ref[rows, :] = wd32_ref[...].astype(BF16)


def _rope(x, cos, sin_signed, first_half):
    partner = jnp.where(first_half,
                        pltpu.roll(x, LANES - DA_HEAD_DIM // 2, axis=1),
                        pltpu.roll(x, DA_HEAD_DIM // 2, axis=1))
    return x * cos + partner * sin_signed


def _pre_mix_kernel(x_ref, g1_ref, wg32_ref, wu32_ref, wd32_ref, gm_ref, win32_ref,
                    cos_ref, sin_ref,
                    h_ref, q_ref, k_ref, vt_ref, glu_ref, qm_ref,
                    wg_ref, wu_ref, wd_ref, win_ref, act_ref):
    step = pl.program_id(0)
    n_stage = wg_ref.shape[0]

    @pl.when(step < n_stage)
    def _():
        _stage_weight_slabs(step, wg32_ref, wu32_ref, wd32_ref, wg_ref, wu_ref, wd_ref)
        win_ref[jnp.minimum(step, win_ref.shape[0] - 1)] = win32_ref[...].astype(BF16)

    @pl.when(step >= n_stage)
    def _():
        _pre_mix_tile(x_ref, g1_ref, wg_ref, wu_ref, wd_ref, gm_ref, win_ref, cos_ref, sin_ref,
                      h_ref, q_ref, k_ref, vt_ref, glu_ref, qm_ref, act_ref)


def _pre_mix_tile(x_ref, g1_ref, wg_ref, wu_ref, wd_ref, gm_ref, win_ref, cos_ref, sin_ref,
                  h_ref, q_ref, k_ref, vt_ref, glu_ref, qm_ref, act_ref):
    x = x_ref[...]
    xn = _rms(x, g1_ref[...]).astype(BF16)
    h = x + 0.5 * _swiglu(xn, wg_ref, wu_ref, wd_ref, act_ref)
    h_ref[...] = h

    n = _rms(h, gm_ref[...]).astype(BF16)
    proj = lambda slab: jnp.dot(n, win_ref[slab], preferred_element_type=F32)
    reps = LANES // cos_ref.shape[1]
    cos = jnp.tile(cos_ref[...], (1, reps))
    sin = jnp.tile(sin_ref[...], (1, reps))
    lane = lax.broadcasted_iota(jnp.int32, cos.shape, 1)
    first_half = (lane % DA_HEAD_DIM) < (DA_HEAD_DIM // 2)
    sin_signed = jnp.where(first_half, -sin, sin)
    scale = DA_HEAD_DIM ** -0.5 * LOG2_E
    q_slabs = DA_WIDTH // MXU_COLS
    for sl in range(q_slabs):
        c0 = sl * MXU_COLS
        qs = proj(sl)
        ks = proj(q_slabs + sl)
        for l0 in range(0, MXU_COLS, LANES):
            q_ref[:, c0 + l0:c0 + l0 + LANES] = (
                _rope(qs[:, l0:l0 + LANES], cos, sin_signed, first_half) * scale).astype(BF16)
            k_ref[:, c0 + l0:c0 + l0 + LANES] = _rope(
                ks[:, l0:l0 + LANES], cos, sin_signed, first_half).astype(BF16)
    vt = jnp.concatenate([proj(2 * q_slabs + sl).T for sl in range(q_slabs)],
                         axis=0).astype(BF16)
    ones = jnp.ones((V_AUG - DA_V_DIM, vt.shape[1]), BF16)
    for hd in range(DA_HEADS):
        vt_ref[0, hd * V_AUG:hd * V_AUG + DA_V_DIM, :] = vt[hd * DA_V_DIM:(hd + 1) * DA_V_DIM]
        vt_ref[0, hd * V_AUG + DA_V_DIM:(hd + 1) * V_AUG, :] = ones
    sl = 3 * q_slabs
    glu_ref[...] = proj(sl) * jax.nn.sigmoid(proj(sl + 1))
    qm_ref[...] = (proj(sl + 2) * MEM_HEAD_DIM ** -0.5).astype(BF16)


def _mem_kv_kernel(mem_ref, g_ref, w_ref, mk_ref, mv_ref):
    n_mem = mem_ref.shape[1]
    mn = _rms(mem_ref[0], g_ref[...]).astype(BF16)
    kv = jnp.dot(mn, w_ref[...], preferred_element_type=F32)
    mk_ref[0] = kv[:, :MEM_WIDTH].astype(BF16)
    mv = kv[:, MEM_WIDTH:]
    head_of_lane = lax.broadcasted_iota(jnp.int32, mv.shape, 1) // MEM_HEAD_DIM
    for hd in range(MEM_HEADS):
        mv_ref[0, hd * n_mem:(hd + 1) * n_mem, :] = jnp.where(
            head_of_lane == hd, mv, 0.0).astype(BF16)


def _diff_attn_kernel(lq1_ref, lk1_ref, lq2_ref, lk2_ref, sg_ref,
                      q_ref, k_ref, vt_ref, o_ref, m_ref, acc_ref, s_ref, mpart_ref):
    qi = pl.program_id(1)
    tq = q_ref.shape[1]
    tk = tq

    lane = lax.broadcasted_iota(jnp.int32, (tq, LANES), 1)
    q_comp = []
    for hd in range(DA_HEADS):
        q = q_ref[0, :, hd * LANES:(hd + 1) * LANES]
        zero = jnp.zeros_like(q)
        q_comp.append(jnp.where(lane < DA_HEAD_DIM, q, zero))
        q_comp.append(jnp.where(lane >= DA_HEAD_DIM, q, zero))

    m_ref[...] = jnp.full(m_ref.shape, NEG_BIG, F32)
    acc_ref[...] = jnp.zeros(acc_ref.shape, F32)

    n_chain = 2 * DA_HEADS
    n_slab = tk // SOFTMAX_SLAB
    rows = lambda r: slice(r * SOFTMAX_SLAB, (r + 1) * SOFTMAX_SLAB)

    def scores(j, i, mask):
        hd = i // 2
        start = pl.multiple_of(j * tk, tk)
        kb = k_ref[0, pl.ds(start, tk), hd * LANES:(hd + 1) * LANES]
        s = lax.dot_general(kb, q_comp[i], (((1,), (1,)), ((), ())),
                            preferred_element_type=F32)
        if mask is not None:
            s = jnp.where(mask, s, NEG_BIG)
        s_ref[i % SCORE_SLOTS] = s
        part = s[rows(0)]
        for r in range(1, n_slab):
            part = jnp.maximum(part, s[rows(r)])
        mpart_ref[i % SCORE_SLOTS] = part

    def softmax_pv(j, i):
        slot = i % SCORE_SLOTS
        hd = i // 2
        vt = vt_ref[0, j, hd * V_AUG:(hd + 1) * V_AUG, :]
        m_old = m_ref[i]
        m_new = jnp.maximum(m_old, jnp.max(mpart_ref[slot], axis=0, keepdims=True))
        alpha = jnp.exp2(m_old - m_new)
        m_ref[i] = m_new
        m_slab = jnp.broadcast_to(m_new, (SOFTMAX_SLAB, tq))
        p = jnp.concatenate(
            [jnp.exp2(s_ref[slot, rows(r), :] - m_slab).astype(BF16) for r in range(n_slab)],
            axis=0)
        acc_ref[i] = alpha * acc_ref[i] + jnp.dot(vt, p, preferred_element_type=F32)

    def block(j, j_next, mask):
        for i in range(n_chain):
            ahead = i + SCORE_AHEAD
            if ahead < n_chain:
                scores(j, ahead, mask)
            else:
                scores(j_next, ahead - n_chain, None)
            softmax_pv(j, i)

    key_chunk = lax.broadcasted_iota(jnp.int32, (tk, tq), 0) // CHUNK
    qry_chunk = lax.broadcasted_iota(jnp.int32, (tk, tq), 1) // CHUNK
    diag_mask = key_chunk <= qry_chunk
    for i in range(SCORE_AHEAD):
        scores(qi, i, diag_mask)
    block(qi, 0, diag_mask)

    def body(j, carry):
        block(j, jnp.minimum(j + 1, qi - 1), None)
        return carry

    lax.fori_loop(0, qi, body, 0)

    lam = (jnp.exp(jnp.sum(lq1_ref[...] * lk1_ref[...], axis=-1, keepdims=True))
           - jnp.exp(jnp.sum(lq2_ref[...] * lk2_ref[...], axis=-1, keepdims=True))
           + LAM_INIT)
    for hd in range(DA_HEADS):
        a1, a2 = acc_ref[2 * hd], acc_ref[2 * hd + 1]
        o = (a1[:DA_V_DIM] / a1[DA_V_DIM:DA_V_DIM + 1]
             - lam * (a2[:DA_V_DIM] / a2[DA_V_DIM:DA_V_DIM + 1]))
        o = o * lax.rsqrt(jnp.mean(o * o, axis=0, keepdims=True) + EPS)
        o = o * sg_ref[...] * (1.0 - LAM_INIT)
        o_ref[0, :, hd * DA_V_DIM:(hd + 1) * DA_V_DIM] = o.T.astype(BF16)


def _post_mix_kernel(h_ref, oda_ref, glu_ref, halo_ref, qm_ref, mk_ref, mv_ref,
                     dww_ref, dwb_ref, lng_ref, lnb_ref, wout32_ref,
                     g2_ref, wg32_ref, wu32_ref, wd32_ref, gf_ref,
                     out_ref, wout_ref, wg_ref, wu_ref, wd_ref,
                     cbuf_ref, shift_ref, yconv_ref, pmem_ref, h2_ref, xn_ref, act_ref,
                     *, tiles_per_seq):
    n_stage = wg_ref.shape[0]
    pid = pl.program_id(0)

    @pl.when(pid < n_stage)
    def _():
        _stage_weight_slabs(pid, wg32_ref, wu32_ref, wd32_ref, wg_ref, wu_ref, wd_ref)
        slab = jnp.minimum(pid, wout_ref.shape[0] // FFN_CHUNK - 1)
        rows = pl.ds(pl.multiple_of(slab * FFN_CHUNK, FFN_CHUNK), FFN_CHUNK)
        wout_ref[rows, :] = wout32_ref[...].astype(BF16)

    @pl.when(pid >= n_stage)
    def _():
        _post_mix_step(pid - n_stage, pl.num_programs(0) - n_stage,
                       h_ref, oda_ref, glu_ref, halo_ref, qm_ref, mk_ref, mv_ref,
                       dww_ref, dwb_ref, lng_ref, lnb_ref, wout_ref,
                       g2_ref, wg_ref, wu_ref, wd_ref, gf_ref,
                       out_ref, cbuf_ref, shift_ref, yconv_ref, pmem_ref, h2_ref, xn_ref,
                       act_ref, tiles_per_seq)


def _post_mix_step(step, n_steps, h_ref, oda_ref, glu_ref, halo_ref, qm_ref, mk_ref, mv_ref,
                   dww_ref, dwb_ref, lng_ref, lnb_ref, wout_ref,
                   g2_ref, wg_ref, wu_ref, wd_ref, gf_ref,
                   out_ref, cbuf_ref, shift_ref, yconv_ref, pmem_ref, h2_ref, xn_ref, act_ref,
                   tiles_per_seq):
    tm = h_ref.shape[0]
    n_mem = mk_ref.shape[1]
    n_slabs = wg_ref.shape[0]
    tile = jnp.minimum(step, n_steps - 2)
    seq_start = (tile % tiles_per_seq) == 0
    base = CONV_HALO - (CONV_WIDTH - 1)

    @pl.when(step == 0)
    def _():
        h2_ref[1] = jnp.zeros(h2_ref.shape[1:], F32)

    def after(c):
        tile_ = act_ref[0:2 * SUBLANES, c * FFN_CHUNK:c * FFN_CHUNK + LANES]
        return jnp.sum(tile_.astype(F32) * 0.0, keepdims=True)

    def before_next_slab(done_bf16):
        sl = (slice(0, 2 * SUBLANES), slice(0, LANES))
        xn_ref[sl] = xn_ref[sl] + done_bf16 * jnp.zeros_like(done_bf16)

    def conv_setup(c):
        halo = halo_ref[...]
        cbuf_ref[0:CONV_HALO, :] = jnp.where(seq_start, jnp.zeros_like(halo), halo)
        cbuf_ref[CONV_HALO:CONV_HALO + tm, :] = glu_ref[...]
        span = tm + CONV_HALO - SUBLANES
        for r in range(1, SUBLANES):
            shift_ref[r - 1] = cbuf_ref[r:r + span, :]
        before_next_slab(shift_ref[SUBLANES - 2, 0:2 * SUBLANES, 0:LANES].astype(BF16))

    def conv_rows(r0, wait):
        y = jnp.zeros((CONV_ROWS, CONV_CH), F32) + wait
        for j in range(CONV_WIDTH):
            phase = (base + j) % SUBLANES
            start = base + j - phase + r0
            if phase == 0:
                window = cbuf_ref[start:start + CONV_ROWS, :]
            else:
                window = shift_ref[phase - 1, start:start + CONV_ROWS, :]
            y = y + dww_ref[j:j + 1, :] * window
        y = y + dwb_ref[...]
        mu = jnp.mean(y, axis=-1, keepdims=True)
        yc = y - mu
        var = jnp.mean(yc * yc, axis=-1, keepdims=True)
        y = yc * lax.rsqrt(var + EPS) * lng_ref[...] + lnb_ref[...]
        piece = (y * jax.nn.sigmoid(y)).astype(BF16)
        yconv_ref[r0:r0 + CONV_ROWS, :] = piece
        return piece

    def conv_pieces(r0s, c):
        wait = after(c - 1)
        for r0 in r0s:
            piece = conv_rows(r0, wait)
        before_next_slab(piece[0:2 * SUBLANES, 0:LANES])

    def mem_heads(heads, c):
        qm = qm_ref[...]
        head_of_lane = lax.broadcasted_iota(jnp.int32, qm.shape, 1) // MEM_HEAD_DIM
        wait = after(c - 1)
        for hd in heads:
            qh = jnp.where(head_of_lane == hd, qm, jnp.zeros_like(qm))
            sc = lax.dot_general(qh, mk_ref[0], (((1,), (1,)), ((), ())),
                                 preferred_element_type=F32)
            sc = sc - (jnp.max(sc, axis=-1, keepdims=True) + wait)
            p = jnp.exp(sc)
            p = (p / jnp.sum(p, axis=-1, keepdims=True)).astype(BF16)
            pmem_ref[:, hd * n_mem:(hd + 1) * n_mem] = p
        before_next_slab(p[0:2 * SUBLANES, 0:LANES])

    pieces = [conv_setup]
    pieces += [functools.partial(conv_pieces, (r0,)) for r0 in range(0, tm, CONV_ROWS)]
    pieces += [functools.partial(mem_heads, (0, 1)), functools.partial(mem_heads, (2, 3))]
    assert len(pieces) <= n_slabs, "at most one mixing piece per SwiGLU slab"
    pieces += [None] * (n_slabs - len(pieces))

    h_prev = h2_ref[(step + 1) % 2]
    xn_ref[...] = _rms(h_prev, g2_ref[...]).astype(BF16)

    h2_ref[step % 2] = h_ref[...] + jnp.dot(oda_ref[...], wout_ref[0:DA_WIDTH, :],
                                            preferred_element_type=F32)
    for n in range(n_slabs):
        c0 = n * FFN_CHUNK
        xn = xn_ref[...]
        gate = jnp.dot(xn, wg_ref[n], preferred_element_type=F32)
        up = jnp.dot(xn, wu_ref[n], preferred_element_type=F32)
        if pieces[n] is not None:
            pieces[n](n)
        act_ref[:, c0:c0 + FFN_CHUNK] = (gate * jax.nn.sigmoid(gate) * up).astype(BF16)
    ffn = jnp.dot(act_ref[...], wd_ref[...], preferred_element_type=F32)

    o_mem = jnp.dot(pmem_ref[...], mv_ref[0], preferred_element_type=F32).astype(BF16)
    mix = (jnp.dot(yconv_ref[...], wout_ref[DA_WIDTH:DA_WIDTH + CONV_CH, :],
                   preferred_element_type=F32)
           + jnp.dot(o_mem, wout_ref[DA_WIDTH + CONV_CH:, :], preferred_element_type=F32))
    h2_ref[step % 2] = h2_ref[step % 2] + mix
    out_ref[...] = _rms(h_prev + 0.5 * ffn, gf_ref[...])


def _resident(shape):
    return pl.BlockSpec(shape, lambda *_: (0,) * len(shape), pipeline_mode=pl.Buffered(1))


def _rope_tables(seq):
    half = DA_HEAD_DIM // 2
    inv_freq = 1.0 / (ROPE_THETA ** (jnp.arange(0, DA_HEAD_DIM, 2, dtype=F32) / DA_HEAD_DIM))
    ang = jnp.arange(seq, dtype=F32)[:, None] * inv_freq[None, :]
    assert ang.shape == (seq, half)
    return jnp.cos(ang), jnp.sin(ang)


def kernel(x, mem, ffn1_norm_g, ffn1_w_gate, ffn1_w_up, ffn1_w_down, mix_norm_g, mem_norm_g, w_in, lambda_q1, lambda_k1, lambda_q2, lambda_k2, subln_g, conv_dw_w, conv_dw_b, conv_ln_g, conv_ln_b, w_mem_kv, w_out, ffn2_norm_g, ffn2_w_gate, ffn2_w_up, ffn2_w_down, final_norm_g):
    b, s, d = x.shape
    n_mem = mem.shape[1]
    d_ff = ffn1_w_gate.shape[-1]
    in_width = w_in.shape[-1]
    assert ffn1_norm_g.shape[0] == 1, "single layer"
    tm = TOKEN_TILE
    ta = ATTN_TILE
    assert s % tm == 0 and s % ta == 0 and tm == ta and d_ff % FFN_CHUNK == 0
    t = b * s
    n_tiles = s // tm
    params = functools.partial(pltpu.CompilerParams, vmem_limit_bytes=VMEM_LIMIT)

    cos_t, sin_t = _rope_tables(s)
    row = lambda a: a.reshape(1, -1)
    bf = lambda a: a[0].astype(BF16)

    n_stage = d_ff // FFN_CHUNK
    in_slabs = in_width // MXU_COLS
    assert FFN_CHUNK == MXU_COLS and in_slabs <= n_stage
    tile_of = lambda i: jnp.maximum(i - n_stage, 0)
    slab_of = lambda i: jnp.minimum(i, n_stage - 1)
    tok = lambda w: pl.BlockSpec((tm, w), lambda i: (tile_of(i), 0))
    col_slab = pl.BlockSpec((d, FFN_CHUNK), lambda i: (0, slab_of(i)))
    h1, q, k, vt, glu, qm = pl.pallas_call(
        _pre_mix_kernel,
        grid=(n_stage + t // tm,),
        in_specs=[tok(d), _resident((1, d)), col_slab, col_slab,
                  pl.BlockSpec((FFN_CHUNK, d), lambda i: (slab_of(i), 0)), _resident((1, d)),
                  pl.BlockSpec((d, MXU_COLS), lambda i: (0, jnp.minimum(i, in_slabs - 1))),
                  pl.BlockSpec((tm, DA_HEAD_DIM // 2), lambda i: (tile_of(i) % n_tiles, 0)),
                  pl.BlockSpec((tm, DA_HEAD_DIM // 2), lambda i: (tile_of(i) % n_tiles, 0))],
        out_specs=[tok(d), tok(DA_WIDTH), tok(DA_WIDTH),
                   pl.BlockSpec((1, DA_HEADS * V_AUG, tm), lambda i: (tile_of(i), 0, 0)),
                   tok(CONV_CH), tok(MEM_WIDTH)],
        out_shape=[jax.ShapeDtypeStruct((t, d), F32),
                   jax.ShapeDtypeStruct((t, DA_WIDTH), BF16),
                   jax.ShapeDtypeStruct((t, DA_WIDTH), BF16),
                   jax.ShapeDtypeStruct((t // tm, DA_HEADS * V_AUG, tm), BF16),
                   jax.ShapeDtypeStruct((t, CONV_CH), F32),
                   jax.ShapeDtypeStruct((t, MEM_WIDTH), BF16)],
        scratch_shapes=[pltpu.VMEM((n_stage, d, FFN_CHUNK), BF16),
                        pltpu.VMEM((n_stage, d, FFN_CHUNK), BF16),
                        pltpu.VMEM((d_ff, d), BF16),
                        pltpu.VMEM((in_slabs, d, MXU_COLS), BF16),
                        pltpu.VMEM((tm, d_ff), BF16)],
        compiler_params=params(dimension_semantics=("arbitrary",)),
        name="pre_mix",
    )(x.reshape(t, d), row(ffn1_norm_g), ffn1_w_gate.reshape(d, d_ff),
      ffn1_w_up.reshape(d, d_ff), ffn1_w_down.reshape(d_ff, d), row(mix_norm_g),
      w_in.reshape(d, in_width), cos_t, sin_t)

    mk, mv = pl.pallas_call(
        _mem_kv_kernel,
        grid=(b,),
        in_specs=[pl.BlockSpec((1, n_mem, d), lambda i: (i, 0, 0)), _resident((1, d)),
                  _resident((d, 2 * MEM_WIDTH))],
        out_specs=[pl.BlockSpec((1, n_mem, MEM_WIDTH), lambda i: (i, 0, 0)),
                   pl.BlockSpec((1, MEM_HEADS * n_mem, MEM_WIDTH), lambda i: (i, 0, 0))],
        out_shape=[jax.ShapeDtypeStruct((b, n_mem, MEM_WIDTH), BF16),
                   jax.ShapeDtypeStruct((b, MEM_HEADS * n_mem, MEM_WIDTH), BF16)],
        compiler_params=params(dimension_semantics=("parallel",)),
        name="mem_kv",
    )(mem, row(mem_norm_g), bf(w_mem_kv))

    lam_spec = _resident((1, DA_HEAD_DIM))
    o_da = pl.pallas_call(
        _diff_attn_kernel,
        grid=(b, s // ta),
        in_specs=[lam_spec, lam_spec, lam_spec, lam_spec, _resident((DA_V_DIM, 1)),
                  pl.BlockSpec((1, ta, DA_WIDTH), lambda bi, qi: (bi, qi, 0)),
                  pl.BlockSpec((1, s, DA_WIDTH), lambda bi, qi: (bi, 0, 0)),
                  pl.BlockSpec((1, s // ta, DA_HEADS * V_AUG, ta),
                               lambda bi, qi: (bi, 0, 0, 0))],
        out_specs=pl.BlockSpec((1, ta, DA_WIDTH), lambda bi, qi: (bi, qi, 0)),
        out_shape=jax.ShapeDtypeStruct((b, s, DA_WIDTH), BF16),
        scratch_shapes=[pltpu.VMEM((2 * DA_HEADS, 1, ta), F32),
                        pltpu.VMEM((2 * DA_HEADS, V_AUG, ta), F32),
                        pltpu.VMEM((SCORE_SLOTS, ta, ta), F32),
                        pltpu.VMEM((SCORE_SLOTS, SOFTMAX_SLAB, ta), F32)],
        compiler_params=params(dimension_semantics=("parallel", "arbitrary")),
        name="diff_attn",
    )(lambda_q1, lambda_k1, lambda_q2, lambda_k2, subln_g.reshape(DA_V_DIM, 1),
      q.reshape(b, s, DA_WIDTH), k.reshape(b, s, DA_WIDTH),
      vt.reshape(b, s // ta, DA_HEADS * V_AUG, ta))

    halo_blocks = tm // CONV_HALO
    n_steps = t // tm + 1
    last = t // tm - 1
    out_slabs = d // FFN_CHUNK
    assert out_slabs <= n_stage
    mixed_tile = lambda i: jnp.clip(i - n_stage, 0, last)
    mixed = lambda w: pl.BlockSpec((tm, w), lambda i: (mixed_tile(i), 0))
    per_seq = lambda rows: pl.BlockSpec((1, rows, MEM_WIDTH),
                                        lambda i: (mixed_tile(i) // n_tiles, 0, 0))
    out = pl.pallas_call(
        functools.partial(_post_mix_kernel, tiles_per_seq=n_tiles),
        grid=(n_stage + n_steps,),
        in_specs=[mixed(d), mixed(DA_WIDTH), mixed(CONV_CH),
                  pl.BlockSpec((CONV_HALO, CONV_CH),
                               lambda i: (jnp.maximum(mixed_tile(i) * halo_blocks - 1, 0), 0)),
                  mixed(MEM_WIDTH), per_seq(n_mem), per_seq(MEM_HEADS * n_mem),
                  _resident((CONV_WIDTH, CONV_CH)), _resident((1, CONV_CH)),
                  _resident((1, CONV_CH)), _resident((1, CONV_CH)),
                  pl.BlockSpec((FFN_CHUNK, d), lambda i: (jnp.minimum(i, out_slabs - 1), 0)),
                  _resident((1, d)), col_slab, col_slab,
                  pl.BlockSpec((FFN_CHUNK, d), lambda i: (slab_of(i), 0)), _resident((1, d))],
        out_specs=pl.BlockSpec((tm, d), lambda i: (jnp.maximum(i - n_stage - 1, 0), 0)),
        out_shape=jax.ShapeDtypeStruct((t, d), F32),
        scratch_shapes=[pltpu.VMEM((d, d), BF16),
                        pltpu.VMEM((n_stage, d, FFN_CHUNK), BF16),
                        pltpu.VMEM((n_stage, d, FFN_CHUNK), BF16),
                        pltpu.VMEM((d_ff, d), BF16),
                        pltpu.VMEM((CONV_HALO + tm, CONV_CH), F32),
                        pltpu.VMEM((SUBLANES - 1, tm + CONV_HALO - SUBLANES, CONV_CH), F32),
                        pltpu.VMEM((tm, CONV_CH), BF16),
                        pltpu.VMEM((tm, MEM_HEADS * n_mem), BF16),
                        pltpu.VMEM((2, tm, d), F32),
                        pltpu.VMEM((tm, d), BF16),
                        pltpu.VMEM((tm, d_ff), BF16)],
        compiler_params=params(dimension_semantics=("arbitrary",)),
        name="post_mix",
    )(h1, o_da.reshape(t, DA_WIDTH), glu, glu, qm, mk, mv, conv_dw_w[0], conv_dw_b,
      conv_ln_g, conv_ln_b, w_out.reshape(d, d), row(ffn2_norm_g),
      ffn2_w_gate.reshape(d, d_ff), ffn2_w_up.reshape(d, d_ff),
      ffn2_w_down.reshape(d_ff, d), row(final_norm_g))
    return out.reshape(b, s, d)
```

```python
import functools
import math

import jax
import jax.numpy as jnp
from jax import lax
from jax.experimental import pallas as pl
from jax.experimental.pallas import tpu as pltpu

F32 = jnp.float32
BF16 = jnp.bfloat16

EPS = 1e-5
ROPE_THETA = 10000.0
CHUNK = 64
DA_HEADS = 4
DA_HEAD_DIM = 64
DA_V_DIM = 128
V_AUG = DA_V_DIM + 16
DA_WIDTH = 512
CONV_CH = 256
CONV_WIDTH = 31
MEM_HEADS = 4
MEM_HEAD_DIM = 64
MEM_WIDTH = 256
LAM_INIT = 0.8 - 0.6 * math.exp(-0.3 * 0)

V7X_VMEM_BYTES = 64 * 1024 * 1024
VMEM_LIMIT = V7X_VMEM_BYTES - 8 * 1024 * 1024
LANES = 128
SUBLANES = 8
MXU_COLS = 256

TOKEN_TILE = 512
ATTN_TILE = 512
FFN_CHUNK = 256
CONV_HALO = 32
CONV_ROWS = 64
SOFTMAX_SLAB = 16
SCORE_AHEAD = 2
SCORE_SLOTS = 4
NEG_BIG = -1e30
LOG2_E = math.log2(math.e)


def _rms(x, g):
    return x * lax.rsqrt(jnp.mean(x * x, axis=-1, keepdims=True) + EPS) * g


def _swiglu(xn, wg_ref, wu_ref, wd_ref, act_ref):
    for n in range(wg_ref.shape[0]):
        c = n * FFN_CHUNK
        gate = jnp.dot(xn, wg_ref[n], preferred_element_type=F32)
        up = jnp.dot(xn, wu_ref[n], preferred_element_type=F32)
        act_ref[:, c:c + FFN_CHUNK] = (gate * jax.nn.sigmoid(gate) * up).astype(BF16)
    return jnp.dot(act_ref[...], wd_ref[...], preferred_element_type=F32)


def _stage_weight_slabs(step, wg32_ref, wu32_ref, wd32_ref, wg_ref, wu_ref, wd_ref):
    wg_ref[step] = wg32_ref[...].astype(BF16)
    wu_ref[step] = wu32_ref[...].astype(BF16)
    rows = pl.ds(pl.multiple_of(step * FFN_CHUNK, FFN_CHUNK), FFN_CHUNK)
    wd_ref[rows, :] = wd32_ref[...].astype(BF16)


def _rope(x, cos, sin_signed, first_half):
    partner = jnp.where(first_half,
                        pltpu.roll(x, LANES - DA_HEAD_DIM // 2, axis=1),
                        pltpu.roll(x, DA_HEAD_DIM // 2, axis=1))
    return x * cos + partner * sin_signed


def _pre_mix_kernel(x_ref, g1_ref, wg32_ref, wu32_ref, wd32_ref, gm_ref, win32_ref,
                    cos_ref, sin_ref,
                    h_ref, q_ref, k_ref, vt_ref, glu_ref, qm_ref,
                    wg_ref, wu_ref, wd_ref, win_ref, act_ref):
    step = pl.program_id(0)
    n_stage = wg_ref.shape[0]

    @pl.when(step < n_stage)
    def _():
        _stage_weight_slabs(step, wg32_ref, wu32_ref, wd32_ref, wg_ref, wu_ref, wd_ref)
        win_ref[jnp.minimum(step, win_ref.shape[0] - 1)] = win32_ref[...].astype(BF16)

    @pl.when(step >= n_stage)
    def _():
        _pre_mix_tile(x_ref, g1_ref, wg_ref, wu_ref, wd_ref, gm_ref, win_ref, cos_ref, sin_ref,
                      h_ref, q_ref, k_ref, vt_ref, glu_ref, qm_ref, act_ref)


def _pre_mix_tile(x_ref, g1_ref, wg_ref, wu_ref, wd_ref, gm_ref, win_ref, cos_ref, sin_ref,
                  h_ref, q_ref, k_ref, vt_ref, glu_ref, qm_ref, act_ref):
    x = x_ref[...]
    xn = _rms(x, g1_ref[...]).astype(BF16)
    h = x + 0.5 * _swiglu(xn, wg_ref, wu_ref, wd_ref, act_ref)
    h_ref[...] = h

    n = _rms(h, gm_ref[...]).astype(BF16)
    proj = lambda slab: jnp.dot(n, win_ref[slab], preferred_element_type=F32)
    reps = LANES // cos_ref.shape[0]
    cos = jnp.tile(cos_ref[...].T, (1, reps))
    sin = jnp.tile(sin_ref[...].T, (1, reps))
    lane = lax.broadcasted_iota(jnp.int32, cos.shape, 1)
    first_half = (lane % DA_HEAD_DIM) < (DA_HEAD_DIM // 2)
    sin_signed = jnp.where(first_half, -sin, sin)
    scale = DA_HEAD_DIM ** -0.5 * LOG2_E
    q_slabs = DA_WIDTH // MXU_COLS
    for sl in range(q_slabs):
        c0 = sl * MXU_COLS
        qs = proj(sl)
        ks = proj(q_slabs + sl)
        for l0 in range(0, MXU_COLS, LANES):
            q_ref[:, c0 + l0:c0 + l0 + LANES] = (
                _rope(qs[:, l0:l0 + LANES], cos, sin_signed, first_half) * scale).astype(BF16)
            k_ref[:, c0 + l0:c0 + l0 + LANES] = _rope(
                ks[:, l0:l0 + LANES], cos, sin_signed, first_half).astype(BF16)
    vt = jnp.concatenate([proj(2 * q_slabs + sl).T for sl in range(q_slabs)],
                         axis=0).astype(BF16)
    ones = jnp.ones((V_AUG - DA_V_DIM, vt.shape[1]), BF16)
    for hd in range(DA_HEADS):
        vt_ref[0, hd * V_AUG:hd * V_AUG + DA_V_DIM, :] = vt[hd * DA_V_DIM:(hd + 1) * DA_V_DIM]
        vt_ref[0, hd * V_AUG + DA_V_DIM:(hd + 1) * V_AUG, :] = ones
    sl = 3 * q_slabs
    glu_ref[...] = proj(sl) * jax.nn.sigmoid(proj(sl + 1))
    qm_ref[...] = (proj(sl + 2) * MEM_HEAD_DIM ** -0.5).astype(BF16)


def _mem_kv_kernel(mem_ref, g_ref, w_ref, mk_ref, mv_ref):
    n_mem = mem_ref.shape[1]
    mn = _rms(mem_ref[0], g_ref[...]).astype(BF16)
    kv = jnp.dot(mn, w_ref[...].astype(BF16), preferred_element_type=F32)
    mk_ref[0] = kv[:, :MEM_WIDTH].astype(BF16)
    mv = kv[:, MEM_WIDTH:]
    head_of_lane = lax.broadcasted_iota(jnp.int32, mv.shape, 1) // MEM_HEAD_DIM
    for hd in range(MEM_HEADS):
        mv_ref[0, hd * n_mem:(hd + 1) * n_mem, :] = jnp.where(
            head_of_lane == hd, mv, 0.0).astype(BF16)


def _diff_attn_kernel(lq1_ref, lk1_ref, lq2_ref, lk2_ref, sg_ref,
                      q_ref, k_ref, vt_ref, o_ref, m_ref, acc_ref, s_ref, mpart_ref):
    qi = pl.program_id(1)
    tq = q_ref.shape[1]
    tk = tq

    lane = lax.broadcasted_iota(jnp.int32, (tq, LANES), 1)
    q_comp = []
    for hd in range(DA_HEADS):
        q = q_ref[0, :, hd * LANES:(hd + 1) * LANES]
        zero = jnp.zeros_like(q)
        q_comp.append(jnp.where(lane < DA_HEAD_DIM, q, zero))
        q_comp.append(jnp.where(lane >= DA_HEAD_DIM, q, zero))

    m_ref[...] = jnp.full(m_ref.shape, NEG_BIG, F32)
    acc_ref[...] = jnp.zeros(acc_ref.shape, F32)

    n_chain = 2 * DA_HEADS
    n_slab = tk // SOFTMAX_SLAB
    rows = lambda r: slice(r * SOFTMAX_SLAB, (r + 1) * SOFTMAX_SLAB)

    def scores(j, i, mask):
        hd = i // 2
        start = pl.multiple_of(j * tk, tk)
        kb = k_ref[0, pl.ds(start, tk), hd * LANES:(hd + 1) * LANES]
        s = lax.dot_general(kb, q_comp[i], (((1,), (1,)), ((), ())),
                            preferred_element_type=F32)
        if mask is not None:
            s = jnp.where(mask, s, NEG_BIG)
        s_ref[i % SCORE_SLOTS] = s
        part = s[rows(0)]
        for r in range(1, n_slab):
            part = jnp.maximum(part, s[rows(r)])
        mpart_ref[i % SCORE_SLOTS] = part

    def softmax_pv(j, i):
        slot = i % SCORE_SLOTS
        hd = i // 2
        vt = vt_ref[0, j, hd * V_AUG:(hd + 1) * V_AUG, :]
        m_old = m_ref[i]
        m_new = jnp.maximum(m_old, jnp.max(mpart_ref[slot], axis=0, keepdims=True))
        alpha = jnp.exp2(m_old - m_new)
        m_ref[i] = m_new
        m_slab = jnp.broadcast_to(m_new, (SOFTMAX_SLAB, tq))
        p = jnp.concatenate(
            [jnp.exp2(s_ref[slot, rows(r), :] - m_slab).astype(BF16) for r in range(n_slab)],
            axis=0)
        acc_ref[i] = alpha * acc_ref[i] + jnp.dot(vt, p, preferred_element_type=F32)

    def block(j, j_next, mask):
        for i in range(n_chain):
            ahead = i + SCORE_AHEAD
            if ahead < n_chain:
                scores(j, ahead, mask)
            else:
                scores(j_next, ahead - n_chain, None)
            softmax_pv(j, i)

    key_chunk = lax.broadcasted_iota(jnp.int32, (tk, tq), 0) // CHUNK
    qry_chunk = lax.broadcasted_iota(jnp.int32, (tk, tq), 1) // CHUNK
    diag_mask = key_chunk <= qry_chunk
    for i in range(SCORE_AHEAD):
        scores(qi, i, diag_mask)
    block(qi, 0, diag_mask)

    def body(j, carry):
        block(j, jnp.minimum(j + 1, qi - 1), None)
        return carry

    lax.fori_loop(0, qi, body, 0)

    lam = (jnp.exp(jnp.sum(lq1_ref[...] * lk1_ref[...], axis=-1, keepdims=True))
           - jnp.exp(jnp.sum(lq2_ref[...] * lk2_ref[...], axis=-1, keepdims=True))
           + LAM_INIT)
    for hd in range(DA_HEADS):
        a1, a2 = acc_ref[2 * hd], acc_ref[2 * hd + 1]
        o = (a1[:DA_V_DIM] / a1[DA_V_DIM:DA_V_DIM + 1]
             - lam * (a2[:DA_V_DIM] / a2[DA_V_DIM:DA_V_DIM + 1]))
        o = o * lax.rsqrt(jnp.mean(o * o, axis=0, keepdims=True) + EPS)
        o = o * sg_ref[...] * (1.0 - LAM_INIT)
        o_ref[0, :, hd * DA_V_DIM:(hd + 1) * DA_V_DIM] = o.T.astype(BF16)


def _post_mix_kernel(h_ref, oda_ref, glu_ref, halo_ref, qm_ref, mk_ref, mv_ref,
                     dww_ref, dwb_ref, lng_ref, lnb_ref, wout32_ref,
                     g2_ref, wg32_ref, wu32_ref, wd32_ref, gf_ref,
                     out_ref, wout_ref, wg_ref, wu_ref, wd_ref,
                     cbuf_ref, shift_ref, yconv_ref, pmem_ref, h2_ref, xn_ref, act_ref,
                     *, tiles_per_seq):
    n_stage = wg_ref.shape[0]
    pid = pl.program_id(0)

    @pl.when(pid < n_stage)
    def _():
        _stage_weight_slabs(pid, wg32_ref, wu32_ref, wd32_ref, wg_ref, wu_ref, wd_ref)
        slab = jnp.minimum(pid, wout_ref.shape[0] // FFN_CHUNK - 1)
        rows = pl.ds(pl.multiple_of(slab * FFN_CHUNK, FFN_CHUNK), FFN_CHUNK)
        wout_ref[rows, :] = wout32_ref[...].astype(BF16)

    @pl.when(pid >= n_stage)
    def _():
        _post_mix_step(pid - n_stage, pl.num_programs(0) - n_stage,
                       h_ref, oda_ref, glu_ref, halo_ref, qm_ref, mk_ref, mv_ref,
                       dww_ref, dwb_ref, lng_ref, lnb_ref, wout_ref,
                       g2_ref, wg_ref, wu_ref, wd_ref, gf_ref,
                       out_ref, cbuf_ref, shift_ref, yconv_ref, pmem_ref, h2_ref, xn_ref,
                       act_ref, tiles_per_seq)


def _post_mix_step(step, n_steps, h_ref, oda_ref, glu_ref, halo_ref, qm_ref, mk_ref, mv_ref,
                   dww_ref, dwb_ref, lng_ref, lnb_ref, wout_ref,
                   g2_ref, wg_ref, wu_ref, wd_ref, gf_ref,
                   out_ref, cbuf_ref, shift_ref, yconv_ref, pmem_ref, h2_ref, xn_ref, act_ref,
                   tiles_per_seq):
    tm = h_ref.shape[0]
    n_mem = mk_ref.shape[1]
    n_slabs = wg_ref.shape[0]
    tile = jnp.minimum(step, n_steps - 2)
    seq_start = (tile % tiles_per_seq) == 0
    base = CONV_HALO - (CONV_WIDTH - 1)

    @pl.when(step == 0)
    def _():
        h2_ref[1] = jnp.zeros(h2_ref.shape[1:], F32)

    def after(c):
        tile_ = act_ref[0:2 * SUBLANES, c * FFN_CHUNK:c * FFN_CHUNK + LANES]
        return jnp.sum(tile_.astype(F32) * 0.0, keepdims=True)

    def before_next_slab(done_bf16):
        sl = (slice(0, 2 * SUBLANES), slice(0, LANES))
        xn_ref[sl] = xn_ref[sl] + done_bf16 * jnp.zeros_like(done_bf16)

    def conv_setup(c):
        halo = halo_ref[...]
        cbuf_ref[0:CONV_HALO, :] = jnp.where(seq_start, jnp.zeros_like(halo), halo)
        cbuf_ref[CONV_HALO:CONV_HALO + tm, :] = glu_ref[...]
        span = tm + CONV_HALO - SUBLANES
        for r in range(1, SUBLANES):
            shift_ref[r - 1] = cbuf_ref[r:r + span, :]
        before_next_slab(shift_ref[SUBLANES - 2, 0:2 * SUBLANES, 0:LANES].astype(BF16))

    def conv_rows(r0, wait):
        y = jnp.zeros((CONV_ROWS, CONV_CH), F32) + wait
        for j in range(CONV_WIDTH):
            phase = (base + j) % SUBLANES
            start = base + j - phase + r0
            if phase == 0:
                window = cbuf_ref[start:start + CONV_ROWS, :]
            else:
                window = shift_ref[phase - 1, start:start + CONV_ROWS, :]
            y = y + dww_ref[j:j + 1, :] * window
        y = y + dwb_ref[...]
        mu = jnp.mean(y, axis=-1, keepdims=True)
        yc = y - mu
        var = jnp.mean(yc * yc, axis=-1, keepdims=True)
        y = yc * lax.rsqrt(var + EPS) * lng_ref[...] + lnb_ref[...]
        piece = (y * jax.nn.sigmoid(y)).astype(BF16)
        yconv_ref[r0:r0 + CONV_ROWS, :] = piece
        return piece

    def conv_pieces(r0s, c):
        wait = after(c - 1)
        for r0 in r0s:
            piece = conv_rows(r0, wait)
        before_next_slab(piece[0:2 * SUBLANES, 0:LANES])

    def mem_heads(heads, c):
        qm = qm_ref[...]
        head_of_lane = lax.broadcasted_iota(jnp.int32, qm.shape, 1) // MEM_HEAD_DIM
        wait = after(c - 1)
        for hd in heads:
            qh = jnp.where(head_of_lane == hd, qm, jnp.zeros_like(qm))
            sc = lax.dot_general(qh, mk_ref[0], (((1,), (1,)), ((), ())),
                                 preferred_element_type=F32)
            sc = sc - (jnp.max(sc, axis=-1, keepdims=True) + wait)
            p = jnp.exp(sc)
            p = (p / jnp.sum(p, axis=-1, keepdims=True)).astype(BF16)
            pmem_ref[:, hd * n_mem:(hd + 1) * n_mem] = p
        before_next_slab(p[0:2 * SUBLANES, 0:LANES])

    pieces = [conv_setup]
    pieces += [functools.partial(conv_pieces, (r0,)) for r0 in range(0, tm, CONV_ROWS)]
    pieces += [functools.partial(mem_heads, (0, 1)), functools.partial(mem_heads, (2, 3))]
    assert len(pieces) <= n_slabs, "at most one mixing piece per SwiGLU slab"
    pieces += [None] * (n_slabs - len(pieces))

    h_prev = h2_ref[(step + 1) % 2]
    xn_ref[...] = _rms(h_prev, g2_ref[...]).astype(BF16)

    h2_ref[step % 2] = h_ref[...] + jnp.dot(oda_ref[...], wout_ref[0:DA_WIDTH, :],
                                            preferred_element_type=F32)
    for n in range(n_slabs):
        c0 = n * FFN_CHUNK
        xn = xn_ref[...]
        gate = jnp.dot(xn, wg_ref[n], preferred_element_type=F32)
        up = jnp.dot(xn, wu_ref[n], preferred_element_type=F32)
        if pieces[n] is not None:
            pieces[n](n)
        act_ref[:, c0:c0 + FFN_CHUNK] = (gate * jax.nn.sigmoid(gate) * up).astype(BF16)
    ffn = jnp.dot(act_ref[...], wd_ref[...], preferred_element_type=F32)

    o_mem = jnp.dot(pmem_ref[...], mv_ref[0], preferred_element_type=F32).astype(BF16)
    mix = (jnp.dot(yconv_ref[...], wout_ref[DA_WIDTH:DA_WIDTH + CONV_CH, :],
                   preferred_element_type=F32)
           + jnp.dot(o_mem, wout_ref[DA_WIDTH + CONV_CH:, :], preferred_element_type=F32))
    h2_ref[step % 2] = h2_ref[step % 2] + mix
    out_ref[...] = _rms(h_prev + 0.5 * ffn, gf_ref[...])


def _resident(shape):
    return pl.BlockSpec(shape, lambda *_: (0,) * len(shape), pipeline_mode=pl.Buffered(1))


def _rope_tables(seq):
    half = DA_HEAD_DIM // 2
    inv_freq = 1.0 / (ROPE_THETA ** (jnp.arange(0, DA_HEAD_DIM, 2, dtype=F32) / DA_HEAD_DIM))
    ang = inv_freq[:, None] * jnp.arange(seq, dtype=F32)[None, :]
    assert ang.shape == (half, seq)
    return jnp.cos(ang), jnp.sin(ang)


def kernel(x, mem, ffn1_norm_g, ffn1_w_gate, ffn1_w_up, ffn1_w_down, mix_norm_g, mem_norm_g, w_in, lambda_q1, lambda_k1, lambda_q2, lambda_k2, subln_g, conv_dw_w, conv_dw_b, conv_ln_g, conv_ln_b, w_mem_kv, w_out, ffn2_norm_g, ffn2_w_gate, ffn2_w_up, ffn2_w_down, final_norm_g):
    b, s, d = x.shape
    n_mem = mem.shape[1]
    d_ff = ffn1_w_gate.shape[-1]
    in_width = w_in.shape[-1]
    assert ffn1_norm_g.shape[0] == 1, "single layer"
    tm = TOKEN_TILE
    ta = ATTN_TILE
    assert s % tm == 0 and s % ta == 0 and tm == ta and d_ff % FFN_CHUNK == 0
    t = b * s
    n_tiles = s // tm
    params = functools.partial(pltpu.CompilerParams, vmem_limit_bytes=VMEM_LIMIT)

    cos_t, sin_t = _rope_tables(s)
    row = lambda a: a.reshape(1, -1)

    n_stage = d_ff // FFN_CHUNK
    in_slabs = in_width // MXU_COLS
    assert FFN_CHUNK == MXU_COLS and in_slabs <= n_stage
    tile_of = lambda i: jnp.maximum(i - n_stage, 0)
    slab_of = lambda i: jnp.minimum(i, n_stage - 1)
    tok = lambda w: pl.BlockSpec((tm, w), lambda i: (tile_of(i), 0))
    col_slab = pl.BlockSpec((d, FFN_CHUNK), lambda i: (0, slab_of(i)))
    h1, q, k, vt, glu, qm = pl.pallas_call(
        _pre_mix_kernel,
        grid=(n_stage + t // tm,),
        in_specs=[tok(d), _resident((1, d)), col_slab, col_slab,
                  pl.BlockSpec((FFN_CHUNK, d), lambda i: (slab_of(i), 0)), _resident((1, d)),
                  pl.BlockSpec((d, MXU_COLS), lambda i: (0, jnp.minimum(i, in_slabs - 1))),
                  pl.BlockSpec((DA_HEAD_DIM // 2, tm), lambda i: (0, tile_of(i) % n_tiles)),
                  pl.BlockSpec((DA_HEAD_DIM // 2, tm), lambda i: (0, tile_of(i) % n_tiles))],
        out_specs=[tok(d), tok(DA_WIDTH), tok(DA_WIDTH),
                   pl.BlockSpec((1, DA_HEADS * V_AUG, tm), lambda i: (tile_of(i), 0, 0)),
                   tok(CONV_CH), tok(MEM_WIDTH)],
        out_shape=[jax.ShapeDtypeStruct((t, d), F32),
                   jax.ShapeDtypeStruct((t, DA_WIDTH), BF16),
                   jax.ShapeDtypeStruct((t, DA_WIDTH), BF16),
                   jax.ShapeDtypeStruct((t // tm, DA_HEADS * V_AUG, tm), BF16),
                   jax.ShapeDtypeStruct((t, CONV_CH), F32),
                   jax.ShapeDtypeStruct((t, MEM_WIDTH), BF16)],
        scratch_shapes=[pltpu.VMEM((n_stage, d, FFN_CHUNK), BF16),
                        pltpu.VMEM((n_stage, d, FFN_CHUNK), BF16),
                        pltpu.VMEM((d_ff, d), BF16),
                        pltpu.VMEM((in_slabs, d, MXU_COLS), BF16),
                        pltpu.VMEM((tm, d_ff), BF16)],
        compiler_params=params(dimension_semantics=("arbitrary",)),
        name="pre_mix",
    )(x.reshape(t, d), row(ffn1_norm_g), ffn1_w_gate.reshape(d, d_ff),
      ffn1_w_up.reshape(d, d_ff), ffn1_w_down.reshape(d_ff, d), row(mix_norm_g),
      w_in.reshape(d, in_width), cos_t, sin_t)

    mk, mv = pl.pallas_call(
        _mem_kv_kernel,
        grid=(b,),
        in_specs=[pl.BlockSpec((1, n_mem, d), lambda i: (i, 0, 0)), _resident((1, d)),
                  _resident((d, 2 * MEM_WIDTH))],
        out_specs=[pl.BlockSpec((1, n_mem, MEM_WIDTH), lambda i: (i, 0, 0)),
                   pl.BlockSpec((1, MEM_HEADS * n_mem, MEM_WIDTH), lambda i: (i, 0, 0))],
        out_shape=[jax.ShapeDtypeStruct((b, n_mem, MEM_WIDTH), BF16),
                   jax.ShapeDtypeStruct((b, MEM_HEADS * n_mem, MEM_WIDTH), BF16)],
        compiler_params=params(dimension_semantics=("parallel",)),
        name="mem_kv",
    )(mem, row(mem_norm_g), w_mem_kv.reshape(d, 2 * MEM_WIDTH))

    lam_spec = _resident((1, DA_HEAD_DIM))
    o_da = pl.pallas_call(
        _diff_attn_kernel,
        grid=(b, s // ta),
        in_specs=[lam_spec, lam_spec, lam_spec, lam_spec, _resident((DA_V_DIM, 1)),
                  pl.BlockSpec((1, ta, DA_WIDTH), lambda bi, qi: (bi, qi, 0)),
                  pl.BlockSpec((1, s, DA_WIDTH), lambda bi, qi: (bi, 0, 0)),
                  pl.BlockSpec((1, s // ta, DA_HEADS * V_AUG, ta),
                               lambda bi, qi: (bi, 0, 0, 0))],
        out_specs=pl.BlockSpec((1, ta, DA_WIDTH), lambda bi, qi: (bi, qi, 0)),
        out_shape=jax.ShapeDtypeStruct((b, s, DA_WIDTH), BF16),
        scratch_shapes=[pltpu.VMEM((2 * DA_HEADS, 1, ta), F32),
                        pltpu.VMEM((2 * DA_HEADS, V_AUG, ta), F32),
                        pltpu.VMEM((SCORE_SLOTS, ta, ta), F32),
                        pltpu.VMEM((SCORE_SLOTS, SOFTMAX_SLAB, ta), F32)],
        compiler_params=params(dimension_semantics=("parallel", "arbitrary")),
        name="diff_attn",
    )(lambda_q1, lambda_k1, lambda_q2, lambda_k2, subln_g.reshape(DA_V_DIM, 1),
      q.reshape(b, s, DA_WIDTH), k.reshape(b, s, DA_WIDTH),
      vt.reshape(b, s // ta, DA_HEADS * V_AUG, ta))

    halo_blocks = tm // CONV_HALO
    n_steps = t // tm + 1
    last = t // tm - 1
    out_slabs = d // FFN_CHUNK
    assert out_slabs <= n_stage
    mixed_tile = lambda i: jnp.clip(i - n_stage, 0, last)
    mixed = lambda w: pl.BlockSpec((tm, w), lambda i: (mixed_tile(i), 0))
    per_seq = lambda rows: pl.BlockSpec((1, rows, MEM_WIDTH),
                                        lambda i: (mixed_tile(i) // n_tiles, 0, 0))
    out = pl.pallas_call(
        functools.partial(_post_mix_kernel, tiles_per_seq=n_tiles),
        grid=(n_stage + n_steps,),
        in_specs=[mixed(d), mixed(DA_WIDTH), mixed(CONV_CH),
                  pl.BlockSpec((CONV_HALO, CONV_CH),
                               lambda i: (jnp.maximum(mixed_tile(i) * halo_blocks - 1, 0), 0)),
                  mixed(MEM_WIDTH), per_seq(n_mem), per_seq(MEM_HEADS * n_mem),
                  _resident((CONV_WIDTH, CONV_CH)), _resident((1, CONV_CH)),
                  _resident((1, CONV_CH)), _resident((1, CONV_CH)),
                  pl.BlockSpec((FFN_CHUNK, d), lambda i: (jnp.minimum(i, out_slabs - 1), 0)),
                  _resident((1, d)), col_slab, col_slab,
                  pl.BlockSpec((FFN_CHUNK, d), lambda i: (slab_of(i), 0)), _resident((1, d))],
        out_specs=pl.BlockSpec((tm, d), lambda i: (jnp.maximum(i - n_stage - 1, 0), 0)),
        out_shape=jax.ShapeDtypeStruct((t, d), F32),
        scratch_shapes=[pltpu.VMEM((d, d), BF16),
                        pltpu.VMEM((n_stage, d, FFN_CHUNK), BF16),
                        pltpu.VMEM((n_stage, d, FFN_CHUNK), BF16),
                        pltpu.VMEM((d_ff, d), BF16),
                        pltpu.VMEM((CONV_HALO + tm, CONV_CH), F32),
                        pltpu.VMEM((SUBLANES - 1, tm + CONV_HALO - SUBLANES, CONV_CH), F32),
                        pltpu.VMEM((tm, CONV_CH), BF16),
                        pltpu.VMEM((tm, MEM_HEADS * n_mem), BF16),
                        pltpu.VMEM((2, tm, d), F32),
                        pltpu.VMEM((tm, d), BF16),
                        pltpu.VMEM((tm, d_ff), BF16)],
        compiler_params=params(dimension_semantics=("arbitrary",)),
        name="post_mix",
    )(h1, o_da.reshape(t, DA_WIDTH), glu, glu, qm, mk, mv,
      conv_dw_w.reshape(CONV_WIDTH, CONV_CH), conv_dw_b,
      conv_ln_g, conv_ln_b, w_out.reshape(d, d), row(ffn2_norm_g),
      ffn2_w_gate.reshape(d, d_ff), ffn2_w_up.reshape(d, d_ff),
      ffn2_w_down.reshape(d_ff, d), row(final_norm_g))
    return out.reshape(b, s, d)
```

```python
import functools
import math

import jax
import jax.numpy as jnp
from jax import lax
from jax.experimental import pallas as pl
from jax.experimental.pallas import tpu as pltpu

F32 = jnp.float32
BF16 = jnp.bfloat16

EPS = 1e-5
ROPE_THETA = 10000.0
CHUNK = 64
DA_HEADS = 4
DA_HEAD_DIM = 64
DA_V_DIM = 128
V_AUG = DA_V_DIM + 16
DA_WIDTH = 512
CONV_CH = 256
CONV_WIDTH = 31
MEM_HEADS = 4
MEM_HEAD_DIM = 64
MEM_WIDTH = 256
LAM_INIT = 0.8 - 0.6 * math.exp(-0.3 * 0)

V7X_VMEM_BYTES = 64 * 1024 * 1024
VMEM_LIMIT = V7X_VMEM_BYTES - 8 * 1024 * 1024
LANES = 128
SUBLANES = 8
MXU_COLS = 256

TOKEN_TILE = 512
ATTN_TILE = 512
ATTN_QTILES = 2
FFN_CHUNK = 256
CONV_HALO = 32
CONV_ROWS = 64
SOFTMAX_SLAB = 16
SCORE_AHEAD = 2
SCORE_SLOTS = 4
NEG_BIG = -1e30
LOG2_E = math.log2(math.e)


def _rms(x, g):
    return x * lax.rsqrt(jnp.mean(x * x, axis=-1, keepdims=True) + EPS) * g


def _swiglu(xn, wg_ref, wu_ref, wd_ref, act_ref):
    for n in range(wg_ref.shape[0]):
        c = n * FFN_CHUNK
        gate = jnp.dot(xn, wg_ref[n], preferred_element_type=F32)
        up = jnp.dot(xn, wu_ref[n], preferred_element_type=F32)
        act_ref[:, c:c + FFN_CHUNK] = (gate * jax.nn.sigmoid(gate) * up).astype(BF16)
    return jnp.dot(act_ref[...], wd_ref[...], preferred_element_type=F32)


def _stage_weight_slabs(step, wg32_ref, wu32_ref, wd32_ref, wg_ref, wu_ref, wd_ref):
    wg_ref[step] = wg32_ref[...].astype(BF16)
    wu_ref[step] = wu32_ref[...].astype(BF16)
    rows = pl.ds(pl.multiple_of(step * FFN_CHUNK, FFN_CHUNK), FFN_CHUNK)
    wd_ref[rows, :] = wd32_ref[...].astype(BF16)


def _rope(x, cos, sin_signed, first_half):
    partner = jnp.where(first_half,
                        pltpu.roll(x, LANES - DA_HEAD_DIM // 2, axis=1),
                        pltpu.roll(x, DA_HEAD_DIM // 2, axis=1))
    return x * cos + partner * sin_signed


def _pre_mix_kernel(x_ref, g1_ref, wg32_ref, wu32_ref, wd32_ref, gm_ref, win32_ref,
                    cos_ref, sin_ref,
                    h_ref, q_ref, k_ref, vt_ref, glu_ref, qm_ref,
                    wg_ref, wu_ref, wd_ref, win_ref, act_ref):
    step = pl.program_id(0)
    n_stage = wg_ref.shape[0]

    @pl.when(step < n_stage)
    def _():
        _stage_weight_slabs(step, wg32_ref, wu32_ref, wd32_ref, wg_ref, wu_ref, wd_ref)
        win_ref[jnp.minimum(step, win_ref.shape[0] - 1)] = win32_ref[...].astype(BF16)

    @pl.when(step >= n_stage)
    def _():
        _pre_mix_tile(x_ref, g1_ref, wg_ref, wu_ref, wd_ref, gm_ref, win_ref, cos_ref, sin_ref,
                      h_ref, q_ref, k_ref, vt_ref, glu_ref, qm_ref, act_ref)


def _pre_mix_tile(x_ref, g1_ref, wg_ref, wu_ref, wd_ref, gm_ref, win_ref, cos_ref, sin_ref,
                  h_ref, q_ref, k_ref, vt_ref, glu_ref, qm_ref, act_ref):
    x = x_ref[...]
    xn = _rms(x, g1_ref[...]).astype(BF16)
    h = x + 0.5 * _swiglu(xn, wg_ref, wu_ref, wd_ref, act_ref)
    h_ref[...] = h

    n = _rms(h, gm_ref[...]).astype(BF16)
    proj = lambda slab: jnp.dot(n, win_ref[slab], preferred_element_type=F32)
    reps = LANES // cos_ref.shape[0]
    cos = jnp.tile(cos_ref[...].T, (1, reps))
    sin = jnp.tile(sin_ref[...].T, (1, reps))
    lane = lax.broadcasted_iota(jnp.int32, cos.shape, 1)
    first_half = (lane % DA_HEAD_DIM) < (DA_HEAD_DIM // 2)
    sin_signed = jnp.where(first_half, -sin, sin)
    scale = DA_HEAD_DIM ** -0.5 * LOG2_E
    q_slabs = DA_WIDTH // MXU_COLS
    for sl in range(q_slabs):
        c0 = sl * MXU_COLS
        qs = proj(sl)
        ks = proj(q_slabs + sl)
        for l0 in range(0, MXU_COLS, LANES):
            q_ref[:, c0 + l0:c0 + l0 + LANES] = (
                _rope(qs[:, l0:l0 + LANES], cos, sin_signed, first_half) * scale).astype(BF16)
            k_ref[:, c0 + l0:c0 + l0 + LANES] = _rope(
                ks[:, l0:l0 + LANES], cos, sin_signed, first_half).astype(BF16)
    vt = jnp.concatenate([proj(2 * q_slabs + sl).T for sl in range(q_slabs)],
                         axis=0).astype(BF16)
    ones = jnp.ones((V_AUG - DA_V_DIM, vt.shape[1]), BF16)
    for hd in range(DA_HEADS):
        vt_ref[0, hd * V_AUG:hd * V_AUG + DA_V_DIM, :] = vt[hd * DA_V_DIM:(hd + 1) * DA_V_DIM]
        vt_ref[0, hd * V_AUG + DA_V_DIM:(hd + 1) * V_AUG, :] = ones
    sl = 3 * q_slabs
    glu_ref[...] = proj(sl) * jax.nn.sigmoid(proj(sl + 1))
    qm_ref[...] = (proj(sl + 2) * MEM_HEAD_DIM ** -0.5).astype(BF16)


def _mem_kv_kernel(mem_ref, g_ref, w_ref, mk_ref, mv_ref):
    n_mem = mem_ref.shape[1]
    mn = _rms(mem_ref[0], g_ref[...]).astype(BF16)
    kv = jnp.dot(mn, w_ref[...].astype(BF16), preferred_element_type=F32)
    mk_ref[0] = kv[:, :MEM_WIDTH].astype(BF16)
    mv = kv[:, MEM_WIDTH:]
    head_of_lane = lax.broadcasted_iota(jnp.int32, mv.shape, 1) // MEM_HEAD_DIM
    for hd in range(MEM_HEADS):
        mv_ref[0, hd * n_mem:(hd + 1) * n_mem, :] = jnp.where(
            head_of_lane == hd, mv, 0.0).astype(BF16)


def _diff_attn_kernel(lq1_ref, lk1_ref, lq2_ref, lk2_ref, sg_ref,
                      q_ref, k_ref, vt_ref, o_ref, m_ref, acc_ref, s_ref, mpart_ref):
    pair = pl.program_id(1)
    tq = ATTN_TILE
    tk = tq
    per_tile = 2 * DA_HEADS
    n_chain = ATTN_QTILES * per_tile

    lane = lax.broadcasted_iota(jnp.int32, (tq, LANES), 1)
    q_comp = []
    for sub in range(ATTN_QTILES):
        for hd in range(DA_HEADS):
            q = q_ref[0, sub * tq:(sub + 1) * tq, hd * LANES:(hd + 1) * LANES]
            zero = jnp.zeros_like(q)
            q_comp.append(jnp.where(lane < DA_HEAD_DIM, q, zero))
            q_comp.append(jnp.where(lane >= DA_HEAD_DIM, q, zero))

    m_ref[...] = jnp.full(m_ref.shape, NEG_BIG, F32)
    acc_ref[...] = jnp.zeros(acc_ref.shape, F32)

    n_slab = tk // SOFTMAX_SLAB
    rows = lambda r: slice(r * SOFTMAX_SLAB, (r + 1) * SOFTMAX_SLAB)
    head_of = lambda c: (c % per_tile) // 2

    def scores(jb, c, slot, mask):
        hd = head_of(c)
        start = pl.multiple_of(jb * tk, tk)
        kb = k_ref[0, pl.ds(start, tk), hd * LANES:(hd + 1) * LANES]
        s = lax.dot_general(kb, q_comp[c], (((1,), (1,)), ((), ())),
                            preferred_element_type=F32)
        if mask is not None:
            s = jnp.where(mask, s, NEG_BIG)
        s_ref[slot] = s
        part = s[rows(0)]
        for r in range(1, n_slab):
            part = jnp.maximum(part, s[rows(r)])
        mpart_ref[slot] = part

    def softmax_pv(jb, c, slot):
        hd = head_of(c)
        vt = vt_ref[0, jb, hd * V_AUG:(hd + 1) * V_AUG, :]
        m_old = m_ref[c]
        m_new = jnp.maximum(m_old, jnp.max(mpart_ref[slot], axis=0, keepdims=True))
        alpha = jnp.exp2(m_old - m_new)
        m_ref[c] = m_new
        m_slab = jnp.broadcast_to(m_new, (SOFTMAX_SLAB, tq))
        p = jnp.concatenate(
            [jnp.exp2(s_ref[slot, rows(r), :] - m_slab).astype(BF16) for r in range(n_slab)],
            axis=0)
        acc_ref[c] = alpha * acc_ref[c] + jnp.dot(vt, p, preferred_element_type=F32)

    key_chunk = lax.broadcasted_iota(jnp.int32, (tk, tq), 0) // CHUNK
    qry_chunk = lax.broadcasted_iota(jnp.int32, (tk, tq), 1) // CHUNK
    diag_mask = key_chunk <= qry_chunk

    first = ATTN_QTILES * pair
    tasks = []
    for u in reversed(range(ATTN_QTILES)):
        for sub in range(u, ATTN_QTILES):
            for i in range(per_tile):
                tasks.append((first + u, sub * per_tile + i, diag_mask if sub == u else None))
    assert len(tasks) % SCORE_SLOTS == 0 and n_chain % SCORE_SLOTS == 0
    n_loop = first

    for t in range(SCORE_AHEAD):
        scores(tasks[t][0], tasks[t][1], t % SCORE_SLOTS, tasks[t][2])
    for t, (jb, c, _) in enumerate(tasks):
        ahead = t + SCORE_AHEAD
        if ahead < len(tasks):
            scores(tasks[ahead][0], tasks[ahead][1], ahead % SCORE_SLOTS, tasks[ahead][2])
        else:
            scores(0, ahead - len(tasks), ahead % SCORE_SLOTS, None)
        softmax_pv(jb, c, t % SCORE_SLOTS)

    def body(j, carry):
        j_next = jnp.minimum(j + 1, n_loop - 1)
        for c in range(n_chain):
            ahead = c + SCORE_AHEAD
            if ahead < n_chain:
                scores(j, ahead, ahead % SCORE_SLOTS, None)
            else:
                scores(j_next, ahead - n_chain, ahead % SCORE_SLOTS, None)
            softmax_pv(j, c, c % SCORE_SLOTS)
        return carry

    lax.fori_loop(0, n_loop, body, 0)

    lam = (jnp.exp(jnp.sum(lq1_ref[...] * lk1_ref[...], axis=-1, keepdims=True))
           - jnp.exp(jnp.sum(lq2_ref[...] * lk2_ref[...], axis=-1, keepdims=True))
           + LAM_INIT)
    for sub in range(ATTN_QTILES):
        for hd in range(DA_HEADS):
            c = sub * per_tile + 2 * hd
            a1, a2 = acc_ref[c], acc_ref[c + 1]
            o = (a1[:DA_V_DIM] / a1[DA_V_DIM:DA_V_DIM + 1]
                 - lam * (a2[:DA_V_DIM] / a2[DA_V_DIM:DA_V_DIM + 1]))
            o = o * lax.rsqrt(jnp.mean(o * o, axis=0, keepdims=True) + EPS)
            o = o * sg_ref[...] * (1.0 - LAM_INIT)
            o_ref[0, sub * tq:(sub + 1) * tq, hd * DA_V_DIM:(hd + 1) * DA_V_DIM] = (
                o.T.astype(BF16))


def _post_mix_kernel(h_ref, oda_ref, glu_ref, halo_ref, qm_ref, mk_ref, mv_ref,
                     dww_ref, dwb_ref, lng_ref, lnb_ref, wout32_ref,
                     g2_ref, wg32_ref, wu32_ref, wd32_ref, gf_ref,
                     out_ref, wout_ref, wg_ref, wu_ref, wd_ref,
                     cbuf_ref, shift_ref, yconv_ref, pmem_ref, h2_ref, xn_ref, act_ref,
                     *, tiles_per_seq):
    n_stage = wg_ref.shape[0]
    pid = pl.program_id(0)

    @pl.when(pid < n_stage)
    def _():
        _stage_weight_slabs(pid, wg32_ref, wu32_ref, wd32_ref, wg_ref, wu_ref, wd_ref)
        slab = jnp.minimum(pid, wout_ref.shape[0] // FFN_CHUNK - 1)
        rows = pl.ds(pl.multiple_of(slab * FFN_CHUNK, FFN_CHUNK), FFN_CHUNK)
        wout_ref[rows, :] = wout32_ref[...].astype(BF16)

    @pl.when(pid >= n_stage)
    def _():
        _post_mix_step(pid - n_stage, pl.num_programs(0) - n_stage,
                       h_ref, oda_ref, glu_ref, halo_ref, qm_ref, mk_ref, mv_ref,
                       dww_ref, dwb_ref, lng_ref, lnb_ref, wout_ref,
                       g2_ref, wg_ref, wu_ref, wd_ref, gf_ref,
                       out_ref, cbuf_ref, shift_ref, yconv_ref, pmem_ref, h2_ref, xn_ref,
                       act_ref, tiles_per_seq)


def _post_mix_step(step, n_steps, h_ref, oda_ref, glu_ref, halo_ref, qm_ref, mk_ref, mv_ref,
                   dww_ref, dwb_ref, lng_ref, lnb_ref, wout_ref,
                   g2_ref, wg_ref, wu_ref, wd_ref, gf_ref,
                   out_ref, cbuf_ref, shift_ref, yconv_ref, pmem_ref, h2_ref, xn_ref, act_ref,
                   tiles_per_seq):
    tm = h_ref.shape[0]
    n_mem = mk_ref.shape[1]
    n_slabs = wg_ref.shape[0]
    tile = jnp.minimum(step, n_steps - 2)
    seq_start = (tile % tiles_per_seq) == 0
    base = CONV_HALO - (CONV_WIDTH - 1)

    @pl.when(step == 0)
    def _():
        h2_ref[1] = jnp.zeros(h2_ref.shape[1:], F32)

    def after(c):
        tile_ = act_ref[0:2 * SUBLANES, c * FFN_CHUNK:c * FFN_CHUNK + LANES]
        return jnp.sum(tile_.astype(F32) * 0.0, keepdims=True)

    def before_next_slab(done_bf16):
        sl = (slice(0, 2 * SUBLANES), slice(0, LANES))
        xn_ref[sl] = xn_ref[sl] + done_bf16 * jnp.zeros_like(done_bf16)

    def conv_setup(c):
        halo = halo_ref[...]
        cbuf_ref[0:CONV_HALO, :] = jnp.where(seq_start, jnp.zeros_like(halo), halo)
        cbuf_ref[CONV_HALO:CONV_HALO + tm, :] = glu_ref[...]
        span = tm + CONV_HALO - SUBLANES
        for r in range(1, SUBLANES):
            shift_ref[r - 1] = cbuf_ref[r:r + span, :]
        before_next_slab(shift_ref[SUBLANES - 2, 0:2 * SUBLANES, 0:LANES].astype(BF16))

    def conv_rows(r0, wait):
        y = jnp.zeros((CONV_ROWS, CONV_CH), F32) + wait
        for j in range(CONV_WIDTH):
            phase = (base + j) % SUBLANES
            start = base + j - phase + r0
            if phase == 0:
                window = cbuf_ref[start:start + CONV_ROWS, :]
            else:
                window = shift_ref[phase - 1, start:start + CONV_ROWS, :]
            y = y + dww_ref[j:j + 1, :] * window
        y = y + dwb_ref[...]
        mu = jnp.mean(y, axis=-1, keepdims=True)
        yc = y - mu
        var = jnp.mean(yc * yc, axis=-1, keepdims=True)
        y = yc * lax.rsqrt(var + EPS) * lng_ref[...] + lnb_ref[...]
        piece = (y * jax.nn.sigmoid(y)).astype(BF16)
        yconv_ref[r0:r0 + CONV_ROWS, :] = piece
        return piece

    def conv_pieces(r0s, c):
        wait = after(c - 1)
        for r0 in r0s:
            piece = conv_rows(r0, wait)
        before_next_slab(piece[0:2 * SUBLANES, 0:LANES])

    def mem_heads(heads, c):
        qm = qm_ref[...]
        head_of_lane = lax.broadcasted_iota(jnp.int32, qm.shape, 1) // MEM_HEAD_DIM
        wait = after(c - 1)
        for hd in heads:
            qh = jnp.where(head_of_lane == hd, qm, jnp.zeros_like(qm))
            sc = lax.dot_general(qh, mk_ref[0], (((1,), (1,)), ((), ())),
                                 preferred_element_type=F32)
            sc = sc - (jnp.max(sc, axis=-1, keepdims=True) + wait)
            p = jnp.exp(sc)
            p = (p / jnp.sum(p, axis=-1, keepdims=True)).astype(BF16)
            pmem_ref[:, hd * n_mem:(hd + 1) * n_mem] = p
        before_next_slab(p[0:2 * SUBLANES, 0:LANES])

    pieces = [conv_setup]
    pieces += [functools.partial(conv_pieces, (r0,)) for r0 in range(0, tm, CONV_ROWS)]
    pieces += [functools.partial(mem_heads, (0, 1)), functools.partial(mem_heads, (2, 3))]
    assert len(pieces) <= n_slabs, "at most one mixing piece per SwiGLU slab"
    pieces += [None] * (n_slabs - len(pieces))

    h_prev = h2_ref[(step + 1) % 2]
    xn_ref[...] = _rms(h_prev, g2_ref[...]).astype(BF16)

    h2_ref[step % 2] = h_ref[...] + jnp.dot(oda_ref[...], wout_ref[0:DA_WIDTH, :],
                                            preferred_element_type=F32)
    for n in range(n_slabs):
        c0 = n * FFN_CHUNK
        xn = xn_ref[...]
        gate = jnp.dot(xn, wg_ref[n], preferred_element_type=F32)
        up = jnp.dot(xn, wu_ref[n], preferred_element_type=F32)
        if pieces[n] is not None:
            pieces[n](n)
        act_ref[:, c0:c0 + FFN_CHUNK] = (gate * jax.nn.sigmoid(gate) * up).astype(BF16)
    ffn = jnp.dot(act_ref[...], wd_ref[...], preferred_element_type=F32)

    o_mem = jnp.dot(pmem_ref[...], mv_ref[0], preferred_element_type=F32).astype(BF16)
    mix = (jnp.dot(yconv_ref[...], wout_ref[DA_WIDTH:DA_WIDTH + CONV_CH, :],
                   preferred_element_type=F32)
           + jnp.dot(o_mem, wout_ref[DA_WIDTH + CONV_CH:, :], preferred_element_type=F32))
    h2_ref[step % 2] = h2_ref[step % 2] + mix
    out_ref[...] = _rms(h_prev + 0.5 * ffn, gf_ref[...])


def _resident(shape):
    return pl.BlockSpec(shape, lambda *_: (0,) * len(shape), pipeline_mode=pl.Buffered(1))


def _rope_tables(seq):
    half = DA_HEAD_DIM // 2
    inv_freq = 1.0 / (ROPE_THETA ** (jnp.arange(0, DA_HEAD_DIM, 2, dtype=F32) / DA_HEAD_DIM))
    ang = inv_freq[:, None] * jnp.arange(seq, dtype=F32)[None, :]
    assert ang.shape == (half, seq)
    return jnp.cos(ang), jnp.sin(ang)


def kernel(x, mem, ffn1_norm_g, ffn1_w_gate, ffn1_w_up, ffn1_w_down, mix_norm_g, mem_norm_g, w_in, lambda_q1, lambda_k1, lambda_q2, lambda_k2, subln_g, conv_dw_w, conv_dw_b, conv_ln_g, conv_ln_b, w_mem_kv, w_out, ffn2_norm_g, ffn2_w_gate, ffn2_w_up, ffn2_w_down, final_norm_g):
    b, s, d = x.shape
    n_mem = mem.shape[1]
    d_ff = ffn1_w_gate.shape[-1]
    in_width = w_in.shape[-1]
    assert ffn1_norm_g.shape[0] == 1, "single layer"
    tm = TOKEN_TILE
    ta = ATTN_TILE
    assert s % tm == 0 and s % (ATTN_QTILES * ta) == 0 and tm == ta and d_ff % FFN_CHUNK == 0
    t = b * s
    n_tiles = s // tm
    params = functools.partial(pltpu.CompilerParams, vmem_limit_bytes=VMEM_LIMIT)

    cos_t, sin_t = _rope_tables(s)
    row = lambda a: a.reshape(1, -1)

    n_stage = d_ff // FFN_CHUNK
    in_slabs = in_width // MXU_COLS
    assert FFN_CHUNK == MXU_COLS and in_slabs <= n_stage
    tile_of = lambda i: jnp.maximum(i - n_stage, 0)
    slab_of = lambda i: jnp.minimum(i, n_stage - 1)
    tok = lambda w: pl.BlockSpec((tm, w), lambda i: (tile_of(i), 0))
    col_slab = pl.BlockSpec((d, FFN_CHUNK), lambda i: (0, slab_of(i)))
    h1, q, k, vt, glu, qm = pl.pallas_call(
        _pre_mix_kernel,
        grid=(n_stage + t // tm,),
        in_specs=[tok(d), _resident((1, d)), col_slab, col_slab,
                  pl.BlockSpec((FFN_CHUNK, d), lambda i: (slab_of(i), 0)), _resident((1, d)),
                  pl.BlockSpec((d, MXU_COLS), lambda i: (0, jnp.minimum(i, in_slabs - 1))),
                  pl.BlockSpec((DA_HEAD_DIM // 2, tm), lambda i: (0, tile_of(i) % n_tiles)),
                  pl.BlockSpec((DA_HEAD_DIM // 2, tm), lambda i: (0, tile_of(i) % n_tiles))],
        out_specs=[tok(d), tok(DA_WIDTH), tok(DA_WIDTH),
                   pl.BlockSpec((1, DA_HEADS * V_AUG, tm), lambda i: (tile_of(i), 0, 0)),
                   tok(CONV_CH), tok(MEM_WIDTH)],
        out_shape=[jax.ShapeDtypeStruct((t, d), F32),
                   jax.ShapeDtypeStruct((t, DA_WIDTH), BF16),
                   jax.ShapeDtypeStruct((t, DA_WIDTH), BF16),
                   jax.ShapeDtypeStruct((t // tm, DA_HEADS * V_AUG, tm), BF16),
                   jax.ShapeDtypeStruct((t, CONV_CH), F32),
                   jax.ShapeDtypeStruct((t, MEM_WIDTH), BF16)],
        scratch_shapes=[pltpu.VMEM((n_stage, d, FFN_CHUNK), BF16),
                        pltpu.VMEM((n_stage, d, FFN_CHUNK), BF16),
                        pltpu.VMEM((d_ff, d), BF16),
                        pltpu.VMEM((in_slabs, d, MXU_COLS), BF16),
                        pltpu.VMEM((tm, d_ff), BF16)],
        compiler_params=params(dimension_semantics=("arbitrary",)),
        name="pre_mix",
    )(x.reshape(t, d), row(ffn1_norm_g), ffn1_w_gate.reshape(d, d_ff),
      ffn1_w_up.reshape(d, d_ff), ffn1_w_down.reshape(d_ff, d), row(mix_norm_g),
      w_in.reshape(d, in_width), cos_t, sin_t)

    mk, mv = pl.pallas_call(
        _mem_kv_kernel,
        grid=(b,),
        in_specs=[pl.BlockSpec((1, n_mem, d), lambda i: (i, 0, 0)), _resident((1, d)),
                  _resident((d, 2 * MEM_WIDTH))],
        out_specs=[pl.BlockSpec((1, n_mem, MEM_WIDTH), lambda i: (i, 0, 0)),
                   pl.BlockSpec((1, MEM_HEADS * n_mem, MEM_WIDTH), lambda i: (i, 0, 0))],
        out_shape=[jax.ShapeDtypeStruct((b, n_mem, MEM_WIDTH), BF16),
                   jax.ShapeDtypeStruct((b, MEM_HEADS * n_mem, MEM_WIDTH), BF16)],
        compiler_params=params(dimension_semantics=("parallel",)),
        name="mem_kv",
    )(mem, row(mem_norm_g), w_mem_kv.reshape(d, 2 * MEM_WIDTH))

    lam_spec = _resident((1, DA_HEAD_DIM))
    o_da = pl.pallas_call(
        _diff_attn_kernel,
        grid=(b, s // (ATTN_QTILES * ta)),
        in_specs=[lam_spec, lam_spec, lam_spec, lam_spec, _resident((DA_V_DIM, 1)),
                  pl.BlockSpec((1, ATTN_QTILES * ta, DA_WIDTH), lambda bi, qi: (bi, qi, 0)),
                  pl.BlockSpec((1, s, DA_WIDTH), lambda bi, qi: (bi, 0, 0)),
                  pl.BlockSpec((1, s // ta, DA_HEADS * V_AUG, ta),
                               lambda bi, qi: (bi, 0, 0, 0))],
        out_specs=pl.BlockSpec((1, ATTN_QTILES * ta, DA_WIDTH), lambda bi, qi: (bi, qi, 0)),
        out_shape=jax.ShapeDtypeStruct((b, s, DA_WIDTH), BF16),
        scratch_shapes=[pltpu.VMEM((ATTN_QTILES * 2 * DA_HEADS, 1, ta), F32),
                        pltpu.VMEM((ATTN_QTILES * 2 * DA_HEADS, V_AUG, ta), F32),
                        pltpu.VMEM((SCORE_SLOTS, ta, ta), F32),
                        pltpu.VMEM((SCORE_SLOTS, SOFTMAX_SLAB, ta), F32)],
        compiler_params=params(dimension_semantics=("parallel", "arbitrary")),
        name="diff_attn",
    )(lambda_q1, lambda_k1, lambda_q2, lambda_k2, subln_g.reshape(DA_V_DIM, 1),
      q.reshape(b, s, DA_WIDTH), k.reshape(b, s, DA_WIDTH),
      vt.reshape(b, s // ta, DA_HEADS * V_AUG, ta))

    halo_blocks = tm // CONV_HALO
    n_steps = t // tm + 1
    last = t // tm - 1
    out_slabs = d // FFN_CHUNK
    assert out_slabs <= n_stage
    mixed_tile = lambda i: jnp.clip(i - n_stage, 0, last)
    mixed = lambda w: pl.BlockSpec((tm, w), lambda i: (mixed_tile(i), 0))
    per_seq = lambda rows: pl.BlockSpec((1, rows, MEM_WIDTH),
                                        lambda i: (mixed_tile(i) // n_tiles, 0, 0))
    out = pl.pallas_call(
        functools.partial(_post_mix_kernel, tiles_per_seq=n_tiles),
        grid=(n_stage + n_steps,),
        in_specs=[mixed(d), mixed(DA_WIDTH), mixed(CONV_CH),
                  pl.BlockSpec((CONV_HALO, CONV_CH),
                               lambda i: (jnp.maximum(mixed_tile(i) * halo_blocks - 1, 0), 0)),
                  mixed(MEM_WIDTH), per_seq(n_mem), per_seq(MEM_HEADS * n_mem),
                  _resident((CONV_WIDTH, CONV_CH)), _resident((1, CONV_CH)),
                  _resident((1, CONV_CH)), _resident((1, CONV_CH)),
                  pl.BlockSpec((FFN_CHUNK, d), lambda i: (jnp.minimum(i, out_slabs - 1), 0)),
                  _resident((1, d)), col_slab, col_slab,
                  pl.BlockSpec((FFN_CHUNK, d), lambda i: (slab_of(i), 0)), _resident((1, d))],
        out_specs=pl.BlockSpec((tm, d), lambda i: (jnp.maximum(i - n_stage - 1, 0), 0)),
        out_shape=jax.ShapeDtypeStruct((t, d), F32),
        scratch_shapes=[pltpu.VMEM((d, d), BF16),
                        pltpu.VMEM((n_stage, d, FFN_CHUNK), BF16),
                        pltpu.VMEM((n_stage, d, FFN_CHUNK), BF16),
                        pltpu.VMEM((d_ff, d), BF16),
                        pltpu.VMEM((CONV_HALO + tm, CONV_CH), F32),
                        pltpu.VMEM((SUBLANES - 1, tm + CONV_HALO - SUBLANES, CONV_CH), F32),
                        pltpu.VMEM((tm, CONV_CH), BF16),
                        pltpu.VMEM((tm, MEM_HEADS * n_mem), BF16),
                        pltpu.VMEM((2, tm, d), F32),
                        pltpu.VMEM((tm, d), BF16),
                        pltpu.VMEM((tm, d_ff), BF16)],
        compiler_params=params(dimension_semantics=("arbitrary",)),
        name="post_mix",
    )(h1, o_da.reshape(t, DA_WIDTH), glu, glu, qm, mk, mv,
      conv_dw_w.reshape(CONV_WIDTH, CONV_CH), conv_dw_b,
      conv_ln_g, conv_ln_b, w_out.reshape(d, d), row(ffn2_norm_g),
      ffn2_w_gate.reshape(d, d_ff), ffn2_w_up.reshape(d, d_ff),
      ffn2_w_down.reshape(d_ff, d), row(final_norm_g))
    return out.reshape(b, s, d)
```

```python
import functools
import math

import jax
import jax.numpy as jnp
from jax import lax
from jax.experimental import pallas as pl
from jax.experimental.pallas import tpu as pltpu

F32 = jnp.float32
BF16 = jnp.bfloat16

EPS = 1e-5
ROPE_THETA = 10000.0
CHUNK = 64
DA_HEADS = 4
DA_HEAD_DIM = 64
DA_V_DIM = 128
V_AUG = DA_V_DIM + 16
DA_WIDTH = 512
CONV_CH = 256
CONV_WIDTH = 31
MEM_HEADS = 4
MEM_HEAD_DIM = 64
MEM_WIDTH = 256
LAM_INIT = 0.8 - 0.6 * math.exp(-0.3 * 0)

V7X_VMEM_BYTES = 64 * 1024 * 1024
VMEM_LIMIT = V7X_VMEM_BYTES - 8 * 1024 * 1024
LANES = 128
SUBLANES = 8
MXU_COLS = 256

TOKEN_TILE = 512
ATTN_TILE = 512
ATTN_QTILES = 2
FFN_CHUNK = 256
CONV_HALO = 32
CONV_ROWS = 64
SOFTMAX_SLAB = 16
SCORE_AHEAD = 2
SCORE_SLOTS = 4
NEG_BIG = -1e30
LOG2_E = math.log2(math.e)


def _rms(x, g):
    return x * lax.rsqrt(jnp.mean(x * x, axis=-1, keepdims=True) + EPS) * g


def _swiglu(xn, wg_ref, wu_ref, wd_ref, act_ref):
    for n in range(wg_ref.shape[0]):
        c = n * FFN_CHUNK
        gate = jnp.dot(xn, wg_ref[n], preferred_element_type=F32)
        up = jnp.dot(xn, wu_ref[n], preferred_element_type=F32)
        act_ref[:, c:c + FFN_CHUNK] = (gate * jax.nn.sigmoid(gate) * up).astype(BF16)
    return jnp.dot(act_ref[...], wd_ref[...], preferred_element_type=F32)


def _stage_weight_slabs(step, wg32_ref, wu32_ref, wd32_ref, wg_ref, wu_ref, wd_ref):
    wg_ref[step] = wg32_ref[...].astype(BF16)
    wu_ref[step] = wu32_ref[...].astype(BF16)
    rows = pl.ds(pl.multiple_of(step * FFN_CHUNK, FFN_CHUNK), FFN_CHUNK)
    wd_ref[rows, :] = wd32_ref[...].astype(BF16)


def _rope(x, cos, sin_signed, first_half):
    partner = jnp.where(first_half,
                        pltpu.roll(x, LANES - DA_HEAD_DIM // 2, axis=1),
                        pltpu.roll(x, DA_HEAD_DIM // 2, axis=1))
    return x * cos + partner * sin_signed


def _pre_mix_kernel(x_ref, g1_ref, wg32_ref, wu32_ref, wd32_ref, gm_ref, win32_ref,
                    cos_ref, sin_ref,
                    h_ref, q_ref, k_ref, vt_ref, glu_ref, qm_ref,
                    wg_ref, wu_ref, wd_ref, win_ref, act_ref):
    step = pl.program_id(0)
    n_stage = wg_ref.shape[0]

    @pl.when(step < n_stage)
    def _():
        _stage_weight_slabs(step, wg32_ref, wu32_ref, wd32_ref, wg_ref, wu_ref, wd_ref)
        win_ref[jnp.minimum(step, win_ref.shape[0] - 1)] = win32_ref[...].astype(BF16)

    @pl.when(step >= n_stage)
    def _():
        _pre_mix_tile(x_ref, g1_ref, wg_ref, wu_ref, wd_ref, gm_ref, win_ref, cos_ref, sin_ref,
                      h_ref, q_ref, k_ref, vt_ref, glu_ref, qm_ref, act_ref)


def _pre_mix_tile(x_ref, g1_ref, wg_ref, wu_ref, wd_ref, gm_ref, win_ref, cos_ref, sin_ref,
                  h_ref, q_ref, k_ref, vt_ref, glu_ref, qm_ref, act_ref):
    x = x_ref[...]
    xn = _rms(x, g1_ref[...]).astype(BF16)
    h = x + 0.5 * _swiglu(xn, wg_ref, wu_ref, wd_ref, act_ref)
    h_ref[...] = h

    n = _rms(h, gm_ref[...]).astype(BF16)
    proj = lambda slab: jnp.dot(n, win_ref[slab], preferred_element_type=F32)
    reps = LANES // cos_ref.shape[0]
    cos = jnp.tile(cos_ref[...].T, (1, reps))
    sin = jnp.tile(sin_ref[...].T, (1, reps))
    lane = lax.broadcasted_iota(jnp.int32, cos.shape, 1)
    first_half = (lane % DA_HEAD_DIM) < (DA_HEAD_DIM // 2)
    sin_signed = jnp.where(first_half, -sin, sin)
    scale = DA_HEAD_DIM ** -0.5 * LOG2_E
    q_slabs = DA_WIDTH // MXU_COLS
    for sl in range(q_slabs):
        c0 = sl * MXU_COLS
        qs = proj(sl)
        ks = proj(q_slabs + sl)
        for l0 in range(0, MXU_COLS, LANES):
            q_ref[:, c0 + l0:c0 + l0 + LANES] = (
                _rope(qs[:, l0:l0 + LANES], cos, sin_signed, first_half) * scale).astype(BF16)
            k_ref[:, c0 + l0:c0 + l0 + LANES] = _rope(
                ks[:, l0:l0 + LANES], cos, sin_signed, first_half).astype(BF16)
    vt = jnp.concatenate([proj(2 * q_slabs + sl).T for sl in range(q_slabs)],
                         axis=0).astype(BF16)
    ones = jnp.ones((V_AUG - DA_V_DIM, vt.shape[1]), BF16)
    for hd in range(DA_HEADS):
        vt_ref[0, hd * V_AUG:hd * V_AUG + DA_V_DIM, :] = vt[hd * DA_V_DIM:(hd + 1) * DA_V_DIM]
        vt_ref[0, hd * V_AUG + DA_V_DIM:(hd + 1) * V_AUG, :] = ones
    sl = 3 * q_slabs
    glu_ref[...] = proj(sl) * jax.nn.sigmoid(proj(sl + 1))
    qm_ref[...] = (proj(sl + 2) * MEM_HEAD_DIM ** -0.5).astype(BF16)


def _mem_kv_kernel(mem_ref, g_ref, w_ref, mk_ref, mv_ref):
    n_mem = mem_ref.shape[1]
    mn = _rms(mem_ref[0], g_ref[...]).astype(BF16)
    kv = jnp.dot(mn, w_ref[...].astype(BF16), preferred_element_type=F32)
    mk_ref[0] = kv[:, :MEM_WIDTH].astype(BF16)
    mv = kv[:, MEM_WIDTH:]
    head_of_lane = lax.broadcasted_iota(jnp.int32, mv.shape, 1) // MEM_HEAD_DIM
    for hd in range(MEM_HEADS):
        mv_ref[0, hd * n_mem:(hd + 1) * n_mem, :] = jnp.where(
            head_of_lane == hd, mv, 0.0).astype(BF16)


def _diff_attn_kernel(lq1_ref, lk1_ref, lq2_ref, lk2_ref, sg_ref,
                      q_ref, k_ref, vt_ref, o_ref, m_ref, acc_ref, s_ref, mpart_ref):
    pair = pl.program_id(1)
    tq = ATTN_TILE
    tk = tq
    per_tile = 2 * DA_HEADS
    n_chain = ATTN_QTILES * per_tile

    lane = lax.broadcasted_iota(jnp.int32, (tq, LANES), 1)
    q_comp = []
    for sub in range(ATTN_QTILES):
        for hd in range(DA_HEADS):
            q = q_ref[0, sub * tq:(sub + 1) * tq, hd * LANES:(hd + 1) * LANES]
            zero = jnp.zeros_like(q)
            q_comp.append(jnp.where(lane < DA_HEAD_DIM, q, zero))
            q_comp.append(jnp.where(lane >= DA_HEAD_DIM, q, zero))

    m_ref[...] = jnp.full(m_ref.shape, NEG_BIG, F32)
    acc_ref[...] = jnp.zeros(acc_ref.shape, F32)

    n_slab = tk // SOFTMAX_SLAB
    rows = lambda r: slice(r * SOFTMAX_SLAB, (r + 1) * SOFTMAX_SLAB)
    head_of = lambda c: (c % per_tile) // 2

    nt_dot = lambda a, b_: lax.dot_general(a, b_, (((1,), (1,)), ((), ())),
                                           preferred_element_type=F32)
    slab_max = lambda s: functools.reduce(
        jnp.maximum, [s[rows(r)] for r in range(s.shape[0] // SOFTMAX_SLAB)])

    def key_block(jb, c):
        start = pl.multiple_of(jb * tk, tk)
        hd = head_of(c)
        return k_ref[0, pl.ds(start, tk), hd * LANES:(hd + 1) * LANES]

    def scores(jb, c, slot, diag):
        if diag:
            return scores_diag(jb, c, slot)
        s = nt_dot(key_block(jb, c), q_comp[c])
        s_ref[slot] = s
        mpart_ref[slot] = slab_max(s)

    def softmax_start(c, slot):
        m_old = m_ref[c]
        m_new = jnp.maximum(m_old, jnp.max(mpart_ref[slot], axis=0, keepdims=True))
        m_ref[c] = m_new
        return jnp.exp2(m_old - m_new), jnp.broadcast_to(m_new, (SOFTMAX_SLAB, tq))

    def probs(slot, slabs, cols, m_slab):
        return jnp.concatenate(
            [jnp.exp2(s_ref[slot, rows(r), cols] - m_slab[:, cols]).astype(BF16)
             for r in slabs], axis=0)

    def softmax_pv(jb, c, slot, diag):
        if diag:
            return softmax_pv_diag(jb, c, slot)
        hd = head_of(c)
        vt = vt_ref[0, jb, hd * V_AUG:(hd + 1) * V_AUG, :]
        alpha, m_slab = softmax_start(c, slot)
        p = probs(slot, range(n_slab), slice(0, tq), m_slab)
        acc_ref[c] = alpha * acc_ref[c] + jnp.dot(vt, p, preferred_element_type=F32)

    half = tk // 2

    def chunk_mask(n_keys, n_queries):
        key_chunk = lax.broadcasted_iota(jnp.int32, (n_keys, n_queries), 0) // CHUNK
        qry_chunk = lax.broadcasted_iota(jnp.int32, (n_keys, n_queries), 1) // CHUNK
        return key_chunk <= qry_chunk

    top_mask, bot_mask = chunk_mask(half, tq), chunk_mask(half, half)
    right = slice(half, tq)

    def scores_diag(jb, c, slot):
        kb = key_block(jb, c)
        s_top = jnp.where(top_mask, nt_dot(kb[0:half], q_comp[c]), NEG_BIG)
        s_bot = jnp.where(bot_mask, nt_dot(kb[half:tk], q_comp[c][half:tq]), NEG_BIG)
        s_ref[slot, 0:half, :] = s_top
        s_ref[slot, half:tk, right] = s_bot
        top, bot = slab_max(s_top), slab_max(s_bot)
        mpart_ref[slot] = jnp.concatenate(
            [top[:, 0:half], jnp.maximum(top[:, right], bot)], axis=1)

    def softmax_pv_diag(jb, c, slot):
        hd = head_of(c)
        vt = vt_ref[0, jb, hd * V_AUG:(hd + 1) * V_AUG, :]
        alpha, m_slab = softmax_start(c, slot)
        p_top = probs(slot, range(n_slab // 2), slice(0, tq), m_slab)
        p_bot = probs(slot, range(n_slab // 2, n_slab), right, m_slab)
        pv = jnp.dot(vt[:, 0:half], p_top, preferred_element_type=F32)
        pv_right = pv[:, right] + jnp.dot(vt[:, half:tk], p_bot, preferred_element_type=F32)
        acc_ref[c] = alpha * acc_ref[c] + jnp.concatenate([pv[:, 0:half], pv_right], axis=1)

    first = ATTN_QTILES * pair
    tasks = []
    for u in reversed(range(ATTN_QTILES)):
        for sub in range(u, ATTN_QTILES):
            for i in range(per_tile):
                tasks.append((first + u, sub * per_tile + i, sub == u))
    assert len(tasks) % SCORE_SLOTS == 0 and n_chain % SCORE_SLOTS == 0
    n_loop = first

    for t in range(SCORE_AHEAD):
        scores(tasks[t][0], tasks[t][1], t % SCORE_SLOTS, tasks[t][2])
    for t, (jb, c, diag) in enumerate(tasks):
        ahead = t + SCORE_AHEAD
        if ahead < len(tasks):
            scores(tasks[ahead][0], tasks[ahead][1], ahead % SCORE_SLOTS, tasks[ahead][2])
        else:
            scores(0, ahead - len(tasks), ahead % SCORE_SLOTS, False)
        softmax_pv(jb, c, t % SCORE_SLOTS, diag)

    def body(j, carry):
        j_next = jnp.minimum(j + 1, n_loop - 1)
        for c in range(n_chain):
            ahead = c + SCORE_AHEAD
            if ahead < n_chain:
                scores(j, ahead, ahead % SCORE_SLOTS, False)
            else:
                scores(j_next, ahead - n_chain, ahead % SCORE_SLOTS, False)
            softmax_pv(j, c, c % SCORE_SLOTS, False)
        return carry

    lax.fori_loop(0, n_loop, body, 0)

    lam = (jnp.exp(jnp.sum(lq1_ref[...] * lk1_ref[...], axis=-1, keepdims=True))
           - jnp.exp(jnp.sum(lq2_ref[...] * lk2_ref[...], axis=-1, keepdims=True))
           + LAM_INIT)
    gain = jnp.broadcast_to(sg_ref[...], (DA_V_DIM, LANES)) * (1.0 - LAM_INIT)
    for sub in range(ATTN_QTILES):
        for hd in range(DA_HEADS):
            c = sub * per_tile + 2 * hd
            for l0 in range(0, tq, LANES):
                a1 = acc_ref[c, :, l0:l0 + LANES]
                a2 = acc_ref[c + 1, :, l0:l0 + LANES]
                w1 = 1.0 / a1[DA_V_DIM:DA_V_DIM + 1]
                w2 = lam / a2[DA_V_DIM:DA_V_DIM + 1]
                o = a1[:DA_V_DIM] * w1 - a2[:DA_V_DIM] * w2
                o = o * lax.rsqrt(jnp.mean(o * o, axis=0, keepdims=True) + EPS) * gain
                o_ref[0, sub * tq + l0:sub * tq + l0 + LANES,
                      hd * DA_V_DIM:(hd + 1) * DA_V_DIM] = o.T.astype(BF16)


def _post_mix_kernel(h_ref, oda_ref, glu_ref, halo_ref, qm_ref, mk_ref, mv_ref,
                     dww_ref, dwb_ref, lng_ref, lnb_ref, wout32_ref,
                     g2_ref, wg32_ref, wu32_ref, wd32_ref, gf_ref,
                     out_ref, wout_ref, wg_ref, wu_ref, wd_ref,
                     cbuf_ref, shift_ref, yconv_ref, pmem_ref, h2_ref, xn_ref, act_ref,
                     *, tiles_per_seq):
    n_stage = wg_ref.shape[0]
    pid = pl.program_id(0)

    @pl.when(pid < n_stage)
    def _():
        _stage_weight_slabs(pid, wg32_ref, wu32_ref, wd32_ref, wg_ref, wu_ref, wd_ref)
        slab = jnp.minimum(pid, wout_ref.shape[0] // FFN_CHUNK - 1)
        rows = pl.ds(pl.multiple_of(slab * FFN_CHUNK, FFN_CHUNK), FFN_CHUNK)
        wout_ref[rows, :] = wout32_ref[...].astype(BF16)

    @pl.when(pid >= n_stage)
    def _():
        _post_mix_step(pid - n_stage, pl.num_programs(0) - n_stage,
                       h_ref, oda_ref, glu_ref, halo_ref, qm_ref, mk_ref, mv_ref,
                       dww_ref, dwb_ref, lng_ref, lnb_ref, wout_ref,
                       g2_ref, wg_ref, wu_ref, wd_ref, gf_ref,
                       out_ref, cbuf_ref, shift_ref, yconv_ref, pmem_ref, h2_ref, xn_ref,
                       act_ref, tiles_per_seq)


def _post_mix_step(step, n_steps, h_ref, oda_ref, glu_ref, halo_ref, qm_ref, mk_ref, mv_ref,
                   dww_ref, dwb_ref, lng_ref, lnb_ref, wout_ref,
                   g2_ref, wg_ref, wu_ref, wd_ref, gf_ref,
                   out_ref, cbuf_ref, shift_ref, yconv_ref, pmem_ref, h2_ref, xn_ref, act_ref,
                   tiles_per_seq):
    tm = h_ref.shape[0]
    n_mem = mk_ref.shape[1]
    n_slabs = wg_ref.shape[0]
    tile = jnp.minimum(step, n_steps - 2)
    seq_start = (tile % tiles_per_seq) == 0
    base = CONV_HALO - (CONV_WIDTH - 1)

    @pl.when(step == 0)
    def _():
        h2_ref[1] = jnp.zeros(h2_ref.shape[1:], F32)

    def after(c):
        tile_ = act_ref[0:2 * SUBLANES, c * FFN_CHUNK:c * FFN_CHUNK + LANES]
        return jnp.sum(tile_.astype(F32) * 0.0, keepdims=True)

    def before_next_slab(done_bf16):
        sl = (slice(0, 2 * SUBLANES), slice(0, LANES))
        xn_ref[sl] = xn_ref[sl] + done_bf16 * jnp.zeros_like(done_bf16)

    def conv_setup(c):
        halo = halo_ref[...]
        cbuf_ref[0:CONV_HALO, :] = jnp.where(seq_start, jnp.zeros_like(halo), halo)
        cbuf_ref[CONV_HALO:CONV_HALO + tm, :] = glu_ref[...]
        span = tm + CONV_HALO - SUBLANES
        for r in range(1, SUBLANES):
            shift_ref[r - 1] = cbuf_ref[r:r + span, :]
        before_next_slab(shift_ref[SUBLANES - 2, 0:2 * SUBLANES, 0:LANES].astype(BF16))

    def conv_rows(r0, wait):
        y = jnp.zeros((CONV_ROWS, CONV_CH), F32) + wait
        for j in range(CONV_WIDTH):
            phase = (base + j) % SUBLANES
            start = base + j - phase + r0
            if phase == 0:
                window = cbuf_ref[start:start + CONV_ROWS, :]
            else:
                window = shift_ref[phase - 1, start:start + CONV_ROWS, :]
            y = y + dww_ref[j:j + 1, :] * window
        y = y + dwb_ref[...]
        mu = jnp.mean(y, axis=-1, keepdims=True)
        yc = y - mu
        var = jnp.mean(yc * yc, axis=-1, keepdims=True)
        y = yc * lax.rsqrt(var + EPS) * lng_ref[...] + lnb_ref[...]
        piece = (y * jax.nn.sigmoid(y)).astype(BF16)
        yconv_ref[r0:r0 + CONV_ROWS, :] = piece
        return piece

    def conv_pieces(r0s, c):
        wait = after(c - 1)
        for r0 in r0s:
            piece = conv_rows(r0, wait)
        before_next_slab(piece[0:2 * SUBLANES, 0:LANES])

    def mem_heads(heads, c):
        qm = qm_ref[...]
        head_of_lane = lax.broadcasted_iota(jnp.int32, qm.shape, 1) // MEM_HEAD_DIM
        wait = after(c - 1)
        for hd in heads:
            qh = jnp.where(head_of_lane == hd, qm, jnp.zeros_like(qm))
            sc = lax.dot_general(qh, mk_ref[0], (((1,), (1,)), ((), ())),
                                 preferred_element_type=F32)
            sc = sc - (jnp.max(sc, axis=-1, keepdims=True) + wait)
            p = jnp.exp(sc)
            p = (p / jnp.sum(p, axis=-1, keepdims=True)).astype(BF16)
            pmem_ref[:, hd * n_mem:(hd + 1) * n_mem] = p
        before_next_slab(p[0:2 * SUBLANES, 0:LANES])

    pieces = [conv_setup]
    pieces += [functools.partial(conv_pieces, (r0,)) for r0 in range(0, tm, CONV_ROWS)]
    pieces += [functools.partial(mem_heads, (0, 1)), functools.partial(mem_heads, (2, 3))]
    assert len(pieces) <= n_slabs, "at most one mixing piece per SwiGLU slab"
    pieces += [None] * (n_slabs - len(pieces))

    h_prev = h2_ref[(step + 1) % 2]
    xn_ref[...] = _rms(h_prev, g2_ref[...]).astype(BF16)

    h2_ref[step % 2] = h_ref[...] + jnp.dot(oda_ref[...], wout_ref[0:DA_WIDTH, :],
                                            preferred_element_type=F32)
    for n in range(n_slabs):
        c0 = n * FFN_CHUNK
        xn = xn_ref[...]
        gate = jnp.dot(xn, wg_ref[n], preferred_element_type=F32)
        up = jnp.dot(xn, wu_ref[n], preferred_element_type=F32)
        if pieces[n] is not None:
            pieces[n](n)
        act_ref[:, c0:c0 + FFN_CHUNK] = (gate * jax.nn.sigmoid(gate) * up).astype(BF16)
    ffn = jnp.dot(act_ref[...], wd_ref[...], preferred_element_type=F32)

    o_mem = jnp.dot(pmem_ref[...], mv_ref[0], preferred_element_type=F32).astype(BF16)
    mix = (jnp.dot(yconv_ref[...], wout_ref[DA_WIDTH:DA_WIDTH + CONV_CH, :],
                   preferred_element_type=F32)
           + jnp.dot(o_mem, wout_ref[DA_WIDTH + CONV_CH:, :], preferred_element_type=F32))
    h2_ref[step % 2] = h2_ref[step % 2] + mix
    out_ref[...] = _rms(h_prev + 0.5 * ffn, gf_ref[...])


def _resident(shape):
    return pl.BlockSpec(shape, lambda *_: (0,) * len(shape), pipeline_mode=pl.Buffered(1))


def _rope_tables(seq):
    half = DA_HEAD_DIM // 2
    inv_freq = 1.0 / (ROPE_THETA ** (jnp.arange(0, DA_HEAD_DIM, 2, dtype=F32) / DA_HEAD_DIM))
    ang = inv_freq[:, None] * jnp.arange(seq, dtype=F32)[None, :]
    assert ang.shape == (half, seq)
    return jnp.cos(ang), jnp.sin(ang)


def kernel(x, mem, ffn1_norm_g, ffn1_w_gate, ffn1_w_up, ffn1_w_down, mix_norm_g, mem_norm_g, w_in, lambda_q1, lambda_k1, lambda_q2, lambda_k2, subln_g, conv_dw_w, conv_dw_b, conv_ln_g, conv_ln_b, w_mem_kv, w_out, ffn2_norm_g, ffn2_w_gate, ffn2_w_up, ffn2_w_down, final_norm_g):
    b, s, d = x.shape
    n_mem = mem.shape[1]
    d_ff = ffn1_w_gate.shape[-1]
    in_width = w_in.shape[-1]
    assert ffn1_norm_g.shape[0] == 1, "single layer"
    tm = TOKEN_TILE
    ta = ATTN_TILE
    assert s % tm == 0 and s % (ATTN_QTILES * ta) == 0 and tm == ta and d_ff % FFN_CHUNK == 0
    t = b * s
    n_tiles = s // tm
    params = functools.partial(pltpu.CompilerParams, vmem_limit_bytes=VMEM_LIMIT)

    cos_t, sin_t = _rope_tables(s)
    row = lambda a: a.reshape(1, -1)

    n_stage = d_ff // FFN_CHUNK
    in_slabs = in_width // MXU_COLS
    assert FFN_CHUNK == MXU_COLS and in_slabs <= n_stage
    tile_of = lambda i: jnp.maximum(i - n_stage, 0)
    slab_of = lambda i: jnp.minimum(i, n_stage - 1)
    tok = lambda w: pl.BlockSpec((tm, w), lambda i: (tile_of(i), 0))
    col_slab = pl.BlockSpec((d, FFN_CHUNK), lambda i: (0, slab_of(i)))
    h1, q, k, vt, glu, qm = pl.pallas_call(
        _pre_mix_kernel,
        grid=(n_stage + t // tm,),
        in_specs=[tok(d), _resident((1, d)), col_slab, col_slab,
                  pl.BlockSpec((FFN_CHUNK, d), lambda i: (slab_of(i), 0)), _resident((1, d)),
                  pl.BlockSpec((d, MXU_COLS), lambda i: (0, jnp.minimum(i, in_slabs - 1))),
                  pl.BlockSpec((DA_HEAD_DIM // 2, tm), lambda i: (0, tile_of(i) % n_tiles)),
                  pl.BlockSpec((DA_HEAD_DIM // 2, tm), lambda i: (0, tile_of(i) % n_tiles))],
        out_specs=[tok(d), tok(DA_WIDTH), tok(DA_WIDTH),
                   pl.BlockSpec((1, DA_HEADS * V_AUG, tm), lambda i: (tile_of(i), 0, 0)),
                   tok(CONV_CH), tok(MEM_WIDTH)],
        out_shape=[jax.ShapeDtypeStruct((t, d), F32),
                   jax.ShapeDtypeStruct((t, DA_WIDTH), BF16),
                   jax.ShapeDtypeStruct((t, DA_WIDTH), BF16),
                   jax.ShapeDtypeStruct((t // tm, DA_HEADS * V_AUG, tm), BF16),
                   jax.ShapeDtypeStruct((t, CONV_CH), F32),
                   jax.ShapeDtypeStruct((t, MEM_WIDTH), BF16)],
        scratch_shapes=[pltpu.VMEM((n_stage, d, FFN_CHUNK), BF16),
                        pltpu.VMEM((n_stage, d, FFN_CHUNK), BF16),
                        pltpu.VMEM((d_ff, d), BF16),
                        pltpu.VMEM((in_slabs, d, MXU_COLS), BF16),
                        pltpu.VMEM((tm, d_ff), BF16)],
        compiler_params=params(dimension_semantics=("arbitrary",)),
        name="pre_mix",
    )(x.reshape(t, d), row(ffn1_norm_g), ffn1_w_gate.reshape(d, d_ff),
      ffn1_w_up.reshape(d, d_ff), ffn1_w_down.reshape(d_ff, d), row(mix_norm_g),
      w_in.reshape(d, in_width), cos_t, sin_t)

    mk, mv = pl.pallas_call(
        _mem_kv_kernel,
        grid=(b,),
        in_specs=[pl.BlockSpec((1, n_mem, d), lambda i: (i, 0, 0)), _resident((1, d)),
                  _resident((d, 2 * MEM_WIDTH))],
        out_specs=[pl.BlockSpec((1, n_mem, MEM_WIDTH), lambda i: (i, 0, 0)),
                   pl.BlockSpec((1, MEM_HEADS * n_mem, MEM_WIDTH), lambda i: (i, 0, 0))],
        out_shape=[jax.ShapeDtypeStruct((b, n_mem, MEM_WIDTH), BF16),
                   jax.ShapeDtypeStruct((b, MEM_HEADS * n_mem, MEM_WIDTH), BF16)],
        compiler_params=params(dimension_semantics=("parallel",)),
        name="mem_kv",
    )(mem, row(mem_norm_g), w_mem_kv.reshape(d, 2 * MEM_WIDTH))

    lam_spec = _resident((1, DA_HEAD_DIM))
    o_da = pl.pallas_call(
        _diff_attn_kernel,
        grid=(b, s // (ATTN_QTILES * ta)),
        in_specs=[lam_spec, lam_spec, lam_spec, lam_spec, _resident((DA_V_DIM, 1)),
                  pl.BlockSpec((1, ATTN_QTILES * ta, DA_WIDTH), lambda bi, qi: (bi, qi, 0)),
                  pl.BlockSpec((1, s, DA_WIDTH), lambda bi, qi: (bi, 0, 0)),
                  pl.BlockSpec((1, s // ta, DA_HEADS * V_AUG, ta),
                               lambda bi, qi: (bi, 0, 0, 0))],
        out_specs=pl.BlockSpec((1, ATTN_QTILES * ta, DA_WIDTH), lambda bi, qi: (bi, qi, 0)),
        out_shape=jax.ShapeDtypeStruct((b, s, DA_WIDTH), BF16),
        scratch_shapes=[pltpu.VMEM((ATTN_QTILES * 2 * DA_HEADS, 1, ta), F32),
                        pltpu.VMEM((ATTN_QTILES * 2 * DA_HEADS, V_AUG, ta), F32),
                        pltpu.VMEM((SCORE_SLOTS, ta, ta), F32),
                        pltpu.VMEM((SCORE_SLOTS, SOFTMAX_SLAB, ta), F32)],
        compiler_params=params(dimension_semantics=("parallel", "arbitrary")),
        name="diff_attn",
    )(lambda_q1, lambda_k1, lambda_q2, lambda_k2, subln_g.reshape(DA_V_DIM, 1),
      q.reshape(b, s, DA_WIDTH), k.reshape(b, s, DA_WIDTH),
      vt.reshape(b, s // ta, DA_HEADS * V_AUG, ta))

    halo_blocks = tm // CONV_HALO
    n_steps = t // tm + 1
    last = t // tm - 1
    out_slabs = d // FFN_CHUNK
    assert out_slabs <= n_stage
    mixed_tile = lambda i: jnp.clip(i - n_stage, 0, last)
    mixed = lambda w: pl.BlockSpec((tm, w), lambda i: (mixed_tile(i), 0))
    per_seq = lambda rows: pl.BlockSpec((1, rows, MEM_WIDTH),
                                        lambda i: (mixed_tile(i) // n_tiles, 0, 0))
    out = pl.pallas_call(
        functools.partial(_post_mix_kernel, tiles_per_seq=n_tiles),
        grid=(n_stage + n_steps,),
        in_specs=[mixed(d), mixed(DA_WIDTH), mixed(CONV_CH),
                  pl.BlockSpec((CONV_HALO, CONV_CH),
                               lambda i: (jnp.maximum(mixed_tile(i) * halo_blocks - 1, 0), 0)),
                  mixed(MEM_WIDTH), per_seq(n_mem), per_seq(MEM_HEADS * n_mem),
                  _resident((CONV_WIDTH, CONV_CH)), _resident((1, CONV_CH)),
                  _resident((1, CONV_CH)), _resident((1, CONV_CH)),
                  pl.BlockSpec((FFN_CHUNK, d), lambda i: (jnp.minimum(i, out_slabs - 1), 0)),
                  _resident((1, d)), col_slab, col_slab,
                  pl.BlockSpec((FFN_CHUNK, d), lambda i: (slab_of(i), 0)), _resident((1, d))],
        out_specs=pl.BlockSpec((tm, d), lambda i: (jnp.maximum(i - n_stage - 1, 0), 0)),
        out_shape=jax.ShapeDtypeStruct((t, d), F32),
        scratch_shapes=[pltpu.VMEM((d, d), BF16),
                        pltpu.VMEM((n_stage, d, FFN_CHUNK), BF16),
                        pltpu.VMEM((n_stage, d, FFN_CHUNK), BF16),
                        pltpu.VMEM((d_ff, d), BF16),
                        pltpu.VMEM((CONV_HALO + tm, CONV_CH), F32),
                        pltpu.VMEM((SUBLANES - 1, tm + CONV_HALO - SUBLANES, CONV_CH), F32),
                        pltpu.VMEM((tm, CONV_CH), BF16),
                        pltpu.VMEM((tm, MEM_HEADS * n_mem), BF16),
                        pltpu.VMEM((2, tm, d), F32),
                        pltpu.VMEM((tm, d), BF16),
                        pltpu.VMEM((tm, d_ff), BF16)],
        compiler_params=params(dimension_semantics=("arbitrary",)),
        name="post_mix",
    )(h1, o_da.reshape(t, DA_WIDTH), glu, glu, qm, mk, mv,
      conv_dw_w.reshape(CONV_WIDTH, CONV_CH), conv_dw_b,
      conv_ln_g, conv_ln_b, w_out.reshape(d, d), row(ffn2_norm_g),
      ffn2_w_gate.reshape(d, d_ff), ffn2_w_up.reshape(d, d_ff),
      ffn2_w_down.reshape(d_ff, d), row(final_norm_g))
    return out.reshape(b, s, d)
```

```python
import functools
import math

import jax
import jax.numpy as jnp
from jax import lax
from jax.experimental import pallas as pl
from jax.experimental.pallas import tpu as pltpu

F32 = jnp.float32
BF16 = jnp.bfloat16

EPS = 1e-5
ROPE_THETA = 10000.0
CHUNK = 64
DA_HEADS = 4
DA_HEAD_DIM = 64
DA_V_DIM = 128
V_AUG = DA_V_DIM + 16
DA_WIDTH = 512
CONV_CH = 256
CONV_WIDTH = 31
MEM_HEADS = 4
MEM_HEAD_DIM = 64
MEM_WIDTH = 256
LAM_INIT = 0.8 - 0.6 * math.exp(-0.3 * 0)

V7X_VMEM_BYTES = 64 * 1024 * 1024
VMEM_LIMIT = V7X_VMEM_BYTES - 8 * 1024 * 1024
LANES = 128
SUBLANES = 8
MXU_COLS = 256

TOKEN_TILE = 512
ATTN_TILE = 512
ATTN_QTILES = 2
FFN_CHUNK = 256
CONV_HALO = 32
CONV_ROWS = 64
SOFTMAX_SLAB = 16
SCORE_AHEAD = 2
SCORE_SLOTS = 4
NEG_BIG = -1e30
LOG2_E = math.log2(math.e)


def _rms(x, g):
    return x * lax.rsqrt(jnp.mean(x * x, axis=-1, keepdims=True) + EPS) * g


def _swiglu(xn, wg_ref, wu_ref, wd_ref, act_ref):
    for n in range(wg_ref.shape[0]):
        c = n * FFN_CHUNK
        gate = jnp.dot(xn, wg_ref[n], preferred_element_type=F32)
        up = jnp.dot(xn, wu_ref[n], preferred_element_type=F32)
        act_ref[:, c:c + FFN_CHUNK] = (gate * jax.nn.sigmoid(gate) * up).astype(BF16)
    return jnp.dot(act_ref[...], wd_ref[...], preferred_element_type=F32)


def _stage_weight_slabs(step, wg32_ref, wu32_ref, wd32_ref, wg_ref, wu_ref, wd_ref):
    wg_ref[step] = wg32_ref[...].astype(BF16)
    wu_ref[step] = wu32_ref[...].astype(BF16)
    rows = pl.ds(pl.multiple_of(step * FFN_CHUNK, FFN_CHUNK), FFN_CHUNK)
    wd_ref[rows, :] = wd32_ref[...].astype(BF16)


def _rope(x, cos, sin_signed, first_half):
    partner = jnp.where(first_half,
                        pltpu.roll(x, LANES - DA_HEAD_DIM // 2, axis=1),
                        pltpu.roll(x, DA_HEAD_DIM // 2, axis=1))
    return x * cos + partner * sin_signed


def _pre_mix_kernel(x_ref, g1_ref, wg32_ref, wu32_ref, wd32_ref, gm_ref, win32_ref,
                    cos_ref, sin_ref,
                    h_ref, q_ref, k_ref, vt_ref, glu_ref, qm_ref,
                    wg_ref, wu_ref, wd_ref, win_ref, act_ref):
    step = pl.program_id(0)
    n_stage = wg_ref.shape[0]

    @pl.when(step < n_stage)
    def _():
        _stage_weight_slabs(step, wg32_ref, wu32_ref, wd32_ref, wg_ref, wu_ref, wd_ref)
        win_ref[jnp.minimum(step, win_ref.shape[0] - 1)] = win32_ref[...].astype(BF16)

    @pl.when(step >= n_stage)
    def _():
        _pre_mix_tile(x_ref, g1_ref, wg_ref, wu_ref, wd_ref, gm_ref, win_ref, cos_ref, sin_ref,
                      h_ref, q_ref, k_ref, vt_ref, glu_ref, qm_ref, act_ref)


def _pre_mix_tile(x_ref, g1_ref, wg_ref, wu_ref, wd_ref, gm_ref, win_ref, cos_ref, sin_ref,
                  h_ref, q_ref, k_ref, vt_ref, glu_ref, qm_ref, act_ref):
    x = x_ref[...]
    xn = _rms(x, g1_ref[...]).astype(BF16)
    h = x + 0.5 * _swiglu(xn, wg_ref, wu_ref, wd_ref, act_ref)
    h_ref[...] = h

    n = _rms(h, gm_ref[...]).astype(BF16)
    proj = lambda slab: jnp.dot(n, win_ref[slab], preferred_element_type=F32)
    reps = LANES // cos_ref.shape[0]
    cos = jnp.tile(cos_ref[...].T, (1, reps))
    sin = jnp.tile(sin_ref[...].T, (1, reps))
    lane = lax.broadcasted_iota(jnp.int32, cos.shape, 1)
    first_half = (lane % DA_HEAD_DIM) < (DA_HEAD_DIM // 2)
    sin_signed = jnp.where(first_half, -sin, sin)
    scale = DA_HEAD_DIM ** -0.5 * LOG2_E
    q_slabs = DA_WIDTH // MXU_COLS
    for sl in range(q_slabs):
        c0 = sl * MXU_COLS
        qs = proj(sl)
        ks = proj(q_slabs + sl)
        for l0 in range(0, MXU_COLS, LANES):
            q_ref[:, c0 + l0:c0 + l0 + LANES] = (
                _rope(qs[:, l0:l0 + LANES], cos, sin_signed, first_half) * scale).astype(BF16)
            k_ref[:, c0 + l0:c0 + l0 + LANES] = _rope(
                ks[:, l0:l0 + LANES], cos, sin_signed, first_half).astype(BF16)
    vt = jnp.concatenate([proj(2 * q_slabs + sl).T for sl in range(q_slabs)],
                         axis=0).astype(BF16)
    ones = jnp.ones((V_AUG - DA_V_DIM, vt.shape[1]), BF16)
    for hd in range(DA_HEADS):
        vt_ref[0, hd * V_AUG:hd * V_AUG + DA_V_DIM, :] = vt[hd * DA_V_DIM:(hd + 1) * DA_V_DIM]
        vt_ref[0, hd * V_AUG + DA_V_DIM:(hd + 1) * V_AUG, :] = ones
    sl = 3 * q_slabs
    glu_ref[...] = proj(sl) * jax.nn.sigmoid(proj(sl + 1))
    qm_ref[...] = (proj(sl + 2) * MEM_HEAD_DIM ** -0.5).astype(BF16)


def _mem_kv_kernel(mem_ref, g_ref, w_ref, mk_ref, mv_ref):
    n_mem = mem_ref.shape[1]
    mn = _rms(mem_ref[0], g_ref[...]).astype(BF16)
    kv = jnp.dot(mn, w_ref[...].astype(BF16), preferred_element_type=F32)
    mk_ref[0] = kv[:, :MEM_WIDTH].astype(BF16)
    mv = kv[:, MEM_WIDTH:]
    head_of_lane = lax.broadcasted_iota(jnp.int32, mv.shape, 1) // MEM_HEAD_DIM
    for hd in range(MEM_HEADS):
        mv_ref[0, hd * n_mem:(hd + 1) * n_mem, :] = jnp.where(
            head_of_lane == hd, mv, 0.0).astype(BF16)


def _diff_attn_kernel(lq1_ref, lk1_ref, lq2_ref, lk2_ref, sg_ref,
                      q_ref, k_ref, vt_ref, o_ref, m_ref, acc_ref, s_ref, mpart_ref):
    pair = pl.program_id(1)
    tq = ATTN_TILE
    tk = tq
    per_tile = 2 * DA_HEADS
    n_chain = ATTN_QTILES * per_tile

    lane = lax.broadcasted_iota(jnp.int32, (tq, LANES), 1)
    q_comp = []
    for sub in range(ATTN_QTILES):
        for hd in range(DA_HEADS):
            q = q_ref[0, sub * tq:(sub + 1) * tq, hd * LANES:(hd + 1) * LANES]
            zero = jnp.zeros_like(q)
            q_comp.append(jnp.where(lane < DA_HEAD_DIM, q, zero))
            q_comp.append(jnp.where(lane >= DA_HEAD_DIM, q, zero))

    n_slab = tk // SOFTMAX_SLAB
    rows = lambda r: slice(r * SOFTMAX_SLAB, (r + 1) * SOFTMAX_SLAB)
    head_of = lambda c: (c % per_tile) // 2

    nt_dot = lambda a, b_: lax.dot_general(a, b_, (((1,), (1,)), ((), ())),
                                           preferred_element_type=F32)
    slab_max = lambda s: functools.reduce(
        jnp.maximum, [s[rows(r)] for r in range(s.shape[0] // SOFTMAX_SLAB)])

    def key_block(jb, c):
        start = pl.multiple_of(jb * tk, tk)
        hd = head_of(c)
        return k_ref[0, pl.ds(start, tk), hd * LANES:(hd + 1) * LANES]

    def scores(jb, c, slot, diag):
        if diag:
            return scores_diag(jb, c, slot)
        s = nt_dot(key_block(jb, c), q_comp[c])
        s_ref[slot] = s
        mpart_ref[slot] = slab_max(s)

    def softmax_start(c, slot):
        m_old = m_ref[c]
        m_new = jnp.maximum(m_old, jnp.max(mpart_ref[slot], axis=0, keepdims=True))
        m_ref[c] = m_new
        return jnp.exp2(m_old - m_new), jnp.broadcast_to(m_new, (SOFTMAX_SLAB, tq))

    def probs(slot, slabs, cols, m_slab):
        return jnp.concatenate(
            [jnp.exp2(s_ref[slot, rows(r), cols] - m_slab[:, cols]).astype(BF16)
             for r in slabs], axis=0)

    def softmax_pv(jb, c, slot, diag):
        if diag:
            return softmax_pv_diag(jb, c, slot)
        hd = head_of(c)
        vt = vt_ref[0, jb, hd * V_AUG:(hd + 1) * V_AUG, :]
        alpha, m_slab = softmax_start(c, slot)
        p = probs(slot, range(n_slab), slice(0, tq), m_slab)
        acc_ref[c] = alpha * acc_ref[c] + jnp.dot(vt, p, preferred_element_type=F32)

    half = tk // 2

    def chunk_mask(n_keys, n_queries):
        key_chunk = lax.broadcasted_iota(jnp.int32, (n_keys, n_queries), 0) // CHUNK
        qry_chunk = lax.broadcasted_iota(jnp.int32, (n_keys, n_queries), 1) // CHUNK
        return key_chunk <= qry_chunk

    top_mask, bot_mask = chunk_mask(half, tq), chunk_mask(half, half)
    right = slice(half, tq)

    def scores_diag(jb, c, slot):
        kb = key_block(jb, c)
        s_top = jnp.where(top_mask, nt_dot(kb[0:half], q_comp[c]), NEG_BIG)
        s_bot = jnp.where(bot_mask, nt_dot(kb[half:tk], q_comp[c][half:tq]), NEG_BIG)
        s_ref[slot, 0:half, :] = s_top
        s_ref[slot, half:tk, right] = s_bot
        top, bot = slab_max(s_top), slab_max(s_bot)
        mpart_ref[slot] = jnp.concatenate(
            [top[:, 0:half], jnp.maximum(top[:, right], bot)], axis=1)

    def softmax_pv_diag(jb, c, slot):
        hd = head_of(c)
        vt = vt_ref[0, jb, hd * V_AUG:(hd + 1) * V_AUG, :]
        m_new = jnp.max(mpart_ref[slot], axis=0, keepdims=True)
        m_ref[c] = m_new
        m_slab = jnp.broadcast_to(m_new, (SOFTMAX_SLAB, tq))
        p_top = probs(slot, range(n_slab // 2), slice(0, tq), m_slab)
        p_bot = probs(slot, range(n_slab // 2, n_slab), right, m_slab)
        pv = jnp.dot(vt[:, 0:half], p_top, preferred_element_type=F32)
        pv_right = pv[:, right] + jnp.dot(vt[:, half:tk], p_bot, preferred_element_type=F32)
        acc_ref[c] = jnp.concatenate([pv[:, 0:half], pv_right], axis=1)

    first = ATTN_QTILES * pair
    tasks = []
    for u in reversed(range(ATTN_QTILES)):
        for sub in range(u, ATTN_QTILES):
            for i in range(per_tile):
                tasks.append((first + u, sub * per_tile + i, sub == u))
    assert len(tasks) % SCORE_SLOTS == 0 and n_chain % SCORE_SLOTS == 0
    n_loop = first

    for t in range(SCORE_AHEAD):
        scores(tasks[t][0], tasks[t][1], t % SCORE_SLOTS, tasks[t][2])
    for t, (jb, c, diag) in enumerate(tasks):
        ahead = t + SCORE_AHEAD
        if ahead < len(tasks):
            scores(tasks[ahead][0], tasks[ahead][1], ahead % SCORE_SLOTS, tasks[ahead][2])
        else:
            scores(0, ahead - len(tasks), ahead % SCORE_SLOTS, False)
        softmax_pv(jb, c, t % SCORE_SLOTS, diag)

    def body(j, carry):
        j_next = jnp.minimum(j + 1, n_loop - 1)
        for c in range(n_chain):
            ahead = c + SCORE_AHEAD
            if ahead < n_chain:
                scores(j, ahead, ahead % SCORE_SLOTS, False)
            else:
                scores(j_next, ahead - n_chain, ahead % SCORE_SLOTS, False)
            softmax_pv(j, c, c % SCORE_SLOTS, False)
        return carry

    lax.fori_loop(0, n_loop, body, 0)

    lam = (jnp.exp(jnp.sum(lq1_ref[...] * lk1_ref[...], axis=-1, keepdims=True))
           - jnp.exp(jnp.sum(lq2_ref[...] * lk2_ref[...], axis=-1, keepdims=True))
           + LAM_INIT)
    gain = jnp.broadcast_to(sg_ref[...], (DA_V_DIM, LANES)) * (1.0 - LAM_INIT)
    for sub in range(ATTN_QTILES):
        for hd in range(DA_HEADS):
            c = sub * per_tile + 2 * hd
            for l0 in range(0, tq, LANES):
                a1 = acc_ref[c, :, l0:l0 + LANES]
                a2 = acc_ref[c + 1, :, l0:l0 + LANES]
                w1 = 1.0 / a1[DA_V_DIM:DA_V_DIM + 1]
                w2 = lam / a2[DA_V_DIM:DA_V_DIM + 1]
                o = a1[:DA_V_DIM] * w1 - a2[:DA_V_DIM] * w2
                o = o * lax.rsqrt(jnp.mean(o * o, axis=0, keepdims=True) + EPS) * gain
                o_ref[0, sub * tq + l0:sub * tq + l0 + LANES,
                      hd * DA_V_DIM:(hd + 1) * DA_V_DIM] = o.T.astype(BF16)


def _post_mix_kernel(h_ref, oda_ref, glu_ref, halo_ref, qm_ref, mk_ref, mv_ref,
                     dww_ref, dwb_ref, lng_ref, lnb_ref, wout32_ref,
                     g2_ref, wg32_ref, wu32_ref, wd32_ref, gf_ref,
                     out_ref, wout_ref, wg_ref, wu_ref, wd_ref,
                     cbuf_ref, shift_ref, yconv_ref, pmem_ref, h2_ref, xn_ref, act_ref,
                     *, tiles_per_seq):
    n_stage = wg_ref.shape[0]
    pid = pl.program_id(0)

    @pl.when(pid < n_stage)
    def _():
        _stage_weight_slabs(pid, wg32_ref, wu32_ref, wd32_ref, wg_ref, wu_ref, wd_ref)
        slab = jnp.minimum(pid, wout_ref.shape[0] // FFN_CHUNK - 1)
        rows = pl.ds(pl.multiple_of(slab * FFN_CHUNK, FFN_CHUNK), FFN_CHUNK)
        wout_ref[rows, :] = wout32_ref[...].astype(BF16)

    @pl.when(pid >= n_stage)
    def _():
        _post_mix_step(pid - n_stage, pl.num_programs(0) - n_stage,
                       h_ref, oda_ref, glu_ref, halo_ref, qm_ref, mk_ref, mv_ref,
                       dww_ref, dwb_ref, lng_ref, lnb_ref, wout_ref,
                       g2_ref, wg_ref, wu_ref, wd_ref, gf_ref,
                       out_ref, cbuf_ref, shift_ref, yconv_ref, pmem_ref, h2_ref, xn_ref,
                       act_ref, tiles_per_seq)


def _post_mix_step(step, n_steps, h_ref, oda_ref, glu_ref, halo_ref, qm_ref, mk_ref, mv_ref,
                   dww_ref, dwb_ref, lng_ref, lnb_ref, wout_ref,
                   g2_ref, wg_ref, wu_ref, wd_ref, gf_ref,
                   out_ref, cbuf_ref, shift_ref, yconv_ref, pmem_ref, h2_ref, xn_ref, act_ref,
                   tiles_per_seq):
    tm = h_ref.shape[0]
    n_mem = mk_ref.shape[1]
    n_slabs = wg_ref.shape[0]
    tile = jnp.minimum(step, n_steps - 2)
    seq_start = (tile % tiles_per_seq) == 0
    base = CONV_HALO - (CONV_WIDTH - 1)

    @pl.when(step == 0)
    def _():
        h2_ref[1] = jnp.zeros(h2_ref.shape[1:], F32)

    def after(c):
        tile_ = act_ref[0:2 * SUBLANES, c * FFN_CHUNK:c * FFN_CHUNK + LANES]
        return jnp.sum(tile_.astype(F32) * 0.0, keepdims=True)

    def before_next_slab(done_bf16):
        sl = (slice(0, 2 * SUBLANES), slice(0, LANES))
        xn_ref[sl] = xn_ref[sl] + done_bf16 * jnp.zeros_like(done_bf16)

    def conv_setup(c):
        halo = halo_ref[...]
        cbuf_ref[0:CONV_HALO, :] = jnp.where(seq_start, jnp.zeros_like(halo), halo)
        cbuf_ref[CONV_HALO:CONV_HALO + tm, :] = glu_ref[...]
        span = tm + CONV_HALO - SUBLANES
        for r in range(1, SUBLANES):
            shift_ref[r - 1] = cbuf_ref[r:r + span, :]
        before_next_slab(shift_ref[SUBLANES - 2, 0:2 * SUBLANES, 0:LANES].astype(BF16))

    def conv_rows(r0, wait):
        y = jnp.zeros((CONV_ROWS, CONV_CH), F32) + wait
        for j in range(CONV_WIDTH):
            phase = (base + j) % SUBLANES
            start = base + j - phase + r0
            if phase == 0:
                window = cbuf_ref[start:start + CONV_ROWS, :]
            else:
                window = shift_ref[phase - 1, start:start + CONV_ROWS, :]
            y = y + dww_ref[j:j + 1, :] * window
        y = y + dwb_ref[...]
        mu = jnp.mean(y, axis=-1, keepdims=True)
        yc = y - mu
        var = jnp.mean(yc * yc, axis=-1, keepdims=True)
        y = yc * lax.rsqrt(var + EPS) * lng_ref[...] + lnb_ref[...]
        piece = (y * jax.nn.sigmoid(y)).astype(BF16)
        yconv_ref[r0:r0 + CONV_ROWS, :] = piece
        return piece

    def conv_pieces(r0s, c):
        wait = after(c - 1)
        for r0 in r0s:
            piece = conv_rows(r0, wait)
        before_next_slab(piece[0:2 * SUBLANES, 0:LANES])

    def mem_heads(heads, c):
        qm = qm_ref[...]
        head_of_lane = lax.broadcasted_iota(jnp.int32, qm.shape, 1) // MEM_HEAD_DIM
        wait = after(c - 1)
        for hd in heads:
            qh = jnp.where(head_of_lane == hd, qm, jnp.zeros_like(qm))
            sc = lax.dot_general(qh, mk_ref[0], (((1,), (1,)), ((), ())),
                                 preferred_element_type=F32)
            sc = sc - (jnp.max(sc, axis=-1, keepdims=True) + wait)
            p = jnp.exp(sc)
            p = (p / jnp.sum(p, axis=-1, keepdims=True)).astype(BF16)
            pmem_ref[:, hd * n_mem:(hd + 1) * n_mem] = p
        before_next_slab(p[0:2 * SUBLANES, 0:LANES])

    pieces = [conv_setup]
    pieces += [functools.partial(conv_pieces, (r0,)) for r0 in range(0, tm, CONV_ROWS)]
    pieces += [functools.partial(mem_heads, (0, 1)), functools.partial(mem_heads, (2, 3))]
    assert len(pieces) <= n_slabs, "at most one mixing piece per SwiGLU slab"
    pieces += [None] * (n_slabs - len(pieces))

    h_prev = h2_ref[(step + 1) % 2]
    xn_ref[...] = _rms(h_prev, g2_ref[...]).astype(BF16)

    h2_ref[step % 2] = h_ref[...] + jnp.dot(oda_ref[...], wout_ref[0:DA_WIDTH, :],
                                            preferred_element_type=F32)
    for n in range(n_slabs):
        c0 = n * FFN_CHUNK
        xn = xn_ref[...]
        gate = jnp.dot(xn, wg_ref[n], preferred_element_type=F32)
        up = jnp.dot(xn, wu_ref[n], preferred_element_type=F32)
        if pieces[n] is not None:
            pieces[n](n)
        act_ref[:, c0:c0 + FFN_CHUNK] = (gate * jax.nn.sigmoid(gate) * up).astype(BF16)
    ffn = jnp.dot(act_ref[...], wd_ref[...], preferred_element_type=F32)

    o_mem = jnp.dot(pmem_ref[...], mv_ref[0], preferred_element_type=F32).astype(BF16)
    mix = (jnp.dot(yconv_ref[...], wout_ref[DA_WIDTH:DA_WIDTH + CONV_CH, :],
                   preferred_element_type=F32)
           + jnp.dot(o_mem, wout_ref[DA_WIDTH + CONV_CH:, :], preferred_element_type=F32))
    h2_ref[step % 2] = h2_ref[step % 2] + mix
    out_ref[...] = _rms(h_prev + 0.5 * ffn, gf_ref[...])


def _resident(shape):
    return pl.BlockSpec(shape, lambda *_: (0,) * len(shape), pipeline_mode=pl.Buffered(1))


def _rope_tables(seq):
    half = DA_HEAD_DIM // 2
    inv_freq = 1.0 / (ROPE_THETA ** (jnp.arange(0, DA_HEAD_DIM, 2, dtype=F32) / DA_HEAD_DIM))
    ang = inv_freq[:, None] * jnp.arange(seq, dtype=F32)[None, :]
    assert ang.shape == (half, seq)
    return jnp.cos(ang), jnp.sin(ang)


def kernel(x, mem, ffn1_norm_g, ffn1_w_gate, ffn1_w_up, ffn1_w_down, mix_norm_g, mem_norm_g, w_in, lambda_q1, lambda_k1, lambda_q2, lambda_k2, subln_g, conv_dw_w, conv_dw_b, conv_ln_g, conv_ln_b, w_mem_kv, w_out, ffn2_norm_g, ffn2_w_gate, ffn2_w_up, ffn2_w_down, final_norm_g):
    b, s, d = x.shape
    n_mem = mem.shape[1]
    d_ff = ffn1_w_gate.shape[-1]
    in_width = w_in.shape[-1]
    assert ffn1_norm_g.shape[0] == 1, "single layer"
    tm = TOKEN_TILE
    ta = ATTN_TILE
    assert s % tm == 0 and s % (ATTN_QTILES * ta) == 0 and tm == ta and d_ff % FFN_CHUNK == 0
    t = b * s
    n_tiles = s // tm
    params = functools.partial(pltpu.CompilerParams, vmem_limit_bytes=VMEM_LIMIT)

    cos_t, sin_t = _rope_tables(s)
    row = lambda a: a.reshape(1, -1)

    n_stage = d_ff // FFN_CHUNK
    in_slabs = in_width // MXU_COLS
    assert FFN_CHUNK == MXU_COLS and in_slabs <= n_stage
    tile_of = lambda i: jnp.maximum(i - n_stage, 0)
    slab_of = lambda i: jnp.minimum(i, n_stage - 1)
    tok = lambda w: pl.BlockSpec((tm, w), lambda i: (tile_of(i), 0))
    col_slab = pl.BlockSpec((d, FFN_CHUNK), lambda i: (0, slab_of(i)))
    h1, q, k, vt, glu, qm = pl.pallas_call(
        _pre_mix_kernel,
        grid=(n_stage + t // tm,),
        in_specs=[tok(d), _resident((1, d)), col_slab, col_slab,
                  pl.BlockSpec((FFN_CHUNK, d), lambda i: (slab_of(i), 0)), _resident((1, d)),
                  pl.BlockSpec((d, MXU_COLS), lambda i: (0, jnp.minimum(i, in_slabs - 1))),
                  pl.BlockSpec((DA_HEAD_DIM // 2, tm), lambda i: (0, tile_of(i) % n_tiles)),
                  pl.BlockSpec((DA_HEAD_DIM // 2, tm), lambda i: (0, tile_of(i) % n_tiles))],
        out_specs=[tok(d), tok(DA_WIDTH), tok(DA_WIDTH),
                   pl.BlockSpec((1, DA_HEADS * V_AUG, tm), lambda i: (tile_of(i), 0, 0)),
                   tok(CONV_CH), tok(MEM_WIDTH)],
        out_shape=[jax.ShapeDtypeStruct((t, d), F32),
                   jax.ShapeDtypeStruct((t, DA_WIDTH), BF16),
                   jax.ShapeDtypeStruct((t, DA_WIDTH), BF16),
                   jax.ShapeDtypeStruct((t // tm, DA_HEADS * V_AUG, tm), BF16),
                   jax.ShapeDtypeStruct((t, CONV_CH), F32),
                   jax.ShapeDtypeStruct((t, MEM_WIDTH), BF16)],
        scratch_shapes=[pltpu.VMEM((n_stage, d, FFN_CHUNK), BF16),
                        pltpu.VMEM((n_stage, d, FFN_CHUNK), BF16),
                        pltpu.VMEM((d_ff, d), BF16),
                        pltpu.VMEM((in_slabs, d, MXU_COLS), BF16),
                        pltpu.VMEM((tm, d_ff), BF16)],
        compiler_params=params(dimension_semantics=("arbitrary",)),
        name="pre_mix",
    )(x.reshape(t, d), row(ffn1_norm_g), ffn1_w_gate.reshape(d, d_ff),
      ffn1_w_up.reshape(d, d_ff), ffn1_w_down.reshape(d_ff, d), row(mix_norm_g),
      w_in.reshape(d, in_width), cos_t, sin_t)

    mk, mv = pl.pallas_call(
        _mem_kv_kernel,
        grid=(b,),
        in_specs=[pl.BlockSpec((1, n_mem, d), lambda i: (i, 0, 0)), _resident((1, d)),
                  _resident((d, 2 * MEM_WIDTH))],
        out_specs=[pl.BlockSpec((1, n_mem, MEM_WIDTH), lambda i: (i, 0, 0)),
                   pl.BlockSpec((1, MEM_HEADS * n_mem, MEM_WIDTH), lambda i: (i, 0, 0))],
        out_shape=[jax.ShapeDtypeStruct((b, n_mem, MEM_WIDTH), BF16),
                   jax.ShapeDtypeStruct((b, MEM_HEADS * n_mem, MEM_WIDTH), BF16)],
        compiler_params=params(dimension_semantics=("parallel",)),
        name="mem_kv",
    )(mem, row(mem_norm_g), w_mem_kv.reshape(d, 2 * MEM_WIDTH))

    lam_spec = _resident((1, DA_HEAD_DIM))
    o_da = pl.pallas_call(
        _diff_attn_kernel,
        grid=(b, s // (ATTN_QTILES * ta)),
        in_specs=[lam_spec, lam_spec, lam_spec, lam_spec, _resident((DA_V_DIM, 1)),
                  pl.BlockSpec((1, ATTN_QTILES * ta, DA_WIDTH), lambda bi, qi: (bi, qi, 0)),
                  pl.BlockSpec((1, s, DA_WIDTH), lambda bi, qi: (bi, 0, 0)),
                  pl.BlockSpec((1, s // ta, DA_HEADS * V_AUG, ta),
                               lambda bi, qi: (bi, 0, 0, 0))],
        out_specs=pl.BlockSpec((1, ATTN_QTILES * ta, DA_WIDTH), lambda bi, qi: (bi, qi, 0)),
        out_shape=jax.ShapeDtypeStruct((b, s, DA_WIDTH), BF16),
        scratch_shapes=[pltpu.VMEM((ATTN_QTILES * 2 * DA_HEADS, 1, ta), F32),
                        pltpu.VMEM((ATTN_QTILES * 2 * DA_HEADS, V_AUG, ta), F32),
                        pltpu.VMEM((SCORE_SLOTS, ta, ta), F32),
                        pltpu.VMEM((SCORE_SLOTS, SOFTMAX_SLAB, ta), F32)],
        compiler_params=params(dimension_semantics=("parallel", "arbitrary")),
        name="diff_attn",
    )(lambda_q1, lambda_k1, lambda_q2, lambda_k2, subln_g.reshape(DA_V_DIM, 1),
      q.reshape(b, s, DA_WIDTH), k.reshape(b, s, DA_WIDTH),
      vt.reshape(b, s // ta, DA_HEADS * V_AUG, ta))

    halo_blocks = tm // CONV_HALO
    n_steps = t // tm + 1
    last = t // tm - 1
    out_slabs = d // FFN_CHUNK
    assert out_slabs <= n_stage
    mixed_tile = lambda i: jnp.clip(i - n_stage, 0, last)
    mixed = lambda w: pl.BlockSpec((tm, w), lambda i: (mixed_tile(i), 0))
    per_seq = lambda rows: pl.BlockSpec((1, rows, MEM_WIDTH),
                                        lambda i: (mixed_tile(i) // n_tiles, 0, 0))
    out = pl.pallas_call(
        functools.partial(_post_mix_kernel, tiles_per_seq=n_tiles),
        grid=(n_stage + n_steps,),
        in_specs=[mixed(d), mixed(DA_WIDTH), mixed(CONV_CH),
                  pl.BlockSpec((CONV_HALO, CONV_CH),
                               lambda i: (jnp.maximum(mixed_tile(i) * halo_blocks - 1, 0), 0)),
                  mixed(MEM_WIDTH), per_seq(n_mem), per_seq(MEM_HEADS * n_mem),
                  _resident((CONV_WIDTH, CONV_CH)), _resident((1, CONV_CH)),
                  _resident((1, CONV_CH)), _resident((1, CONV_CH)),
                  pl.BlockSpec((FFN_CHUNK, d), lambda i: (jnp.minimum(i, out_slabs - 1), 0)),
                  _resident((1, d)), col_slab, col_slab,
                  pl.BlockSpec((FFN_CHUNK, d), lambda i: (slab_of(i), 0)), _resident((1, d))],
        out_specs=pl.BlockSpec((tm, d), lambda i: (jnp.maximum(i - n_stage - 1, 0), 0)),
        out_shape=jax.ShapeDtypeStruct((t, d), F32),
        scratch_shapes=[pltpu.VMEM((d, d), BF16),
                        pltpu.VMEM((n_stage, d, FFN_CHUNK), BF16),
                        pltpu.VMEM((n_stage, d, FFN_CHUNK), BF16),
                        pltpu.VMEM((d_ff, d), BF16),
                        pltpu.VMEM((CONV_HALO + tm, CONV_CH), F32),
                        pltpu.VMEM((SUBLANES - 1, tm + CONV_HALO - SUBLANES, CONV_CH), F32),
                        pltpu.VMEM((tm, CONV_CH), BF16),
                        pltpu.VMEM((tm, MEM_HEADS * n_mem), BF16),
                        pltpu.VMEM((2, tm, d), F32),
                        pltpu.VMEM((tm, d), BF16),
                        pltpu.VMEM((tm, d_ff), BF16)],
        compiler_params=params(dimension_semantics=("arbitrary",)),
        name="post_mix",
    )(h1, o_da.reshape(t, DA_WIDTH), glu, glu, qm, mk, mv,
      conv_dw_w.reshape(CONV_WIDTH, CONV_CH), conv_dw_b,
      conv_ln_g, conv_ln_b, w_out.reshape(d, d), row(ffn2_norm_g),
      ffn2_w_gate.reshape(d, d_ff), ffn2_w_up.reshape(d, d_ff),
      ffn2_w_down.reshape(d_ff, d), row(final_norm_g))
    return out.reshape(b, s, d)
```

```python
import functools
import math

import jax
import jax.numpy as jnp
from jax import lax
from jax.experimental import pallas as pl
from jax.experimental.pallas import tpu as pltpu

F32 = jnp.float32
BF16 = jnp.bfloat16

EPS = 1e-5
ROPE_THETA = 10000.0
CHUNK = 64
DA_HEADS = 4
DA_HEAD_DIM = 64
DA_V_DIM = 128
V_AUG = DA_V_DIM + 16
DA_WIDTH = 512
CONV_CH = 256
CONV_WIDTH = 31
MEM_HEADS = 4
MEM_HEAD_DIM = 64
MEM_WIDTH = 256
LAM_INIT = 0.8 - 0.6 * math.exp(-0.3 * 0)

V7X_VMEM_BYTES = 64 * 1024 * 1024
VMEM_LIMIT = V7X_VMEM_BYTES - 8 * 1024 * 1024
LANES = 128
SUBLANES = 8
MXU_COLS = 256

TOKEN_TILE = 512
ATTN_TILE = 512
ATTN_QTILES = 2
FFN_CHUNK = 256
CONV_HALO = 32
CONV_ROWS = 64
SOFTMAX_SLAB = 16
SCORE_AHEAD = 2
SCORE_SLOTS = 4
NEG_BIG = -1e30
LOG2_E = math.log2(math.e)


def _rms(x, g):
    return x * lax.rsqrt(jnp.mean(x * x, axis=-1, keepdims=True) + EPS) * g


def _swiglu(xn, wg_ref, wu_ref, wd_ref, act_ref):
    for n in range(wg_ref.shape[0]):
        c = n * FFN_CHUNK
        gate = jnp.dot(xn, wg_ref[n], preferred_element_type=F32)
        up = jnp.dot(xn, wu_ref[n], preferred_element_type=F32)
        act_ref[:, c:c + FFN_CHUNK] = (gate * jax.nn.sigmoid(gate) * up).astype(BF16)
    return jnp.dot(act_ref[...], wd_ref[...], preferred_element_type=F32)


def _stage_weight_slabs(step, wg32_ref, wu32_ref, wd32_ref, wg_ref, wu_ref, wd_ref):
    wg_ref[step] = wg32_ref[...].astype(BF16)
    wu_ref[step] = wu32_ref[...].astype(BF16)
    rows = pl.ds(pl.multiple_of(step * FFN_CHUNK, FFN_CHUNK), FFN_CHUNK)
    wd_ref[rows, :] = wd32_ref[...].astype(BF16)


def _rope(x, cos, sin_signed, first_half):
    partner = jnp.where(first_half,
                        pltpu.roll(x, LANES - DA_HEAD_DIM // 2, axis=1),
                        pltpu.roll(x, DA_HEAD_DIM // 2, axis=1))
    return x * cos + partner * sin_signed


def _pre_mix_kernel(x_ref, g1_ref, wg32_ref, wu32_ref, wd32_ref, gm_ref, win32_ref,
                    cos_ref, sin_ref,
                    h_ref, q_ref, k_ref, vt_ref, glu_ref, qm_ref,
                    wg_ref, wu_ref, wd_ref, win_ref, act_ref):
    step = pl.program_id(0)
    n_stage = wg_ref.shape[0]

    @pl.when(step < n_stage)
    def _():
        _stage_weight_slabs(step, wg32_ref, wu32_ref, wd32_ref, wg_ref, wu_ref, wd_ref)
        win_ref[jnp.minimum(step, win_ref.shape[0] - 1)] = win32_ref[...].astype(BF16)

    @pl.when(step >= n_stage)
    def _():
        _pre_mix_tile(x_ref, g1_ref, wg_ref, wu_ref, wd_ref, gm_ref, win_ref, cos_ref, sin_ref,
                      h_ref, q_ref, k_ref, vt_ref, glu_ref, qm_ref, act_ref)


def _pre_mix_tile(x_ref, g1_ref, wg_ref, wu_ref, wd_ref, gm_ref, win_ref, cos_ref, sin_ref,
                  h_ref, q_ref, k_ref, vt_ref, glu_ref, qm_ref, act_ref):
    x = x_ref[...]
    xn = _rms(x, g1_ref[...]).astype(BF16)
    h = x + 0.5 * _swiglu(xn, wg_ref, wu_ref, wd_ref, act_ref)
    h_ref[...] = h

    n = _rms(h, gm_ref[...]).astype(BF16)
    proj = lambda slab: jnp.dot(n, win_ref[slab], preferred_element_type=F32)
    reps = LANES // cos_ref.shape[0]
    cos = jnp.tile(cos_ref[...].T, (1, reps))
    sin = jnp.tile(sin_ref[...].T, (1, reps))
    lane = lax.broadcasted_iota(jnp.int32, cos.shape, 1)
    first_half = (lane % DA_HEAD_DIM) < (DA_HEAD_DIM // 2)
    sin_signed = jnp.where(first_half, -sin, sin)
    scale = DA_HEAD_DIM ** -0.5 * LOG2_E
    q_slabs = DA_WIDTH // MXU_COLS
    for sl in range(q_slabs):
        c0 = sl * MXU_COLS
        qs = proj(sl)
        ks = proj(q_slabs + sl)
        for l0 in range(0, MXU_COLS, LANES):
            q_ref[:, c0 + l0:c0 + l0 + LANES] = (
                _rope(qs[:, l0:l0 + LANES], cos, sin_signed, first_half) * scale).astype(BF16)
            k_ref[:, c0 + l0:c0 + l0 + LANES] = _rope(
                ks[:, l0:l0 + LANES], cos, sin_signed, first_half).astype(BF16)
    vt = jnp.concatenate([proj(2 * q_slabs + sl).T for sl in range(q_slabs)],
                         axis=0).astype(BF16)
    ones = jnp.ones((V_AUG - DA_V_DIM, vt.shape[1]), BF16)
    for hd in range(DA_HEADS):
        vt_ref[0, hd * V_AUG:hd * V_AUG + DA_V_DIM, :] = vt[hd * DA_V_DIM:(hd + 1) * DA_V_DIM]
        vt_ref[0, hd * V_AUG + DA_V_DIM:(hd + 1) * V_AUG, :] = ones
    sl = 3 * q_slabs
    glu_ref[...] = proj(sl) * jax.nn.sigmoid(proj(sl + 1))
    qm_ref[...] = (proj(sl + 2) * MEM_HEAD_DIM ** -0.5).astype(BF16)


def _mem_kv_kernel(mem_ref, g_ref, w_ref, mk_ref, mv_ref):
    n_mem = mem_ref.shape[1]
    mn = _rms(mem_ref[0], g_ref[...]).astype(BF16)
    kv = jnp.dot(mn, w_ref[...].astype(BF16), preferred_element_type=F32)
    mk_ref[0] = kv[:, :MEM_WIDTH].astype(BF16)
    mv = kv[:, MEM_WIDTH:]
    head_of_lane = lax.broadcasted_iota(jnp.int32, mv.shape, 1) // MEM_HEAD_DIM
    for hd in range(MEM_HEADS):
        mv_ref[0, hd * n_mem:(hd + 1) * n_mem, :] = jnp.where(
            head_of_lane == hd, mv, 0.0).astype(BF16)


def _diff_attn_kernel(lq1_ref, lk1_ref, lq2_ref, lk2_ref, sg_ref,
                      q_ref, k_ref, vt_ref, o_ref, m_ref, acc_ref, s_ref, mpart_ref):
    pair = pl.program_id(1)
    tq = ATTN_TILE
    tk = tq
    per_tile = 2 * DA_HEADS
    n_chain = ATTN_QTILES * per_tile

    lane = lax.broadcasted_iota(jnp.int32, (tq, LANES), 1)
    q_comp = []
    for sub in range(ATTN_QTILES):
        for hd in range(DA_HEADS):
            q = q_ref[0, sub * tq:(sub + 1) * tq, hd * LANES:(hd + 1) * LANES]
            zero = jnp.zeros_like(q)
            q_comp.append(jnp.where(lane < DA_HEAD_DIM, q, zero))
            q_comp.append(jnp.where(lane >= DA_HEAD_DIM, q, zero))

    n_slab = tk // SOFTMAX_SLAB
    rows = lambda r: slice(r * SOFTMAX_SLAB, (r + 1) * SOFTMAX_SLAB)
    head_of = lambda c: (c % per_tile) // 2

    nt_dot = lambda a, b_: lax.dot_general(a, b_, (((1,), (1,)), ((), ())),
                                           preferred_element_type=F32)
    slab_max = lambda s: functools.reduce(
        jnp.maximum, [s[rows(r)] for r in range(s.shape[0] // SOFTMAX_SLAB)])

    def key_block(jb, c):
        start = pl.multiple_of(jb * tk, tk)
        hd = head_of(c)
        return k_ref[0, pl.ds(start, tk), hd * LANES:(hd + 1) * LANES]

    def scores(jb, c, slot, diag):
        if diag:
            return scores_diag(jb, c, slot)
        s = nt_dot(key_block(jb, c), q_comp[c])
        s_ref[slot] = s
        mpart_ref[slot] = slab_max(s)

    def softmax_start(c, slot):
        m_old = m_ref[c]
        m_new = jnp.maximum(m_old, jnp.max(mpart_ref[slot], axis=0, keepdims=True))
        m_ref[c] = m_new
        return jnp.exp2(m_old - m_new), jnp.broadcast_to(m_new, (SOFTMAX_SLAB, tq))

    def probs(slot, slabs, cols, m_slab):
        return jnp.concatenate(
            [jnp.exp2(s_ref[slot, rows(r), cols] - m_slab[:, cols]).astype(BF16)
             for r in slabs], axis=0)

    def softmax_pv(jb, c, slot, diag):
        if diag:
            return softmax_pv_diag(jb, c, slot)
        hd = head_of(c)
        vt = vt_ref[0, jb, hd * V_AUG:(hd + 1) * V_AUG, :]
        alpha, m_slab = softmax_start(c, slot)
        p = probs(slot, range(n_slab), slice(0, tq), m_slab)
        acc_ref[c] = alpha * acc_ref[c] + jnp.dot(vt, p, preferred_element_type=F32)

    half = tk // 2

    def chunk_mask(n_keys, n_queries):
        key_chunk = lax.broadcasted_iota(jnp.int32, (n_keys, n_queries), 0) // CHUNK
        qry_chunk = lax.broadcasted_iota(jnp.int32, (n_keys, n_queries), 1) // CHUNK
        return key_chunk <= qry_chunk

    top_mask, bot_mask = chunk_mask(half, tq), chunk_mask(half, half)
    right = slice(half, tq)

    def scores_diag(jb, c, slot):
        kb = key_block(jb, c)
        s_top = jnp.where(top_mask, nt_dot(kb[0:half], q_comp[c]), NEG_BIG)
        s_bot = jnp.where(bot_mask, nt_dot(kb[half:tk], q_comp[c][half:tq]), NEG_BIG)
        s_ref[slot, 0:half, :] = s_top
        s_ref[slot, half:tk, right] = s_bot
        top, bot = slab_max(s_top), slab_max(s_bot)
        mpart_ref[slot] = jnp.concatenate(
            [top[:, 0:half], jnp.maximum(top[:, right], bot)], axis=1)

    def softmax_pv_diag(jb, c, slot):
        hd = head_of(c)
        vt = vt_ref[0, jb, hd * V_AUG:(hd + 1) * V_AUG, :]
        m_new = jnp.max(mpart_ref[slot], axis=0, keepdims=True)
        m_ref[c] = m_new
        m_slab = jnp.broadcast_to(m_new, (SOFTMAX_SLAB, tq))
        p_top = probs(slot, range(n_slab // 2), slice(0, tq), m_slab)
        p_bot = probs(slot, range(n_slab // 2, n_slab), right, m_slab)
        pv = jnp.dot(vt[:, 0:half], p_top, preferred_element_type=F32)
        pv_right = pv[:, right] + jnp.dot(vt[:, half:tk], p_bot, preferred_element_type=F32)
        acc_ref[c] = jnp.concatenate([pv[:, 0:half], pv_right], axis=1)

    first = ATTN_QTILES * pair
    tasks = []
    for u in reversed(range(ATTN_QTILES)):
        for sub in range(u, ATTN_QTILES):
            for i in range(per_tile):
                tasks.append((first + u, sub * per_tile + i, sub == u))
    assert len(tasks) % SCORE_SLOTS == 0 and n_chain % SCORE_SLOTS == 0
    n_loop = first

    for t in range(SCORE_AHEAD):
        scores(tasks[t][0], tasks[t][1], t % SCORE_SLOTS, tasks[t][2])
    for t, (jb, c, diag) in enumerate(tasks):
        ahead = t + SCORE_AHEAD
        if ahead < len(tasks):
            scores(tasks[ahead][0], tasks[ahead][1], ahead % SCORE_SLOTS, tasks[ahead][2])
        else:
            scores(0, ahead - len(tasks), ahead % SCORE_SLOTS, False)
        softmax_pv(jb, c, t % SCORE_SLOTS, diag)

    def body(j, carry):
        j_next = jnp.minimum(j + 1, n_loop - 1)
        for c in range(n_chain):
            ahead = c + SCORE_AHEAD
            if ahead < n_chain:
                scores(j, ahead, ahead % SCORE_SLOTS, False)
            else:
                scores(j_next, ahead - n_chain, ahead % SCORE_SLOTS, False)
            softmax_pv(j, c, c % SCORE_SLOTS, False)
        return carry

    lax.fori_loop(0, n_loop, body, 0)

    lam = (jnp.exp(jnp.sum(lq1_ref[...] * lk1_ref[...], axis=-1, keepdims=True))
           - jnp.exp(jnp.sum(lq2_ref[...] * lk2_ref[...], axis=-1, keepdims=True))
           + LAM_INIT)
    gain = jnp.broadcast_to(sg_ref[...], (DA_V_DIM, LANES)) * (1.0 - LAM_INIT)
    for sub in range(ATTN_QTILES):
        for hd in range(DA_HEADS):
            c = sub * per_tile + 2 * hd
            for l0 in range(0, tq, LANES):
                a1 = acc_ref[c, :, l0:l0 + LANES]
                a2 = acc_ref[c + 1, :, l0:l0 + LANES]
                w1 = 1.0 / a1[DA_V_DIM:DA_V_DIM + 1]
                w2 = lam / a2[DA_V_DIM:DA_V_DIM + 1]
                o = a1[:DA_V_DIM] * w1 - a2[:DA_V_DIM] * w2
                o = o * lax.rsqrt(jnp.mean(o * o, axis=0, keepdims=True) + EPS) * gain
                o_ref[0, sub * tq + l0:sub * tq + l0 + LANES,
                      hd * DA_V_DIM:(hd + 1) * DA_V_DIM] = o.T.astype(BF16)


def _post_mix_kernel(h_ref, oda_ref, glu_ref, halo_ref, qm_ref, mk_ref, mv_ref,
                     dww_ref, dwb_ref, lng_ref, lnb_ref, wout32_ref,
                     g2_ref, wg32_ref, wu32_ref, wd32_ref, gf_ref,
                     out_ref, wout_ref, wg_ref, wu_ref, wd_ref,
                     cbuf_ref, shift_ref, yconv_ref, pmem_ref, h2_ref, xn_ref, act_ref,
                     *, tiles_per_seq):
    n_stage = wg_ref.shape[0]
    pid = pl.program_id(0)
    n_steps = pl.num_programs(0) - n_stage + 1
    refs = (h_ref, oda_ref, glu_ref, halo_ref, qm_ref, mk_ref, mv_ref,
            dww_ref, dwb_ref, lng_ref, lnb_ref, wout_ref,
            g2_ref, wg_ref, wu_ref, wd_ref, gf_ref,
            out_ref, cbuf_ref, shift_ref, yconv_ref, pmem_ref, h2_ref, xn_ref, act_ref)

    @pl.when(pid < n_stage)
    def _():
        _stage_weight_slabs(pid, wg32_ref, wu32_ref, wd32_ref, wg_ref, wu_ref, wd_ref)
        slab = jnp.minimum(pid, wout_ref.shape[0] // FFN_CHUNK - 1)
        rows = pl.ds(pl.multiple_of(slab * FFN_CHUNK, FFN_CHUNK), FFN_CHUNK)
        wout_ref[rows, :] = wout32_ref[...].astype(BF16)

    @pl.when(pid == n_stage - 1)
    def _():
        _post_mix_step(0, n_steps, *refs, tiles_per_seq, with_swiglu=False)

    @pl.when(pid >= n_stage)
    def _():
        _post_mix_step(pid - n_stage + 1, n_steps, *refs, tiles_per_seq, with_swiglu=True)


def _post_mix_step(step, n_steps, h_ref, oda_ref, glu_ref, halo_ref, qm_ref, mk_ref, mv_ref,
                   dww_ref, dwb_ref, lng_ref, lnb_ref, wout_ref,
                   g2_ref, wg_ref, wu_ref, wd_ref, gf_ref,
                   out_ref, cbuf_ref, shift_ref, yconv_ref, pmem_ref, h2_ref, xn_ref, act_ref,
                   tiles_per_seq, with_swiglu):
    tm = h_ref.shape[0]
    n_mem = mk_ref.shape[1]
    n_slabs = wg_ref.shape[0]
    tile = jnp.minimum(step, n_steps - 2)
    seq_start = (tile % tiles_per_seq) == 0
    base = CONV_HALO - (CONV_WIDTH - 1)

    def after(c):
        tile_ = act_ref[0:2 * SUBLANES, c * FFN_CHUNK:c * FFN_CHUNK + LANES]
        return jnp.sum(tile_.astype(F32) * 0.0, keepdims=True)

    def before_next_slab(done_bf16):
        sl = (slice(0, 2 * SUBLANES), slice(0, LANES))
        xn_ref[sl] = xn_ref[sl] + done_bf16 * jnp.zeros_like(done_bf16)

    def conv_setup(c):
        halo = halo_ref[...]
        cbuf_ref[0:CONV_HALO, :] = jnp.where(seq_start, jnp.zeros_like(halo), halo)
        cbuf_ref[CONV_HALO:CONV_HALO + tm, :] = glu_ref[...]
        span = tm + CONV_HALO - SUBLANES
        for r in range(1, SUBLANES):
            shift_ref[r - 1] = cbuf_ref[r:r + span, :]
        before_next_slab(shift_ref[SUBLANES - 2, 0:2 * SUBLANES, 0:LANES].astype(BF16))

    def conv_rows(r0, wait):
        y = jnp.zeros((CONV_ROWS, CONV_CH), F32) + wait
        for j in range(CONV_WIDTH):
            phase = (base + j) % SUBLANES
            start = base + j - phase + r0
            if phase == 0:
                window = cbuf_ref[start:start + CONV_ROWS, :]
            else:
                window = shift_ref[phase - 1, start:start + CONV_ROWS, :]
            y = y + dww_ref[j:j + 1, :] * window
        y = y + dwb_ref[...]
        mu = jnp.mean(y, axis=-1, keepdims=True)
        yc = y - mu
        var = jnp.mean(yc * yc, axis=-1, keepdims=True)
        y = yc * lax.rsqrt(var + EPS) * lng_ref[...] + lnb_ref[...]
        piece = (y * jax.nn.sigmoid(y)).astype(BF16)
        yconv_ref[r0:r0 + CONV_ROWS, :] = piece
        return piece

    def conv_pieces(r0s, c):
        wait = after(c - 1)
        for r0 in r0s:
            piece = conv_rows(r0, wait)
        before_next_slab(piece[0:2 * SUBLANES, 0:LANES])

    def mem_heads(heads, c):
        qm = qm_ref[...]
        head_of_lane = lax.broadcasted_iota(jnp.int32, qm.shape, 1) // MEM_HEAD_DIM
        wait = after(c - 1)
        for hd in heads:
            qh = jnp.where(head_of_lane == hd, qm, jnp.zeros_like(qm))
            sc = lax.dot_general(qh, mk_ref[0], (((1,), (1,)), ((), ())),
                                 preferred_element_type=F32)
            sc = sc - (jnp.max(sc, axis=-1, keepdims=True) + wait)
            p = jnp.exp(sc)
            p = (p / jnp.sum(p, axis=-1, keepdims=True)).astype(BF16)
            pmem_ref[:, hd * n_mem:(hd + 1) * n_mem] = p
        before_next_slab(p[0:2 * SUBLANES, 0:LANES])

    pieces = [conv_setup]
    pieces += [functools.partial(conv_pieces, (r0,)) for r0 in range(0, tm, CONV_ROWS)]
    pieces += [functools.partial(mem_heads, (0, 1)), functools.partial(mem_heads, (2, 3))]
    assert len(pieces) <= n_slabs, "at most one mixing piece per SwiGLU slab"
    pieces += [None] * (n_slabs - len(pieces))

    def finish_mix():
        o_mem = jnp.dot(pmem_ref[...], mv_ref[0], preferred_element_type=F32).astype(BF16)
        mix = (jnp.dot(yconv_ref[...], wout_ref[DA_WIDTH:DA_WIDTH + CONV_CH, :],
                       preferred_element_type=F32)
               + jnp.dot(o_mem, wout_ref[DA_WIDTH + CONV_CH:, :], preferred_element_type=F32))
        h2_ref[step % 2] = h2_ref[step % 2] + mix

    if not with_swiglu:
        xn_ref[0:2 * SUBLANES, 0:LANES] = jnp.zeros((2 * SUBLANES, LANES), BF16)
        act_ref[0:2 * SUBLANES, :] = jnp.zeros((2 * SUBLANES, act_ref.shape[1]), BF16)
        h2_ref[step % 2] = h_ref[...] + jnp.dot(oda_ref[...], wout_ref[0:DA_WIDTH, :],
                                                preferred_element_type=F32)
        for n, piece in enumerate(pieces):
            if piece is not None:
                piece(n)
        finish_mix()
        return

    h_prev = h2_ref[(step + 1) % 2]
    xn_ref[...] = _rms(h_prev, g2_ref[...]).astype(BF16)

    h2_ref[step % 2] = h_ref[...] + jnp.dot(oda_ref[...], wout_ref[0:DA_WIDTH, :],
                                            preferred_element_type=F32)
    for n in range(n_slabs):
        c0 = n * FFN_CHUNK
        xn = xn_ref[...]
        gate = jnp.dot(xn, wg_ref[n], preferred_element_type=F32)
        up = jnp.dot(xn, wu_ref[n], preferred_element_type=F32)
        if pieces[n] is not None:
            pieces[n](n)
        act_ref[:, c0:c0 + FFN_CHUNK] = (gate * jax.nn.sigmoid(gate) * up).astype(BF16)
    ffn = jnp.dot(act_ref[...], wd_ref[...], preferred_element_type=F32)

    finish_mix()
    out_ref[...] = _rms(h_prev + 0.5 * ffn, gf_ref[...])


def _resident(shape):
    return pl.BlockSpec(shape, lambda *_: (0,) * len(shape), pipeline_mode=pl.Buffered(1))


def _rope_tables(seq):
    half = DA_HEAD_DIM // 2
    inv_freq = 1.0 / (ROPE_THETA ** (jnp.arange(0, DA_HEAD_DIM, 2, dtype=F32) / DA_HEAD_DIM))
    ang = inv_freq[:, None] * jnp.arange(seq, dtype=F32)[None, :]
    assert ang.shape == (half, seq)
    return jnp.cos(ang), jnp.sin(ang)


def kernel(x, mem, ffn1_norm_g, ffn1_w_gate, ffn1_w_up, ffn1_w_down, mix_norm_g, mem_norm_g, w_in, lambda_q1, lambda_k1, lambda_q2, lambda_k2, subln_g, conv_dw_w, conv_dw_b, conv_ln_g, conv_ln_b, w_mem_kv, w_out, ffn2_norm_g, ffn2_w_gate, ffn2_w_up, ffn2_w_down, final_norm_g):
    b, s, d = x.shape
    n_mem = mem.shape[1]
    d_ff = ffn1_w_gate.shape[-1]
    in_width = w_in.shape[-1]
    assert ffn1_norm_g.shape[0] == 1, "single layer"
    tm = TOKEN_TILE
    ta = ATTN_TILE
    assert s % tm == 0 and s % (ATTN_QTILES * ta) == 0 and tm == ta and d_ff % FFN_CHUNK == 0
    t = b * s
    n_tiles = s // tm
    params = functools.partial(pltpu.CompilerParams, vmem_limit_bytes=VMEM_LIMIT)

    cos_t, sin_t = _rope_tables(s)
    row = lambda a: a.reshape(1, -1)

    n_stage = d_ff // FFN_CHUNK
    in_slabs = in_width // MXU_COLS
    assert FFN_CHUNK == MXU_COLS and in_slabs <= n_stage
    tile_of = lambda i: jnp.maximum(i - n_stage, 0)
    slab_of = lambda i: jnp.minimum(i, n_stage - 1)
    tok = lambda w: pl.BlockSpec((tm, w), lambda i: (tile_of(i), 0))
    col_slab = pl.BlockSpec((d, FFN_CHUNK), lambda i: (0, slab_of(i)))
    h1, q, k, vt, glu, qm = pl.pallas_call(
        _pre_mix_kernel,
        grid=(n_stage + t // tm,),
        in_specs=[tok(d), _resident((1, d)), col_slab, col_slab,
                  pl.BlockSpec((FFN_CHUNK, d), lambda i: (slab_of(i), 0)), _resident((1, d)),
                  pl.BlockSpec((d, MXU_COLS), lambda i: (0, jnp.minimum(i, in_slabs - 1))),
                  pl.BlockSpec((DA_HEAD_DIM // 2, tm), lambda i: (0, tile_of(i) % n_tiles)),
                  pl.BlockSpec((DA_HEAD_DIM // 2, tm), lambda i: (0, tile_of(i) % n_tiles))],
        out_specs=[tok(d), tok(DA_WIDTH), tok(DA_WIDTH),
                   pl.BlockSpec((1, DA_HEADS * V_AUG, tm), lambda i: (tile_of(i), 0, 0)),
                   tok(CONV_CH), tok(MEM_WIDTH)],
        out_shape=[jax.ShapeDtypeStruct((t, d), F32),
                   jax.ShapeDtypeStruct((t, DA_WIDTH), BF16),
                   jax.ShapeDtypeStruct((t, DA_WIDTH), BF16),
                   jax.ShapeDtypeStruct((t // tm, DA_HEADS * V_AUG, tm), BF16),
                   jax.ShapeDtypeStruct((t, CONV_CH), F32),
                   jax.ShapeDtypeStruct((t, MEM_WIDTH), BF16)],
        scratch_shapes=[pltpu.VMEM((n_stage, d, FFN_CHUNK), BF16),
                        pltpu.VMEM((n_stage, d, FFN_CHUNK), BF16),
                        pltpu.VMEM((d_ff, d), BF16),
                        pltpu.VMEM((in_slabs, d, MXU_COLS), BF16),
                        pltpu.VMEM((tm, d_ff), BF16)],
        compiler_params=params(dimension_semantics=("arbitrary",)),
        name="pre_mix",
    )(x.reshape(t, d), row(ffn1_norm_g), ffn1_w_gate.reshape(d, d_ff),
      ffn1_w_up.reshape(d, d_ff), ffn1_w_down.reshape(d_ff, d), row(mix_norm_g),
      w_in.reshape(d, in_width), cos_t, sin_t)

    mk, mv = pl.pallas_call(
        _mem_kv_kernel,
        grid=(b,),
        in_specs=[pl.BlockSpec((1, n_mem, d), lambda i: (i, 0, 0)), _resident((1, d)),
                  _resident((d, 2 * MEM_WIDTH))],
        out_specs=[pl.BlockSpec((1, n_mem, MEM_WIDTH), lambda i: (i, 0, 0)),
                   pl.BlockSpec((1, MEM_HEADS * n_mem, MEM_WIDTH), lambda i: (i, 0, 0))],
        out_shape=[jax.ShapeDtypeStruct((b, n_mem, MEM_WIDTH), BF16),
                   jax.ShapeDtypeStruct((b, MEM_HEADS * n_mem, MEM_WIDTH), BF16)],
        compiler_params=params(dimension_semantics=("parallel",)),
        name="mem_kv",
    )(mem, row(mem_norm_g), w_mem_kv.reshape(d, 2 * MEM_WIDTH))

    lam_spec = _resident((1, DA_HEAD_DIM))
    o_da = pl.pallas_call(
        _diff_attn_kernel,
        grid=(b, s // (ATTN_QTILES * ta)),
        in_specs=[lam_spec, lam_spec, lam_spec, lam_spec, _resident((DA_V_DIM, 1)),
                  pl.BlockSpec((1, ATTN_QTILES * ta, DA_WIDTH), lambda bi, qi: (bi, qi, 0)),
                  pl.BlockSpec((1, s, DA_WIDTH), lambda bi, qi: (bi, 0, 0)),
                  pl.BlockSpec((1, s // ta, DA_HEADS * V_AUG, ta),
                               lambda bi, qi: (bi, 0, 0, 0))],
        out_specs=pl.BlockSpec((1, ATTN_QTILES * ta, DA_WIDTH), lambda bi, qi: (bi, qi, 0)),
        out_shape=jax.ShapeDtypeStruct((b, s, DA_WIDTH), BF16),
        scratch_shapes=[pltpu.VMEM((ATTN_QTILES * 2 * DA_HEADS, 1, ta), F32),
                        pltpu.VMEM((ATTN_QTILES * 2 * DA_HEADS, V_AUG, ta), F32),
                        pltpu.VMEM((SCORE_SLOTS, ta, ta), F32),
                        pltpu.VMEM((SCORE_SLOTS, SOFTMAX_SLAB, ta), F32)],
        compiler_params=params(dimension_semantics=("parallel", "arbitrary")),
        name="diff_attn",
    )(lambda_q1, lambda_k1, lambda_q2, lambda_k2, subln_g.reshape(DA_V_DIM, 1),
      q.reshape(b, s, DA_WIDTH), k.reshape(b, s, DA_WIDTH),
      vt.reshape(b, s // ta, DA_HEADS * V_AUG, ta))

    halo_blocks = tm // CONV_HALO
    last = t // tm - 1
    out_slabs = d // FFN_CHUNK
    assert out_slabs < n_stage
    mixed_tile = lambda i: jnp.clip(i - n_stage + 1, 0, last)
    mixed = lambda w: pl.BlockSpec((tm, w), lambda i: (mixed_tile(i), 0))
    per_seq = lambda rows: pl.BlockSpec((1, rows, MEM_WIDTH),
                                        lambda i: (mixed_tile(i) // n_tiles, 0, 0))
    out = pl.pallas_call(
        functools.partial(_post_mix_kernel, tiles_per_seq=n_tiles),
        grid=(n_stage + t // tm,),
        in_specs=[mixed(d), mixed(DA_WIDTH), mixed(CONV_CH),
                  pl.BlockSpec((CONV_HALO, CONV_CH),
                               lambda i: (jnp.maximum(mixed_tile(i) * halo_blocks - 1, 0), 0)),
                  mixed(MEM_WIDTH), per_seq(n_mem), per_seq(MEM_HEADS * n_mem),
                  _resident((CONV_WIDTH, CONV_CH)), _resident((1, CONV_CH)),
                  _resident((1, CONV_CH)), _resident((1, CONV_CH)),
                  pl.BlockSpec((FFN_CHUNK, d), lambda i: (jnp.minimum(i, out_slabs - 1), 0),
                               pipeline_mode=pl.Buffered(1)),
                  _resident((1, d)), col_slab, col_slab,
                  pl.BlockSpec((FFN_CHUNK, d), lambda i: (slab_of(i), 0)), _resident((1, d))],
        out_specs=pl.BlockSpec((tm, d), lambda i: (jnp.maximum(i - n_stage, 0), 0)),
        out_shape=jax.ShapeDtypeStruct((t, d), F32),
        scratch_shapes=[pltpu.VMEM((d, d), BF16),
                        pltpu.VMEM((n_stage, d, FFN_CHUNK), BF16),
                        pltpu.VMEM((n_stage, d, FFN_CHUNK), BF16),
                        pltpu.VMEM((d_ff, d), BF16),
                        pltpu.VMEM((CONV_HALO + tm, CONV_CH), F32),
                        pltpu.VMEM((SUBLANES - 1, tm + CONV_HALO - SUBLANES, CONV_CH), F32),
                        pltpu.VMEM((tm, CONV_CH), BF16),
                        pltpu.VMEM((tm, MEM_HEADS * n_mem), BF16),
                        pltpu.VMEM((2, tm, d), F32),
                        pltpu.VMEM((tm, d), BF16),
                        pltpu.VMEM((tm, d_ff), BF16)],
        compiler_params=params(dimension_semantics=("arbitrary",)),
        name="post_mix",
    )(h1, o_da.reshape(t, DA_WIDTH), glu, glu, qm, mk, mv,
      conv_dw_w.reshape(CONV_WIDTH, CONV_CH), conv_dw_b,
      conv_ln_g, conv_ln_b, w_out.reshape(d, d), row(ffn2_norm_g),
      ffn2_w_gate.reshape(d, d_ff), ffn2_w_up.reshape(d, d_ff),
      ffn2_w_down.reshape(d_ff, d), row(final_norm_g))
    return out.reshape(b, s, d)
```

```python
import functools
import math

import jax
import jax.numpy as jnp
from jax import lax
from jax.experimental import pallas as pl
from jax.experimental.pallas import tpu as pltpu

F32 = jnp.float32
BF16 = jnp.bfloat16

EPS = 1e-5
ROPE_THETA = 10000.0
CHUNK = 64
DA_HEADS = 4
DA_HEAD_DIM = 64
DA_V_DIM = 128
V_AUG = DA_V_DIM + 16
DA_WIDTH = 512
CONV_CH = 256
CONV_WIDTH = 31
MEM_HEADS = 4
MEM_HEAD_DIM = 64
MEM_WIDTH = 256
LAM_INIT = 0.8 - 0.6 * math.exp(-0.3 * 0)

V7X_VMEM_BYTES = 64 * 1024 * 1024
VMEM_LIMIT = V7X_VMEM_BYTES - 8 * 1024 * 1024
LANES = 128
SUBLANES = 8
MXU_COLS = 256

TOKEN_TILE = 512
ATTN_TILE = 512
ATTN_QTILES = 2
FFN_CHUNK = 256
CONV_HALO = 32
CONV_ROWS = 64
SOFTMAX_SLAB = 16
SCORE_AHEAD = 2
SCORE_SLOTS = 4
NEG_BIG = -1e30
LOG2_E = math.log2(math.e)


def _rms(x, g):
    return x * lax.rsqrt(jnp.mean(x * x, axis=-1, keepdims=True) + EPS) * g


def _swiglu(xn, wg_ref, wu_ref, wd_ref, act_ref, first_slab=0):
    for n in range(first_slab, wg_ref.shape[0]):
        c = n * FFN_CHUNK
        gate = jnp.dot(xn, wg_ref[n], preferred_element_type=F32)
        up = jnp.dot(xn, wu_ref[n], preferred_element_type=F32)
        act_ref[:, c:c + FFN_CHUNK] = (gate * jax.nn.sigmoid(gate) * up).astype(BF16)
    return jnp.dot(act_ref[...], wd_ref[...], preferred_element_type=F32)


def _stage_weight_slabs(step, wg32_ref, wu32_ref, wd32_ref, wg_ref, wu_ref, wd_ref):
    wg_ref[step] = wg32_ref[...].astype(BF16)
    wu_ref[step] = wu32_ref[...].astype(BF16)
    rows = pl.ds(pl.multiple_of(step * FFN_CHUNK, FFN_CHUNK), FFN_CHUNK)
    wd_ref[rows, :] = wd32_ref[...].astype(BF16)


def _rope(x, cos, sin_signed, first_half):
    partner = jnp.where(first_half,
                        pltpu.roll(x, LANES - DA_HEAD_DIM // 2, axis=1),
                        pltpu.roll(x, DA_HEAD_DIM // 2, axis=1))
    return x * cos + partner * sin_signed


def _pre_mix_kernel(x_ref, g1_ref, wg32_ref, wu32_ref, wd32_ref, gm_ref, win32_ref,
                    cos_ref, sin_ref,
                    h_ref, q_ref, k_ref, vt_ref, glu_ref, qm_ref,
                    wg_ref, wu_ref, wd_ref, win_ref, act_ref):
    step = pl.program_id(0)
    n_stage = wg_ref.shape[0]

    @pl.when(step < n_stage)
    def _():
        _stage_weight_slabs(step, wg32_ref, wu32_ref, wd32_ref, wg_ref, wu_ref, wd_ref)
        win_ref[jnp.minimum(step, win_ref.shape[0] - 1)] = win32_ref[...].astype(BF16)

    @pl.when(step >= n_stage)
    def _():
        _pre_mix_tile(x_ref, g1_ref, wg_ref, wu_ref, wd_ref, gm_ref, win_ref, cos_ref, sin_ref,
                      h_ref, q_ref, k_ref, vt_ref, glu_ref, qm_ref, act_ref)


def _pre_mix_tile(x_ref, g1_ref, wg_ref, wu_ref, wd_ref, gm_ref, win_ref, cos_ref, sin_ref,
                  h_ref, q_ref, k_ref, vt_ref, glu_ref, qm_ref, act_ref):
    x = x_ref[...]
    half_rows = x.shape[0] // 2
    xn_halves = []
    for r0 in range(0, x.shape[0], half_rows):
        xh = _rms(x[r0:r0 + half_rows], g1_ref[...]).astype(BF16)
        gate = jnp.dot(xh, wg_ref[0], preferred_element_type=F32)
        up = jnp.dot(xh, wu_ref[0], preferred_element_type=F32)
        act_ref[r0:r0 + half_rows, 0:FFN_CHUNK] = (gate * jax.nn.sigmoid(gate) * up).astype(BF16)
        xn_halves.append(xh)
    xn = jnp.concatenate(xn_halves, axis=0)
    h = x + 0.5 * _swiglu(xn, wg_ref, wu_ref, wd_ref, act_ref, first_slab=1)
    h_ref[...] = h

    n = _rms(h, gm_ref[...]).astype(BF16)
    proj = lambda slab: jnp.dot(n, win_ref[slab], preferred_element_type=F32)
    reps = LANES // cos_ref.shape[0]
    cos = jnp.tile(cos_ref[...].T, (1, reps))
    sin = jnp.tile(sin_ref[...].T, (1, reps))
    lane = lax.broadcasted_iota(jnp.int32, cos.shape, 1)
    first_half = (lane % DA_HEAD_DIM) < (DA_HEAD_DIM // 2)
    sin_signed = jnp.where(first_half, -sin, sin)
    scale = DA_HEAD_DIM ** -0.5 * LOG2_E
    q_slabs = DA_WIDTH // MXU_COLS
    for sl in range(q_slabs):
        c0 = sl * MXU_COLS
        qs = proj(sl)
        ks = proj(q_slabs + sl)
        for l0 in range(0, MXU_COLS, LANES):
            q_ref[:, c0 + l0:c0 + l0 + LANES] = (
                _rope(qs[:, l0:l0 + LANES], cos, sin_signed, first_half) * scale).astype(BF16)
            k_ref[:, c0 + l0:c0 + l0 + LANES] = _rope(
                ks[:, l0:l0 + LANES], cos, sin_signed, first_half).astype(BF16)
    vt = jnp.concatenate([proj(2 * q_slabs + sl).T for sl in range(q_slabs)],
                         axis=0).astype(BF16)
    ones = jnp.ones((V_AUG - DA_V_DIM, vt.shape[1]), BF16)
    for hd in range(DA_HEADS):
        vt_ref[0, hd * V_AUG:hd * V_AUG + DA_V_DIM, :] = vt[hd * DA_V_DIM:(hd + 1) * DA_V_DIM]
        vt_ref[0, hd * V_AUG + DA_V_DIM:(hd + 1) * V_AUG, :] = ones
    sl = 3 * q_slabs
    glu_ref[...] = proj(sl) * jax.nn.sigmoid(proj(sl + 1))
    qm_ref[...] = (proj(sl + 2) * MEM_HEAD_DIM ** -0.5).astype(BF16)


def _mem_kv_kernel(mem_ref, g_ref, w_ref, mk_ref, mv_ref):
    n_mem = mem_ref.shape[1]
    mn = _rms(mem_ref[0], g_ref[...]).astype(BF16)
    kv = jnp.dot(mn, w_ref[...].astype(BF16), preferred_element_type=F32)
    mk_ref[0] = kv[:, :MEM_WIDTH].astype(BF16)
    mv = kv[:, MEM_WIDTH:]
    head_of_lane = lax.broadcasted_iota(jnp.int32, mv.shape, 1) // MEM_HEAD_DIM
    for hd in range(MEM_HEADS):
        mv_ref[0, hd * n_mem:(hd + 1) * n_mem, :] = jnp.where(
            head_of_lane == hd, mv, 0.0).astype(BF16)


def _diff_attn_kernel(lq1_ref, lk1_ref, lq2_ref, lk2_ref, sg_ref,
                      q_ref, k_ref, vt_ref, o_ref, m_ref, acc_ref, s_ref, mpart_ref):
    pair = pl.program_id(1)
    tq = ATTN_TILE
    tk = tq
    per_tile = 2 * DA_HEADS
    n_chain = ATTN_QTILES * per_tile

    lane = lax.broadcasted_iota(jnp.int32, (tq, LANES), 1)
    q_comp = []
    for sub in range(ATTN_QTILES):
        for hd in range(DA_HEADS):
            q = q_ref[0, sub * tq:(sub + 1) * tq, hd * LANES:(hd + 1) * LANES]
            zero = jnp.zeros_like(q)
            q_comp.append(jnp.where(lane < DA_HEAD_DIM, q, zero))
            q_comp.append(jnp.where(lane >= DA_HEAD_DIM, q, zero))

    n_slab = tk // SOFTMAX_SLAB
    rows = lambda r: slice(r * SOFTMAX_SLAB, (r + 1) * SOFTMAX_SLAB)
    head_of = lambda c: (c % per_tile) // 2

    nt_dot = lambda a, b_: lax.dot_general(a, b_, (((1,), (1,)), ((), ())),
                                           preferred_element_type=F32)
    slab_max = lambda s: functools.reduce(
        jnp.maximum, [s[rows(r)] for r in range(s.shape[0] // SOFTMAX_SLAB)])

    def key_block(jb, c):
        start = pl.multiple_of(jb * tk, tk)
        hd = head_of(c)
        return k_ref[0, pl.ds(start, tk), hd * LANES:(hd + 1) * LANES]

    def scores(jb, c, slot, diag):
        if diag:
            return scores_diag(jb, c, slot)
        s = nt_dot(key_block(jb, c), q_comp[c])
        s_ref[slot] = s
        mpart_ref[slot] = slab_max(s)

    def softmax_start(c, slot):
        m_old = m_ref[c]
        m_new = jnp.maximum(m_old, jnp.max(mpart_ref[slot], axis=0, keepdims=True))
        m_ref[c] = m_new
        return jnp.exp2(m_old - m_new), jnp.broadcast_to(m_new, (SOFTMAX_SLAB, tq))

    def probs(slot, slabs, cols, m_slab):
        return jnp.concatenate(
            [jnp.exp2(s_ref[slot, rows(r), cols] - m_slab[:, cols]).astype(BF16)
             for r in slabs], axis=0)

    def softmax_pv(jb, c, slot, diag):
        if diag:
            return softmax_pv_diag(jb, c, slot)
        hd = head_of(c)
        vt = vt_ref[0, jb, hd * V_AUG:(hd + 1) * V_AUG, :]
        alpha, m_slab = softmax_start(c, slot)
        p = probs(slot, range(n_slab), slice(0, tq), m_slab)
        acc_ref[c] = alpha * acc_ref[c] + jnp.dot(vt, p, preferred_element_type=F32)

    half = tk // 2

    def chunk_mask(n_keys, n_queries):
        key_chunk = lax.broadcasted_iota(jnp.int32, (n_keys, n_queries), 0) // CHUNK
        qry_chunk = lax.broadcasted_iota(jnp.int32, (n_keys, n_queries), 1) // CHUNK
        return key_chunk <= qry_chunk

    top_mask, bot_mask = chunk_mask(half, tq), chunk_mask(half, half)
    right = slice(half, tq)

    def scores_diag(jb, c, slot):
        kb = key_block(jb, c)
        s_top = jnp.where(top_mask, nt_dot(kb[0:half], q_comp[c]), NEG_BIG)
        s_bot = jnp.where(bot_mask, nt_dot(kb[half:tk], q_comp[c][half:tq]), NEG_BIG)
        s_ref[slot, 0:half, :] = s_top
        s_ref[slot, half:tk, right] = s_bot
        top, bot = slab_max(s_top), slab_max(s_bot)
        mpart_ref[slot] = jnp.concatenate(
            [top[:, 0:half], jnp.maximum(top[:, right], bot)], axis=1)

    def softmax_pv_diag(jb, c, slot):
        hd = head_of(c)
        vt = vt_ref[0, jb, hd * V_AUG:(hd + 1) * V_AUG, :]
        m_new = jnp.max(mpart_ref[slot], axis=0, keepdims=True)
        m_ref[c] = m_new
        m_slab = jnp.broadcast_to(m_new, (SOFTMAX_SLAB, tq))
        p_top = probs(slot, range(n_slab // 2), slice(0, tq), m_slab)
        p_bot = probs(slot, range(n_slab // 2, n_slab), right, m_slab)
        pv = jnp.dot(vt[:, 0:half], p_top, preferred_element_type=F32)
        pv_right = pv[:, right] + jnp.dot(vt[:, half:tk], p_bot, preferred_element_type=F32)
        acc_ref[c] = jnp.concatenate([pv[:, 0:half], pv_right], axis=1)

    first = ATTN_QTILES * pair
    tasks = []
    for u in reversed(range(ATTN_QTILES)):
        for sub in range(u, ATTN_QTILES):
            for i in range(per_tile):
                tasks.append((first + u, sub * per_tile + i, sub == u))
    assert len(tasks) % SCORE_SLOTS == 0 and n_chain % SCORE_SLOTS == 0
    n_loop = first

    for t in range(SCORE_AHEAD):
        scores(tasks[t][0], tasks[t][1], t % SCORE_SLOTS, tasks[t][2])
    for t, (jb, c, diag) in enumerate(tasks):
        ahead = t + SCORE_AHEAD
        if ahead < len(tasks):
            scores(tasks[ahead][0], tasks[ahead][1], ahead % SCORE_SLOTS, tasks[ahead][2])
        else:
            scores(0, ahead - len(tasks), ahead % SCORE_SLOTS, False)
        softmax_pv(jb, c, t % SCORE_SLOTS, diag)

    def body(j, carry):
        j_next = jnp.minimum(j + 1, n_loop - 1)
        for c in range(n_chain):
            ahead = c + SCORE_AHEAD
            if ahead < n_chain:
                scores(j, ahead, ahead % SCORE_SLOTS, False)
            else:
                scores(j_next, ahead - n_chain, ahead % SCORE_SLOTS, False)
            softmax_pv(j, c, c % SCORE_SLOTS, False)
        return carry

    lax.fori_loop(0, n_loop, body, 0)

    lam = (jnp.exp(jnp.sum(lq1_ref[...] * lk1_ref[...], axis=-1, keepdims=True))
           - jnp.exp(jnp.sum(lq2_ref[...] * lk2_ref[...], axis=-1, keepdims=True))
           + LAM_INIT)
    gain = jnp.broadcast_to(sg_ref[...], (DA_V_DIM, LANES)) * (1.0 - LAM_INIT)
    for sub in range(ATTN_QTILES):
        for hd in range(DA_HEADS):
            c = sub * per_tile + 2 * hd
            for l0 in range(0, tq, LANES):
                a1 = acc_ref[c, :, l0:l0 + LANES]
                a2 = acc_ref[c + 1, :, l0:l0 + LANES]
                w1 = 1.0 / a1[DA_V_DIM:DA_V_DIM + 1]
                w2 = lam / a2[DA_V_DIM:DA_V_DIM + 1]
                o = a1[:DA_V_DIM] * w1 - a2[:DA_V_DIM] * w2
                o = o * lax.rsqrt(jnp.mean(o * o, axis=0, keepdims=True) + EPS) * gain
                o_ref[0, sub * tq + l0:sub * tq + l0 + LANES,
                      hd * DA_V_DIM:(hd + 1) * DA_V_DIM] = o.T.astype(BF16)


def _post_mix_kernel(h_ref, oda_ref, glu_ref, halo_ref, qm_ref, mk_ref, mv_ref,
                     dww_ref, dwb_ref, lng_ref, lnb_ref, wout32_ref,
                     g2_ref, wg32_ref, wu32_ref, wd32_ref, gf_ref,
                     out_ref, wout_ref, wg_ref, wu_ref, wd_ref,
                     cbuf_ref, shift_ref, yconv_ref, pmem_ref, h2_ref, xn_ref, act_ref,
                     *, tiles_per_seq):
    n_stage = wg_ref.shape[0]
    pid = pl.program_id(0)
    n_steps = pl.num_programs(0) - n_stage + 1
    refs = (h_ref, oda_ref, glu_ref, halo_ref, qm_ref, mk_ref, mv_ref,
            dww_ref, dwb_ref, lng_ref, lnb_ref, wout_ref,
            g2_ref, wg_ref, wu_ref, wd_ref, gf_ref,
            out_ref, cbuf_ref, shift_ref, yconv_ref, pmem_ref, h2_ref, xn_ref, act_ref)

    @pl.when(pid < n_stage)
    def _():
        _stage_weight_slabs(pid, wg32_ref, wu32_ref, wd32_ref, wg_ref, wu_ref, wd_ref)
        slab = jnp.minimum(pid, wout_ref.shape[0] // FFN_CHUNK - 1)
        rows = pl.ds(pl.multiple_of(slab * FFN_CHUNK, FFN_CHUNK), FFN_CHUNK)
        wout_ref[rows, :] = wout32_ref[...].astype(BF16)

    @pl.when(pid == n_stage - 1)
    def _():
        _post_mix_step(0, n_steps, *refs, tiles_per_seq, with_swiglu=False)

    @pl.when(pid >= n_stage)
    def _():
        _post_mix_step(pid - n_stage + 1, n_steps, *refs, tiles_per_seq, with_swiglu=True)


def _post_mix_step(step, n_steps, h_ref, oda_ref, glu_ref, halo_ref, qm_ref, mk_ref, mv_ref,
                   dww_ref, dwb_ref, lng_ref, lnb_ref, wout_ref,
                   g2_ref, wg_ref, wu_ref, wd_ref, gf_ref,
                   out_ref, cbuf_ref, shift_ref, yconv_ref, pmem_ref, h2_ref, xn_ref, act_ref,
                   tiles_per_seq, with_swiglu):
    tm = h_ref.shape[0]
    n_mem = mk_ref.shape[1]
    n_slabs = wg_ref.shape[0]
    tile = jnp.minimum(step, n_steps - 2)
    seq_start = (tile % tiles_per_seq) == 0
    base = CONV_HALO - (CONV_WIDTH - 1)

    def after(c):
        tile_ = act_ref[0:2 * SUBLANES, c * FFN_CHUNK:c * FFN_CHUNK + LANES]
        return jnp.sum(tile_.astype(F32) * 0.0, keepdims=True)

    def before_next_slab(done_bf16):
        sl = (slice(0, 2 * SUBLANES), slice(0, LANES))
        xn_ref[sl] = xn_ref[sl] + done_bf16 * jnp.zeros_like(done_bf16)

    def conv_setup(c):
        halo = halo_ref[...]
        cbuf_ref[0:CONV_HALO, :] = jnp.where(seq_start, jnp.zeros_like(halo), halo)
        cbuf_ref[CONV_HALO:CONV_HALO + tm, :] = glu_ref[...]
        span = tm + CONV_HALO - SUBLANES
        for r in range(1, SUBLANES):
            shift_ref[r - 1] = cbuf_ref[r:r + span, :]
        before_next_slab(shift_ref[SUBLANES - 2, 0:2 * SUBLANES, 0:LANES].astype(BF16))

    def conv_rows(r0, wait):
        y = jnp.zeros((CONV_ROWS, CONV_CH), F32) + wait
        for j in range(CONV_WIDTH):
            phase = (base + j) % SUBLANES
            start = base + j - phase + r0
            if phase == 0:
                window = cbuf_ref[start:start + CONV_ROWS, :]
            else:
                window = shift_ref[phase - 1, start:start + CONV_ROWS, :]
            y = y + dww_ref[j:j + 1, :] * window
        y = y + dwb_ref[...]
        mu = jnp.mean(y, axis=-1, keepdims=True)
        yc = y - mu
        var = jnp.mean(yc * yc, axis=-1, keepdims=True)
        y = yc * lax.rsqrt(var + EPS) * lng_ref[...] + lnb_ref[...]
        piece = (y * jax.nn.sigmoid(y)).astype(BF16)
        yconv_ref[r0:r0 + CONV_ROWS, :] = piece
        return piece

    def conv_pieces(r0s, c):
        wait = after(c - 1)
        for r0 in r0s:
            piece = conv_rows(r0, wait)
        before_next_slab(piece[0:2 * SUBLANES, 0:LANES])

    def mem_heads(heads, c):
        qm = qm_ref[...]
        head_of_lane = lax.broadcasted_iota(jnp.int32, qm.shape, 1) // MEM_HEAD_DIM
        wait = after(c - 1)
        for hd in heads:
            qh = jnp.where(head_of_lane == hd, qm, jnp.zeros_like(qm))
            sc = lax.dot_general(qh, mk_ref[0], (((1,), (1,)), ((), ())),
                                 preferred_element_type=F32)
            sc = sc - (jnp.max(sc, axis=-1, keepdims=True) + wait)
            p = jnp.exp(sc)
            p = (p / jnp.sum(p, axis=-1, keepdims=True)).astype(BF16)
            pmem_ref[:, hd * n_mem:(hd + 1) * n_mem] = p
        before_next_slab(p[0:2 * SUBLANES, 0:LANES])

    pieces = [conv_setup]
    pieces += [functools.partial(conv_pieces, (r0,)) for r0 in range(0, tm, CONV_ROWS)]
    pieces += [functools.partial(mem_heads, (0, 1)), functools.partial(mem_heads, (2, 3))]
    assert len(pieces) <= n_slabs, "at most one mixing piece per SwiGLU slab"
    pieces += [None] * (n_slabs - len(pieces))

    def finish_mix():
        o_mem = jnp.dot(pmem_ref[...], mv_ref[0], preferred_element_type=F32).astype(BF16)
        mix = (jnp.dot(yconv_ref[...], wout_ref[DA_WIDTH:DA_WIDTH + CONV_CH, :],
                       preferred_element_type=F32)
               + jnp.dot(o_mem, wout_ref[DA_WIDTH + CONV_CH:, :], preferred_element_type=F32))
        h2_ref[step % 2] = h2_ref[step % 2] + mix

    if not with_swiglu:
        xn_ref[0:2 * SUBLANES, 0:LANES] = jnp.zeros((2 * SUBLANES, LANES), BF16)
        act_ref[0:2 * SUBLANES, :] = jnp.zeros((2 * SUBLANES, act_ref.shape[1]), BF16)
        h2_ref[step % 2] = h_ref[...] + jnp.dot(oda_ref[...], wout_ref[0:DA_WIDTH, :],
                                                preferred_element_type=F32)
        for n, piece in enumerate(pieces):
            if piece is not None:
                piece(n)
        finish_mix()
        return

    h_prev = h2_ref[(step + 1) % 2]
    xn_ref[...] = _rms(h_prev, g2_ref[...]).astype(BF16)

    h2_ref[step % 2] = h_ref[...] + jnp.dot(oda_ref[...], wout_ref[0:DA_WIDTH, :],
                                            preferred_element_type=F32)
    for n in range(n_slabs):
        c0 = n * FFN_CHUNK
        xn = xn_ref[...]
        gate = jnp.dot(xn, wg_ref[n], preferred_element_type=F32)
        up = jnp.dot(xn, wu_ref[n], preferred_element_type=F32)
        if pieces[n] is not None:
            pieces[n](n)
        act_ref[:, c0:c0 + FFN_CHUNK] = (gate * jax.nn.sigmoid(gate) * up).astype(BF16)
    ffn = jnp.dot(act_ref[...], wd_ref[...], preferred_element_type=F32)

    finish_mix()
    out_ref[...] = _rms(h_prev + 0.5 * ffn, gf_ref[...])


def _resident(shape):
    return pl.BlockSpec(shape, lambda *_: (0,) * len(shape), pipeline_mode=pl.Buffered(1))


def _rope_tables(seq):
    half = DA_HEAD_DIM // 2
    inv_freq = 1.0 / (ROPE_THETA ** (jnp.arange(0, DA_HEAD_DIM, 2, dtype=F32) / DA_HEAD_DIM))
    ang = inv_freq[:, None] * jnp.arange(seq, dtype=F32)[None, :]
    assert ang.shape == (half, seq)
    return jnp.cos(ang), jnp.sin(ang)


def kernel(x, mem, ffn1_norm_g, ffn1_w_gate, ffn1_w_up, ffn1_w_down, mix_norm_g, mem_norm_g, w_in, lambda_q1, lambda_k1, lambda_q2, lambda_k2, subln_g, conv_dw_w, conv_dw_b, conv_ln_g, conv_ln_b, w_mem_kv, w_out, ffn2_norm_g, ffn2_w_gate, ffn2_w_up, ffn2_w_down, final_norm_g):
    b, s, d = x.shape
    n_mem = mem.shape[1]
    d_ff = ffn1_w_gate.shape[-1]
    in_width = w_in.shape[-1]
    assert ffn1_norm_g.shape[0] == 1, "single layer"
    tm = TOKEN_TILE
    ta = ATTN_TILE
    assert s % tm == 0 and s % (ATTN_QTILES * ta) == 0 and tm == ta and d_ff % FFN_CHUNK == 0
    t = b * s
    n_tiles = s // tm
    params = functools.partial(pltpu.CompilerParams, vmem_limit_bytes=VMEM_LIMIT)

    cos_t, sin_t = _rope_tables(s)
    row = lambda a: a.reshape(1, -1)

    n_stage = d_ff // FFN_CHUNK
    in_slabs = in_width // MXU_COLS
    assert FFN_CHUNK == MXU_COLS and in_slabs <= n_stage
    tile_of = lambda i: jnp.maximum(i - n_stage, 0)
    slab_of = lambda i: jnp.minimum(i, n_stage - 1)
    tok = lambda w: pl.BlockSpec((tm, w), lambda i: (tile_of(i), 0))
    col_slab = pl.BlockSpec((d, FFN_CHUNK), lambda i: (0, slab_of(i)))
    h1, q, k, vt, glu, qm = pl.pallas_call(
        _pre_mix_kernel,
        grid=(n_stage + t // tm,),
        in_specs=[tok(d), _resident((1, d)), col_slab, col_slab,
                  pl.BlockSpec((FFN_CHUNK, d), lambda i: (slab_of(i), 0)), _resident((1, d)),
                  pl.BlockSpec((d, MXU_COLS), lambda i: (0, jnp.minimum(i, in_slabs - 1))),
                  pl.BlockSpec((DA_HEAD_DIM // 2, tm), lambda i: (0, tile_of(i) % n_tiles)),
                  pl.BlockSpec((DA_HEAD_DIM // 2, tm), lambda i: (0, tile_of(i) % n_tiles))],
        out_specs=[tok(d), tok(DA_WIDTH), tok(DA_WIDTH),
                   pl.BlockSpec((1, DA_HEADS * V_AUG, tm), lambda i: (tile_of(i), 0, 0)),
                   tok(CONV_CH), tok(MEM_WIDTH)],
        out_shape=[jax.ShapeDtypeStruct((t, d), F32),
                   jax.ShapeDtypeStruct((t, DA_WIDTH), BF16),
                   jax.ShapeDtypeStruct((t, DA_WIDTH), BF16),
                   jax.ShapeDtypeStruct((t // tm, DA_HEADS * V_AUG, tm), BF16),
                   jax.ShapeDtypeStruct((t, CONV_CH), F32),
                   jax.ShapeDtypeStruct((t, MEM_WIDTH), BF16)],
        scratch_shapes=[pltpu.VMEM((n_stage, d, FFN_CHUNK), BF16),
                        pltpu.VMEM((n_stage, d, FFN_CHUNK), BF16),
                        pltpu.VMEM((d_ff, d), BF16),
                        pltpu.VMEM((in_slabs, d, MXU_COLS), BF16),
                        pltpu.VMEM((tm, d_ff), BF16)],
        compiler_params=params(dimension_semantics=("arbitrary",)),
        name="pre_mix",
    )(x.reshape(t, d), row(ffn1_norm_g), ffn1_w_gate.reshape(d, d_ff),
      ffn1_w_up.reshape(d, d_ff), ffn1_w_down.reshape(d_ff, d), row(mix_norm_g),
      w_in.reshape(d, in_width), cos_t, sin_t)

    mk, mv = pl.pallas_call(
        _mem_kv_kernel,
        grid=(b,),
        in_specs=[pl.BlockSpec((1, n_mem, d), lambda i: (i, 0, 0)), _resident((1, d)),
                  _resident((d, 2 * MEM_WIDTH))],
        out_specs=[pl.BlockSpec((1, n_mem, MEM_WIDTH), lambda i: (i, 0, 0)),
                   pl.BlockSpec((1, MEM_HEADS * n_mem, MEM_WIDTH), lambda i: (i, 0, 0))],
        out_shape=[jax.ShapeDtypeStruct((b, n_mem, MEM_WIDTH), BF16),
                   jax.ShapeDtypeStruct((b, MEM_HEADS * n_mem, MEM_WIDTH), BF16)],
        compiler_params=params(dimension_semantics=("parallel",)),
        name="mem_kv",
    )(mem, row(mem_norm_g), w_mem_kv.reshape(d, 2 * MEM_WIDTH))

    lam_spec = _resident((1, DA_HEAD_DIM))
    o_da = pl.pallas_call(
        _diff_attn_kernel,
        grid=(b, s // (ATTN_QTILES * ta)),
        in_specs=[lam_spec, lam_spec, lam_spec, lam_spec, _resident((DA_V_DIM, 1)),
                  pl.BlockSpec((1, ATTN_QTILES * ta, DA_WIDTH), lambda bi, qi: (bi, qi, 0)),
                  pl.BlockSpec((1, s, DA_WIDTH), lambda bi, qi: (bi, 0, 0)),
                  pl.BlockSpec((1, s // ta, DA_HEADS * V_AUG, ta),
                               lambda bi, qi: (bi, 0, 0, 0))],
        out_specs=pl.BlockSpec((1, ATTN_QTILES * ta, DA_WIDTH), lambda bi, qi: (bi, qi, 0)),
        out_shape=jax.ShapeDtypeStruct((b, s, DA_WIDTH), BF16),
        scratch_shapes=[pltpu.VMEM((ATTN_QTILES * 2 * DA_HEADS, 1, ta), F32),
                        pltpu.VMEM((ATTN_QTILES * 2 * DA_HEADS, V_AUG, ta), F32),
                        pltpu.VMEM((SCORE_SLOTS, ta, ta), F32),
                        pltpu.VMEM((SCORE_SLOTS, SOFTMAX_SLAB, ta), F32)],
        compiler_params=params(dimension_semantics=("parallel", "arbitrary")),
        name="diff_attn",
    )(lambda_q1, lambda_k1, lambda_q2, lambda_k2, subln_g.reshape(DA_V_DIM, 1),
      q.reshape(b, s, DA_WIDTH), k.reshape(b, s, DA_WIDTH),
      vt.reshape(b, s // ta, DA_HEADS * V_AUG, ta))

    halo_blocks = tm // CONV_HALO
    last = t // tm - 1
    out_slabs = d // FFN_CHUNK
    assert out_slabs < n_stage
    mixed_tile = lambda i: jnp.clip(i - n_stage + 1, 0, last)
    mixed = lambda w: pl.BlockSpec((tm, w), lambda i: (mixed_tile(i), 0))
    per_seq = lambda rows: pl.BlockSpec((1, rows, MEM_WIDTH),
                                        lambda i: (mixed_tile(i) // n_tiles, 0, 0))
    out = pl.pallas_call(
        functools.partial(_post_mix_kernel, tiles_per_seq=n_tiles),
        grid=(n_stage + t // tm,),
        in_specs=[mixed(d), mixed(DA_WIDTH), mixed(CONV_CH),
                  pl.BlockSpec((CONV_HALO, CONV_CH),
                               lambda i: (jnp.maximum(mixed_tile(i) * halo_blocks - 1, 0), 0)),
                  mixed(MEM_WIDTH), per_seq(n_mem), per_seq(MEM_HEADS * n_mem),
                  _resident((CONV_WIDTH, CONV_CH)), _resident((1, CONV_CH)),
                  _resident((1, CONV_CH)), _resident((1, CONV_CH)),
                  pl.BlockSpec((FFN_CHUNK, d), lambda i: (jnp.minimum(i, out_slabs - 1), 0),
                               pipeline_mode=pl.Buffered(1)),
                  _resident((1, d)), col_slab, col_slab,
                  pl.BlockSpec((FFN_CHUNK, d), lambda i: (slab_of(i), 0)), _resident((1, d))],
        out_specs=pl.BlockSpec((tm, d), lambda i: (jnp.maximum(i - n_stage, 0), 0)),
        out_shape=jax.ShapeDtypeStruct((t, d), F32),
        scratch_shapes=[pltpu.VMEM((d, d), BF16),
                        pltpu.VMEM((n_stage, d, FFN_CHUNK), BF16),
                        pltpu.VMEM((n_stage, d, FFN_CHUNK), BF16),
                        pltpu.VMEM((d_ff, d), BF16),
                        pltpu.VMEM((CONV_HALO + tm, CONV_CH), F32),
                        pltpu.VMEM((SUBLANES - 1, tm + CONV_HALO - SUBLANES, CONV_CH), F32),
                        pltpu.VMEM((tm, CONV_CH), BF16),
                        pltpu.VMEM((tm, MEM_HEADS * n_mem), BF16),
                        pltpu.VMEM((2, tm, d), F32),
                        pltpu.VMEM((tm, d), BF16),
                        pltpu.VMEM((tm, d_ff), BF16)],
        compiler_params=params(dimension_semantics=("arbitrary",)),
        name="post_mix",
    )(h1, o_da.reshape(t, DA_WIDTH), glu, glu, qm, mk, mv,
      conv_dw_w.reshape(CONV_WIDTH, CONV_CH), conv_dw_b,
      conv_ln_g, conv_ln_b, w_out.reshape(d, d), row(ffn2_norm_g),
      ffn2_w_gate.reshape(d, d_ff), ffn2_w_up.reshape(d, d_ff),
      ffn2_w_down.reshape(d_ff, d), row(final_norm_g))
    return out.reshape(b, s, d)
```

```python
import functools
import math

import jax
import jax.numpy as jnp
from jax import lax
from jax.experimental import pallas as pl
from jax.experimental.pallas import tpu as pltpu

F32 = jnp.float32
BF16 = jnp.bfloat16

EPS = 1e-5
ROPE_THETA = 10000.0
CHUNK = 64
DA_HEADS = 4
DA_HEAD_DIM = 64
DA_V_DIM = 128
V_AUG = DA_V_DIM + 16
DA_WIDTH = 512
CONV_CH = 256
CONV_WIDTH = 31
MEM_HEADS = 4
MEM_HEAD_DIM = 64
MEM_WIDTH = 256
LAM_INIT = 0.8 - 0.6 * math.exp(-0.3 * 0)

V7X_VMEM_BYTES = 64 * 1024 * 1024
VMEM_LIMIT = V7X_VMEM_BYTES - 8 * 1024 * 1024
LANES = 128
SUBLANES = 8
MXU_COLS = 256

TOKEN_TILE = 512
ATTN_TILE = 512
ATTN_QTILES = 2
FFN_CHUNK = 256
CONV_HALO = 32
CONV_ROWS = 64
SOFTMAX_SLAB = 16
SCORE_AHEAD = 1
SCORE_SLOTS = 4
NEG_BIG = -1e30
LOG2_E = math.log2(math.e)


def _rms(x, g):
    return x * lax.rsqrt(jnp.mean(x * x, axis=-1, keepdims=True) + EPS) * g


def _swiglu(xn, wg_ref, wu_ref, wd_ref, act_ref):
    for n in range(wg_ref.shape[0]):
        c = n * FFN_CHUNK
        gate = jnp.dot(xn, wg_ref[n], preferred_element_type=F32)
        up = jnp.dot(xn, wu_ref[n], preferred_element_type=F32)
        act_ref[:, c:c + FFN_CHUNK] = (gate * jax.nn.sigmoid(gate) * up).astype(BF16)
    return jnp.dot(act_ref[...], wd_ref[...], preferred_element_type=F32)


def _stage_weight_slabs(step, wg32_ref, wu32_ref, wd32_ref, wg_ref, wu_ref, wd_ref):
    wg_ref[step] = wg32_ref[...].astype(BF16)
    wu_ref[step] = wu32_ref[...].astype(BF16)
    rows = pl.ds(pl.multiple_of(step * FFN_CHUNK, FFN_CHUNK), FFN_CHUNK)
    wd_ref[rows, :] = wd32_ref[...].astype(BF16)


def _rope(x, cos, sin_signed, first_half):
    partner = jnp.where(first_half,
                        pltpu.roll(x, LANES - DA_HEAD_DIM // 2, axis=1),
                        pltpu.roll(x, DA_HEAD_DIM // 2, axis=1))
    return x * cos + partner * sin_signed


def _pre_mix_kernel(x_ref, g1_ref, wg32_ref, wu32_ref, wd32_ref, gm_ref, win32_ref,
                    cos_ref, sin_ref,
                    h_ref, q_ref, k_ref, vt_ref, glu_ref, qm_ref,
                    wg_ref, wu_ref, wd_ref, win_ref, act_ref):
    step = pl.program_id(0)
    n_stage = wg_ref.shape[0]

    @pl.when(step < n_stage)
    def _():
        _stage_weight_slabs(step, wg32_ref, wu32_ref, wd32_ref, wg_ref, wu_ref, wd_ref)
        win_ref[jnp.minimum(step, win_ref.shape[0] - 1)] = win32_ref[...].astype(BF16)

    @pl.when(step >= n_stage)
    def _():
        _pre_mix_tile(x_ref, g1_ref, wg_ref, wu_ref, wd_ref, gm_ref, win_ref, cos_ref, sin_ref,
                      h_ref, q_ref, k_ref, vt_ref, glu_ref, qm_ref, act_ref)


def _pre_mix_tile(x_ref, g1_ref, wg_ref, wu_ref, wd_ref, gm_ref, win_ref, cos_ref, sin_ref,
                  h_ref, q_ref, k_ref, vt_ref, glu_ref, qm_ref, act_ref):
    x = x_ref[...]
    xn = _rms(x, g1_ref[...]).astype(BF16)
    h = x + 0.5 * _swiglu(xn, wg_ref, wu_ref, wd_ref, act_ref)
    h_ref[...] = h

    n = _rms(h, gm_ref[...]).astype(BF16)
    proj = lambda slab: jnp.dot(n, win_ref[slab], preferred_element_type=F32)
    reps = LANES // cos_ref.shape[0]
    cos = jnp.tile(cos_ref[...].T, (1, reps))
    sin = jnp.tile(sin_ref[...].T, (1, reps))
    lane = lax.broadcasted_iota(jnp.int32, cos.shape, 1)
    first_half = (lane % DA_HEAD_DIM) < (DA_HEAD_DIM // 2)
    sin_signed = jnp.where(first_half, -sin, sin)
    scale = DA_HEAD_DIM ** -0.5 * LOG2_E
    q_slabs = DA_WIDTH // MXU_COLS
    for sl in range(q_slabs):
        c0 = sl * MXU_COLS
        qs = proj(sl)
        ks = proj(q_slabs + sl)
        for l0 in range(0, MXU_COLS, LANES):
            q_ref[:, c0 + l0:c0 + l0 + LANES] = (
                _rope(qs[:, l0:l0 + LANES], cos, sin_signed, first_half) * scale).astype(BF16)
            k_ref[:, c0 + l0:c0 + l0 + LANES] = _rope(
                ks[:, l0:l0 + LANES], cos, sin_signed, first_half).astype(BF16)
    vt = jnp.concatenate([proj(2 * q_slabs + sl).T for sl in range(q_slabs)],
                         axis=0).astype(BF16)
    ones = jnp.ones((V_AUG - DA_V_DIM, vt.shape[1]), BF16)
    for hd in range(DA_HEADS):
        vt_ref[0, hd * V_AUG:hd * V_AUG + DA_V_DIM, :] = vt[hd * DA_V_DIM:(hd + 1) * DA_V_DIM]
        vt_ref[0, hd * V_AUG + DA_V_DIM:(hd + 1) * V_AUG, :] = ones
    sl = 3 * q_slabs
    glu_ref[...] = proj(sl) * jax.nn.sigmoid(proj(sl + 1))
    qm_ref[...] = (proj(sl + 2) * MEM_HEAD_DIM ** -0.5).astype(BF16)


def _mem_kv_kernel(mem_ref, g_ref, w_ref, mk_ref, mv_ref):
    n_mem = mem_ref.shape[1]
    mn = _rms(mem_ref[0], g_ref[...]).astype(BF16)
    kv = jnp.dot(mn, w_ref[...].astype(BF16), preferred_element_type=F32)
    mk_ref[0] = kv[:, :MEM_WIDTH].astype(BF16)
    mv = kv[:, MEM_WIDTH:]
    head_of_lane = lax.broadcasted_iota(jnp.int32, mv.shape, 1) // MEM_HEAD_DIM
    for hd in range(MEM_HEADS):
        mv_ref[0, hd * n_mem:(hd + 1) * n_mem, :] = jnp.where(
            head_of_lane == hd, mv, 0.0).astype(BF16)


def _diff_attn_kernel(lq1_ref, lk1_ref, lq2_ref, lk2_ref, sg_ref,
                      q_ref, k_ref, vt_ref, o_ref, m_ref, acc_ref, s_ref, mpart_ref):
    pair = pl.program_id(1)
    tq = ATTN_TILE
    tk = tq
    per_tile = 2 * DA_HEADS
    n_chain = ATTN_QTILES * per_tile

    lane = lax.broadcasted_iota(jnp.int32, (tq, LANES), 1)
    q_comp = []
    for sub in range(ATTN_QTILES):
        for hd in range(DA_HEADS):
            q = q_ref[0, sub * tq:(sub + 1) * tq, hd * LANES:(hd + 1) * LANES]
            zero = jnp.zeros_like(q)
            q_comp.append(jnp.where(lane < DA_HEAD_DIM, q, zero))
            q_comp.append(jnp.where(lane >= DA_HEAD_DIM, q, zero))

    n_slab = tk // SOFTMAX_SLAB
    rows = lambda r: slice(r * SOFTMAX_SLAB, (r + 1) * SOFTMAX_SLAB)
    head_of = lambda c: (c % per_tile) // 2

    nt_dot = lambda a, b_: lax.dot_general(a, b_, (((1,), (1,)), ((), ())),
                                           preferred_element_type=F32)
    slab_max = lambda s: functools.reduce(
        jnp.maximum, [s[rows(r)] for r in range(s.shape[0] // SOFTMAX_SLAB)])

    def key_block(jb, c):
        start = pl.multiple_of(jb * tk, tk)
        hd = head_of(c)
        return k_ref[0, pl.ds(start, tk), hd * LANES:(hd + 1) * LANES]

    def scores(jb, c, slot, diag):
        if diag:
            return scores_diag(jb, c, slot)
        s = nt_dot(key_block(jb, c), q_comp[c])
        s_ref[slot] = s
        mpart_ref[slot] = slab_max(s)

    def softmax_start(c, slot):
        m_old = m_ref[c]
        m_new = jnp.maximum(m_old, jnp.max(mpart_ref[slot], axis=0, keepdims=True))
        m_ref[c] = m_new
        return jnp.exp2(m_old - m_new), jnp.broadcast_to(m_new, (SOFTMAX_SLAB, tq))

    def probs(slot, slabs, cols, m_slab):
        return jnp.concatenate(
            [jnp.exp2(s_ref[slot, rows(r), cols] - m_slab[:, cols]).astype(BF16)
             for r in slabs], axis=0)

    def softmax_pv(jb, c, slot, diag):
        if diag:
            return softmax_pv_diag(jb, c, slot)
        hd = head_of(c)
        vt = vt_ref[0, jb, hd * V_AUG:(hd + 1) * V_AUG, :]
        alpha, m_slab = softmax_start(c, slot)
        p = probs(slot, range(n_slab), slice(0, tq), m_slab)
        acc_ref[c] = alpha * acc_ref[c] + jnp.dot(vt, p, preferred_element_type=F32)

    half = tk // 2

    def chunk_mask(n_keys, n_queries):
        key_chunk = lax.broadcasted_iota(jnp.int32, (n_keys, n_queries), 0) // CHUNK
        qry_chunk = lax.broadcasted_iota(jnp.int32, (n_keys, n_queries), 1) // CHUNK
        return key_chunk <= qry_chunk

    top_mask, bot_mask = chunk_mask(half, tq), chunk_mask(half, half)
    right = slice(half, tq)

    def scores_diag(jb, c, slot):
        kb = key_block(jb, c)
        s_top = jnp.where(top_mask, nt_dot(kb[0:half], q_comp[c]), NEG_BIG)
        s_bot = jnp.where(bot_mask, nt_dot(kb[half:tk], q_comp[c][half:tq]), NEG_BIG)
        s_ref[slot, 0:half, :] = s_top
        s_ref[slot, half:tk, right] = s_bot
        top, bot = slab_max(s_top), slab_max(s_bot)
        mpart_ref[slot] = jnp.concatenate(
            [top[:, 0:half], jnp.maximum(top[:, right], bot)], axis=1)

    def softmax_pv_diag(jb, c, slot):
        hd = head_of(c)
        vt = vt_ref[0, jb, hd * V_AUG:(hd + 1) * V_AUG, :]
        m_new = jnp.max(mpart_ref[slot], axis=0, keepdims=True)
        m_ref[c] = m_new
        m_slab = jnp.broadcast_to(m_new, (SOFTMAX_SLAB, tq))
        p_top = probs(slot, range(n_slab // 2), slice(0, tq), m_slab)
        p_bot = probs(slot, range(n_slab // 2, n_slab), right, m_slab)
        pv = jnp.dot(vt[:, 0:half], p_top, preferred_element_type=F32)
        pv_right = pv[:, right] + jnp.dot(vt[:, half:tk], p_bot, preferred_element_type=F32)
        acc_ref[c] = jnp.concatenate([pv[:, 0:half], pv_right], axis=1)

    first = ATTN_QTILES * pair
    tasks = []
    for u in reversed(range(ATTN_QTILES)):
        for sub in range(u, ATTN_QTILES):
            for i in range(per_tile):
                tasks.append((first + u, sub * per_tile + i, sub == u))
    assert len(tasks) % SCORE_SLOTS == 0 and n_chain % SCORE_SLOTS == 0
    n_loop = first

    for t in range(SCORE_AHEAD):
        scores(tasks[t][0], tasks[t][1], t % SCORE_SLOTS, tasks[t][2])
    for t, (jb, c, diag) in enumerate(tasks):
        ahead = t + SCORE_AHEAD
        if ahead < len(tasks):
            scores(tasks[ahead][0], tasks[ahead][1], ahead % SCORE_SLOTS, tasks[ahead][2])
        else:
            scores(0, ahead - len(tasks), ahead % SCORE_SLOTS, False)
        softmax_pv(jb, c, t % SCORE_SLOTS, diag)

    def body(j, carry):
        j_next = jnp.minimum(j + 1, n_loop - 1)
        for c in range(n_chain):
            ahead = c + SCORE_AHEAD
            if ahead < n_chain:
                scores(j, ahead, ahead % SCORE_SLOTS, False)
            else:
                scores(j_next, ahead - n_chain, ahead % SCORE_SLOTS, False)
            softmax_pv(j, c, c % SCORE_SLOTS, False)
        return carry

    lax.fori_loop(0, n_loop, body, 0)

    lam = (jnp.exp(jnp.sum(lq1_ref[...] * lk1_ref[...], axis=-1, keepdims=True))
           - jnp.exp(jnp.sum(lq2_ref[...] * lk2_ref[...], axis=-1, keepdims=True))
           + LAM_INIT)
    gain = jnp.broadcast_to(sg_ref[...], (DA_V_DIM, LANES)) * (1.0 - LAM_INIT)
    for sub in range(ATTN_QTILES):
        for hd in range(DA_HEADS):
            c = sub * per_tile + 2 * hd
            for l0 in range(0, tq, LANES):
                a1 = acc_ref[c, :, l0:l0 + LANES]
                a2 = acc_ref[c + 1, :, l0:l0 + LANES]
                w1 = 1.0 / a1[DA_V_DIM:DA_V_DIM + 1]
                w2 = lam / a2[DA_V_DIM:DA_V_DIM + 1]
                o = a1[:DA_V_DIM] * w1 - a2[:DA_V_DIM] * w2
                o = o * lax.rsqrt(jnp.mean(o * o, axis=0, keepdims=True) + EPS) * gain
                o_ref[0, sub * tq + l0:sub * tq + l0 + LANES,
                      hd * DA_V_DIM:(hd + 1) * DA_V_DIM] = o.T.astype(BF16)


def _post_mix_kernel(h_ref, oda_ref, glu_ref, halo_ref, qm_ref, mk_ref, mv_ref,
                     dww_ref, dwb_ref, lng_ref, lnb_ref, wout32_ref,
                     g2_ref, wg32_ref, wu32_ref, wd32_ref, gf_ref,
                     out_ref, wout_ref, wg_ref, wu_ref, wd_ref,
                     cbuf_ref, shift_ref, yconv_ref, pmem_ref, h2_ref, xn_ref, act_ref,
                     *, tiles_per_seq):
    n_stage = wg_ref.shape[0]
    pid = pl.program_id(0)
    n_steps = pl.num_programs(0) - n_stage + 1
    refs = (h_ref, oda_ref, glu_ref, halo_ref, qm_ref, mk_ref, mv_ref,
            dww_ref, dwb_ref, lng_ref, lnb_ref, wout_ref,
            g2_ref, wg_ref, wu_ref, wd_ref, gf_ref,
            out_ref, cbuf_ref, shift_ref, yconv_ref, pmem_ref, h2_ref, xn_ref, act_ref)

    @pl.when(pid < n_stage)
    def _():
        _stage_weight_slabs(pid, wg32_ref, wu32_ref, wd32_ref, wg_ref, wu_ref, wd_ref)
        slab = jnp.minimum(pid, wout_ref.shape[0] // FFN_CHUNK - 1)
        rows = pl.ds(pl.multiple_of(slab * FFN_CHUNK, FFN_CHUNK), FFN_CHUNK)
        wout_ref[rows, :] = wout32_ref[...].astype(BF16)

    @pl.when(pid == n_stage - 1)
    def _():
        _post_mix_step(0, n_steps, *refs, tiles_per_seq, with_swiglu=False)

    @pl.when(pid >= n_stage)
    def _():
        _post_mix_step(pid - n_stage + 1, n_steps, *refs, tiles_per_seq, with_swiglu=True)


def _post_mix_step(step, n_steps, h_ref, oda_ref, glu_ref, halo_ref, qm_ref, mk_ref, mv_ref,
                   dww_ref, dwb_ref, lng_ref, lnb_ref, wout_ref,
                   g2_ref, wg_ref, wu_ref, wd_ref, gf_ref,
                   out_ref, cbuf_ref, shift_ref, yconv_ref, pmem_ref, h2_ref, xn_ref, act_ref,
                   tiles_per_seq, with_swiglu):
    tm = h_ref.shape[0]
    n_mem = mk_ref.shape[1]
    n_slabs = wg_ref.shape[0]
    tile = jnp.minimum(step, n_steps - 2)
    seq_start = (tile % tiles_per_seq) == 0
    base = CONV_HALO - (CONV_WIDTH - 1)

    def after(c):
        tile_ = act_ref[0:2 * SUBLANES, c * FFN_CHUNK:c * FFN_CHUNK + LANES]
        return jnp.sum(tile_.astype(F32) * 0.0, keepdims=True)

    def before_next_slab(done_bf16):
        sl = (slice(0, 2 * SUBLANES), slice(0, LANES))
        xn_ref[sl] = xn_ref[sl] + done_bf16 * jnp.zeros_like(done_bf16)

    def conv_setup(c):
        halo = halo_ref[...]
        cbuf_ref[0:CONV_HALO, :] = jnp.where(seq_start, jnp.zeros_like(halo), halo)
        cbuf_ref[CONV_HALO:CONV_HALO + tm, :] = glu_ref[...]
        span = tm + CONV_HALO - SUBLANES
        for r in range(1, SUBLANES):
            shift_ref[r - 1] = cbuf_ref[r:r + span, :]
        before_next_slab(shift_ref[SUBLANES - 2, 0:2 * SUBLANES, 0:LANES].astype(BF16))

    def conv_rows(r0, wait):
        y = jnp.zeros((CONV_ROWS, CONV_CH), F32) + wait
        for j in range(CONV_WIDTH):
            phase = (base + j) % SUBLANES
            start = base + j - phase + r0
            if phase == 0:
                window = cbuf_ref[start:start + CONV_ROWS, :]
            else:
                window = shift_ref[phase - 1, start:start + CONV_ROWS, :]
            y = y + dww_ref[j:j + 1, :] * window
        y = y + dwb_ref[...]
        mu = jnp.mean(y, axis=-1, keepdims=True)
        yc = y - mu
        var = jnp.mean(yc * yc, axis=-1, keepdims=True)
        y = yc * lax.rsqrt(var + EPS) * lng_ref[...] + lnb_ref[...]
        piece = (y * jax.nn.sigmoid(y)).astype(BF16)
        yconv_ref[r0:r0 + CONV_ROWS, :] = piece
        return piece

    def conv_pieces(r0s, c):
        wait = after(c - 1)
        for r0 in r0s:
            piece = conv_rows(r0, wait)
        before_next_slab(piece[0:2 * SUBLANES, 0:LANES])

    def mem_heads(heads, c):
        qm = qm_ref[...]
        head_of_lane = lax.broadcasted_iota(jnp.int32, qm.shape, 1) // MEM_HEAD_DIM
        wait = after(c - 1)
        for hd in heads:
            qh = jnp.where(head_of_lane == hd, qm, jnp.zeros_like(qm))
            sc = lax.dot_general(qh, mk_ref[0], (((1,), (1,)), ((), ())),
                                 preferred_element_type=F32)
            sc = sc - (jnp.max(sc, axis=-1, keepdims=True) + wait)
            p = jnp.exp(sc)
            p = (p / jnp.sum(p, axis=-1, keepdims=True)).astype(BF16)
            pmem_ref[:, hd * n_mem:(hd + 1) * n_mem] = p
        before_next_slab(p[0:2 * SUBLANES, 0:LANES])

    pieces = [conv_setup]
    pieces += [functools.partial(conv_pieces, (r0,)) for r0 in range(0, tm, CONV_ROWS)]
    pieces += [functools.partial(mem_heads, (0, 1)), functools.partial(mem_heads, (2, 3))]
    assert len(pieces) <= n_slabs, "at most one mixing piece per SwiGLU slab"
    pieces += [None] * (n_slabs - len(pieces))

    def finish_mix():
        o_mem = jnp.dot(pmem_ref[...], mv_ref[0], preferred_element_type=F32).astype(BF16)
        mix = (jnp.dot(yconv_ref[...], wout_ref[DA_WIDTH:DA_WIDTH + CONV_CH, :],
                       preferred_element_type=F32)
               + jnp.dot(o_mem, wout_ref[DA_WIDTH + CONV_CH:, :], preferred_element_type=F32))
        h2_ref[step % 2] = h2_ref[step % 2] + mix

    if not with_swiglu:
        xn_ref[0:2 * SUBLANES, 0:LANES] = jnp.zeros((2 * SUBLANES, LANES), BF16)
        act_ref[0:2 * SUBLANES, :] = jnp.zeros((2 * SUBLANES, act_ref.shape[1]), BF16)
        h2_ref[step % 2] = h_ref[...] + jnp.dot(oda_ref[...], wout_ref[0:DA_WIDTH, :],
                                                preferred_element_type=F32)
        for n, piece in enumerate(pieces):
            if piece is not None:
                piece(n)
        finish_mix()
        return

    h_prev = h2_ref[(step + 1) % 2]
    xn_ref[...] = _rms(h_prev, g2_ref[...]).astype(BF16)

    h2_ref[step % 2] = h_ref[...] + jnp.dot(oda_ref[...], wout_ref[0:DA_WIDTH, :],
                                            preferred_element_type=F32)
    for n in range(n_slabs):
        c0 = n * FFN_CHUNK
        xn = xn_ref[...]
        gate = jnp.dot(xn, wg_ref[n], preferred_element_type=F32)
        up = jnp.dot(xn, wu_ref[n], preferred_element_type=F32)
        if pieces[n] is not None:
            pieces[n](n)
        act_ref[:, c0:c0 + FFN_CHUNK] = (gate * jax.nn.sigmoid(gate) * up).astype(BF16)
    ffn = jnp.dot(act_ref[...], wd_ref[...], preferred_element_type=F32)

    finish_mix()
    out_ref[...] = _rms(h_prev + 0.5 * ffn, gf_ref[...])


def _resident(shape):
    return pl.BlockSpec(shape, lambda *_: (0,) * len(shape), pipeline_mode=pl.Buffered(1))


def _rope_tables(seq):
    half = DA_HEAD_DIM // 2
    inv_freq = 1.0 / (ROPE_THETA ** (jnp.arange(0, DA_HEAD_DIM, 2, dtype=F32) / DA_HEAD_DIM))
    ang = inv_freq[:, None] * jnp.arange(seq, dtype=F32)[None, :]
    assert ang.shape == (half, seq)
    return jnp.cos(ang), jnp.sin(ang)


def kernel(x, mem, ffn1_norm_g, ffn1_w_gate, ffn1_w_up, ffn1_w_down, mix_norm_g, mem_norm_g, w_in, lambda_q1, lambda_k1, lambda_q2, lambda_k2, subln_g, conv_dw_w, conv_dw_b, conv_ln_g, conv_ln_b, w_mem_kv, w_out, ffn2_norm_g, ffn2_w_gate, ffn2_w_up, ffn2_w_down, final_norm_g):
    b, s, d = x.shape
    n_mem = mem.shape[1]
    d_ff = ffn1_w_gate.shape[-1]
    in_width = w_in.shape[-1]
    assert ffn1_norm_g.shape[0] == 1, "single layer"
    tm = TOKEN_TILE
    ta = ATTN_TILE
    assert s % tm == 0 and s % (ATTN_QTILES * ta) == 0 and tm == ta and d_ff % FFN_CHUNK == 0
    t = b * s
    n_tiles = s // tm
    params = functools.partial(pltpu.CompilerParams, vmem_limit_bytes=VMEM_LIMIT)

    cos_t, sin_t = _rope_tables(s)
    row = lambda a: a.reshape(1, -1)

    n_stage = d_ff // FFN_CHUNK
    in_slabs = in_width // MXU_COLS
    assert FFN_CHUNK == MXU_COLS and in_slabs <= n_stage
    tile_of = lambda i: jnp.maximum(i - n_stage, 0)
    slab_of = lambda i: jnp.minimum(i, n_stage - 1)
    tok = lambda w: pl.BlockSpec((tm, w), lambda i: (tile_of(i), 0))
    col_slab = pl.BlockSpec((d, FFN_CHUNK), lambda i: (0, slab_of(i)))
    h1, q, k, vt, glu, qm = pl.pallas_call(
        _pre_mix_kernel,
        grid=(n_stage + t // tm,),
        in_specs=[tok(d), _resident((1, d)), col_slab, col_slab,
                  pl.BlockSpec((FFN_CHUNK, d), lambda i: (slab_of(i), 0)), _resident((1, d)),
                  pl.BlockSpec((d, MXU_COLS), lambda i: (0, jnp.minimum(i, in_slabs - 1))),
                  pl.BlockSpec((DA_HEAD_DIM // 2, tm), lambda i: (0, tile_of(i) % n_tiles)),
                  pl.BlockSpec((DA_HEAD_DIM // 2, tm), lambda i: (0, tile_of(i) % n_tiles))],
        out_specs=[tok(d), tok(DA_WIDTH), tok(DA_WIDTH),
                   pl.BlockSpec((1, DA_HEADS * V_AUG, tm), lambda i: (tile_of(i), 0, 0)),
                   tok(CONV_CH), tok(MEM_WIDTH)],
        out_shape=[jax.ShapeDtypeStruct((t, d), F32),
                   jax.ShapeDtypeStruct((t, DA_WIDTH), BF16),
                   jax.ShapeDtypeStruct((t, DA_WIDTH), BF16),
                   jax.ShapeDtypeStruct((t // tm, DA_HEADS * V_AUG, tm), BF16),
                   jax.ShapeDtypeStruct((t, CONV_CH), F32),
                   jax.ShapeDtypeStruct((t, MEM_WIDTH), BF16)],
        scratch_shapes=[pltpu.VMEM((n_stage, d, FFN_CHUNK), BF16),
                        pltpu.VMEM((n_stage, d, FFN_CHUNK), BF16),
                        pltpu.VMEM((d_ff, d), BF16),
                        pltpu.VMEM((in_slabs, d, MXU_COLS), BF16),
                        pltpu.VMEM((tm, d_ff), BF16)],
        compiler_params=params(dimension_semantics=("arbitrary",)),
        name="pre_mix",
    )(x.reshape(t, d), row(ffn1_norm_g), ffn1_w_gate.reshape(d, d_ff),
      ffn1_w_up.reshape(d, d_ff), ffn1_w_down.reshape(d_ff, d), row(mix_norm_g),
      w_in.reshape(d, in_width), cos_t, sin_t)

    mk, mv = pl.pallas_call(
        _mem_kv_kernel,
        grid=(b,),
        in_specs=[pl.BlockSpec((1, n_mem, d), lambda i: (i, 0, 0)), _resident((1, d)),
                  _resident((d, 2 * MEM_WIDTH))],
        out_specs=[pl.BlockSpec((1, n_mem, MEM_WIDTH), lambda i: (i, 0, 0)),
                   pl.BlockSpec((1, MEM_HEADS * n_mem, MEM_WIDTH), lambda i: (i, 0, 0))],
        out_shape=[jax.ShapeDtypeStruct((b, n_mem, MEM_WIDTH), BF16),
                   jax.ShapeDtypeStruct((b, MEM_HEADS * n_mem, MEM_WIDTH), BF16)],
        compiler_params=params(dimension_semantics=("parallel",)),
        name="mem_kv",
    )(mem, row(mem_norm_g), w_mem_kv.reshape(d, 2 * MEM_WIDTH))

    lam_spec = _resident((1, DA_HEAD_DIM))
    o_da = pl.pallas_call(
        _diff_attn_kernel,
        grid=(b, s // (ATTN_QTILES * ta)),
        in_specs=[lam_spec, lam_spec, lam_spec, lam_spec, _resident((DA_V_DIM, 1)),
                  pl.BlockSpec((1, ATTN_QTILES * ta, DA_WIDTH), lambda bi, qi: (bi, qi, 0)),
                  pl.BlockSpec((1, s, DA_WIDTH), lambda bi, qi: (bi, 0, 0)),
                  pl.BlockSpec((1, s // ta, DA_HEADS * V_AUG, ta),
                               lambda bi, qi: (bi, 0, 0, 0))],
        out_specs=pl.BlockSpec((1, ATTN_QTILES * ta, DA_WIDTH), lambda bi, qi: (bi, qi, 0)),
        out_shape=jax.ShapeDtypeStruct((b, s, DA_WIDTH), BF16),
        scratch_shapes=[pltpu.VMEM((ATTN_QTILES * 2 * DA_HEADS, 1, ta), F32),
                        pltpu.VMEM((ATTN_QTILES * 2 * DA_HEADS, V_AUG, ta), F32),
                        pltpu.VMEM((SCORE_SLOTS, ta, ta), F32),
                        pltpu.VMEM((SCORE_SLOTS, SOFTMAX_SLAB, ta), F32)],
        compiler_params=params(dimension_semantics=("parallel", "arbitrary")),
        name="diff_attn",
    )(lambda_q1, lambda_k1, lambda_q2, lambda_k2, subln_g.reshape(DA_V_DIM, 1),
      q.reshape(b, s, DA_WIDTH), k.reshape(b, s, DA_WIDTH),
      vt.reshape(b, s // ta, DA_HEADS * V_AUG, ta))

    halo_blocks = tm // CONV_HALO
    last = t // tm - 1
    out_slabs = d // FFN_CHUNK
    assert out_slabs < n_stage
    mixed_tile = lambda i: jnp.clip(i - n_stage + 1, 0, last)
    mixed = lambda w: pl.BlockSpec((tm, w), lambda i: (mixed_tile(i), 0))
    per_seq = lambda rows: pl.BlockSpec((1, rows, MEM_WIDTH),
                                        lambda i: (mixed_tile(i) // n_tiles, 0, 0))
    out = pl.pallas_call(
        functools.partial(_post_mix_kernel, tiles_per_seq=n_tiles),
        grid=(n_stage + t // tm,),
        in_specs=[mixed(d), mixed(DA_WIDTH), mixed(CONV_CH),
                  pl.BlockSpec((CONV_HALO, CONV_CH),
                               lambda i: (jnp.maximum(mixed_tile(i) * halo_blocks - 1, 0), 0)),
                  mixed(MEM_WIDTH), per_seq(n_mem), per_seq(MEM_HEADS * n_mem),
                  _resident((CONV_WIDTH, CONV_CH)), _resident((1, CONV_CH)),
                  _resident((1, CONV_CH)), _resident((1, CONV_CH)),
                  pl.BlockSpec((FFN_CHUNK, d), lambda i: (jnp.minimum(i, out_slabs - 1), 0),
                               pipeline_mode=pl.Buffered(1)),
                  _resident((1, d)), col_slab, col_slab,
                  pl.BlockSpec((FFN_CHUNK, d), lambda i: (slab_of(i), 0)), _resident((1, d))],
        out_specs=pl.BlockSpec((tm, d), lambda i: (jnp.maximum(i - n_stage, 0), 0)),
        out_shape=jax.ShapeDtypeStruct((t, d), F32),
        scratch_shapes=[pltpu.VMEM((d, d), BF16),
                        pltpu.VMEM((n_stage, d, FFN_CHUNK), BF16),
                        pltpu.VMEM((n_stage, d, FFN_CHUNK), BF16),
                        pltpu.VMEM((d_ff, d), BF16),
                        pltpu.VMEM((CONV_HALO + tm, CONV_CH), F32),
                        pltpu.VMEM((SUBLANES - 1, tm + CONV_HALO - SUBLANES, CONV_CH), F32),
                        pltpu.VMEM((tm, CONV_CH), BF16),
                        pltpu.VMEM((tm, MEM_HEADS * n_mem), BF16),
                        pltpu.VMEM((2, tm, d), F32),
                        pltpu.VMEM((tm, d), BF16),
                        pltpu.VMEM((tm, d_ff), BF16)],
        compiler_params=params(dimension_semantics=("arbitrary",)),
        name="post_mix",
    )(h1, o_da.reshape(t, DA_WIDTH), glu, glu, qm, mk, mv,
      conv_dw_w.reshape(CONV_WIDTH, CONV_CH), conv_dw_b,
      conv_ln_g, conv_ln_b, w_out.reshape(d, d), row(ffn2_norm_g),
      ffn2_w_gate.reshape(d, d_ff), ffn2_w_up.reshape(d, d_ff),
      ffn2_w_down.reshape(d_ff, d), row(final_norm_g))
    return out.reshape(b, s, d)
```
